```python
import math
import jax, jax.numpy as jnp
from jax import lax
import numpy as np

D_MODEL = 2048
BATCH = 4
SEQ = 2048
DEPTH = 4
DEC_BATCH = 8
DEC_SEQ = 8
PAST_LEN = 16384
PAGE_SIZE = 128

A_HEADS = 8
A_HEAD_DIM = D_MODEL // 16
A_WIDTH = A_HEADS * A_HEAD_DIM
MOBA_BLOCK = 256
MOBA_TOPK = 3
MOBA_Q_CHUNK = 32
REL_BUCKETS = 32
REL_MAX_DIST = 128
B_HEADS = 4
B_HEAD_DIM = D_MODEL // 8
B_WIDTH = B_HEADS * B_HEAD_DIM
MLSTM_CHUNK = 64
C_WIDTH = D_MODEL
CONV_K = 3
N_EVEN = (DEPTH + 1) // 2
N_ODD = DEPTH // 2
EVEN_SPLITS = (A_WIDTH,) * 4 + (B_WIDTH,) * 5 + (B_HEADS, B_HEADS)
EVEN_IN = sum(EVEN_SPLITS)
EVEN_MIX = A_WIDTH + B_WIDTH
ODD_IN = 4 * C_WIDTH
ALPHA = (2.0 * DEPTH) ** 0.25
BETA = (8.0 * DEPTH) ** -0.25
LN_EPS = 1e-5

kernel_name = 'moba_mlstm_shortconv_deepnorm_step'


def _split(z, sizes):
    offs = np.cumsum(sizes)[:-1]
    return jnp.split(z, [int(o) for o in offs], axis=-1)


def _layernorm(x, g, b):
    xf = x.astype(jnp.float32)
    mu = jnp.mean(xf, axis=-1, keepdims=True)
    var = jnp.mean(jnp.square(xf - mu), axis=-1, keepdims=True)
    return ((xf - mu) * lax.rsqrt(var + LN_EPS) * g.astype(jnp.float32) + b.astype(jnp.float32)).astype(x.dtype)


def _head_norm(h, g):
    mu = jnp.mean(h, axis=-1, keepdims=True)
    var = jnp.mean(jnp.square(h - mu), axis=-1, keepdims=True)
    return (h - mu) * lax.rsqrt(var + LN_EPS) * g.astype(jnp.float32).reshape(h.shape[2], h.shape[3])


def _t5_bucket(dist):
    n = jnp.maximum(dist, 0)
    max_exact = REL_BUCKETS // 2
    nf = jnp.maximum(n, 1).astype(jnp.float32)
    large = max_exact + (jnp.log(nf / max_exact) / math.log(REL_MAX_DIST / max_exact)
                         * (REL_BUCKETS - max_exact)).astype(jnp.int32)
    large = jnp.minimum(large, REL_BUCKETS - 1)
    return jnp.where(n < max_exact, n, large)


def _attend(s_own, v_own, s_sel, v_sel):
    f32 = jnp.float32
    if s_sel is None:
        p = jax.nn.softmax(s_own, axis=-1)
        return jnp.einsum('bhqk,bkhd->bhqd', p, v_own.astype(f32))
    bsz, nh, nq, n_sel, kb = s_sel.shape
    ko = s_own.shape[-1]
    p = jax.nn.softmax(jnp.concatenate([s_own, s_sel.reshape(bsz, nh, nq, n_sel * kb)], axis=-1), axis=-1)
    o = jnp.einsum('bhqk,bkhd->bhqd', p[..., :ko], v_own.astype(f32))
    o = o + jnp.einsum('bhqnk,bhqnkd->bhqd', p[..., ko:].reshape(bsz, nh, nq, n_sel, kb), v_sel.astype(f32))
    return o


def _moba_prompt(q, k, v, rel_bias):
    f32 = jnp.float32
    bsz, T, H, dh = q.shape
    nb = -(-T // MOBA_BLOCK)
    pad = nb * MOBA_BLOCK - T
    kb = jnp.pad(k, ((0, 0), (0, pad), (0, 0), (0, 0))).reshape(bsz, nb, MOBA_BLOCK, H, dh)
    vb = jnp.pad(v, ((0, 0), (0, pad), (0, 0), (0, 0))).reshape(bsz, nb, MOBA_BLOCK, H, dh)
    k_mean = jnp.mean(kb.astype(f32), axis=2)
    pos = jnp.arange(T, dtype=jnp.int32)
    gate = jnp.einsum('bthd,bnhd->bhtn', q.astype(f32), k_mean)
    fully_past = jnp.arange(nb, dtype=jnp.int32)[None, :] < (pos // MOBA_BLOCK)[:, None]
    gate = jnp.where(fully_past, gate, -jnp.inf)
    n_sel = max(1, min(MOBA_TOPK, nb - 1))
    sel_val, sel_idx = lax.top_k(gate, n_sel)
    sel_ok = jnp.isfinite(sel_val)
    qc = math.gcd(T, MOBA_Q_CHUNK)
    n_chunks = T // qc
    q_c = jnp.moveaxis(q.reshape(bsz, n_chunks, qc, H, dh), 1, 0)
    idx_c = jnp.moveaxis(sel_idx.reshape(bsz, H, n_chunks, qc, n_sel), 2, 0)
    ok_c = jnp.moveaxis(sel_ok.reshape(bsz, H, n_chunks, qc, n_sel), 2, 0)
    bi = jnp.arange(bsz)[:, None, None, None]
    hi = jnp.arange(H)[None, :, None, None]
    hi5 = hi[..., None]
    table = rel_bias.astype(f32)
    scale = dh ** -0.5
    offs = jnp.arange(MOBA_BLOCK, dtype=jnp.int32)

    def one_chunk(args):
        c, qch, idx, ok = args
        start = c * qc
        pos_q = start + jnp.arange(qc, dtype=jnp.int32)
        own = start // MOBA_BLOCK
        qh = jnp.swapaxes(qch, 1, 2).astype(f32) * scale
        k_own = lax.dynamic_index_in_dim(kb, own, axis=1, keepdims=False)
        v_own = lax.dynamic_index_in_dim(vb, own, axis=1, keepdims=False)
        d_own = pos_q[:, None] - (own * MOBA_BLOCK + offs)[None, :]
        s_own = jnp.einsum('bhqd,bkhd->bhqk', qh, k_own) + table[:, _t5_bucket(d_own)][None]
        s_own = jnp.where(d_own >= 0, s_own, -jnp.inf)
        k_sel = kb[bi, idx, :, hi, :]
        v_sel = vb[bi, idx, :, hi, :]
        d_sel = pos_q[None, None, :, None, None] - (idx[..., None] * MOBA_BLOCK + offs)
        s_sel = jnp.einsum('bhqd,bhqnkd->bhqnk', qh, k_sel) + table[hi5, _t5_bucket(d_sel)]
        s_sel = jnp.where(ok[..., None], s_sel, -jnp.inf)
        return jnp.swapaxes(_attend(s_own, v_own, s_sel, v_sel), 1, 2)

    out = lax.map(one_chunk, (jnp.arange(n_chunks, dtype=jnp.int32), q_c, idx_c, ok_c))
    return jnp.moveaxis(out, 0, 1).reshape(bsz, T, H, dh)


def _moba_sample(q, k, v, k_pool, v_pool, layer, page_table, rel_bias):
    f32 = jnp.float32
    bsz, S, H, dh = q.shape
    n_pages = page_table.shape[1]
    past = n_pages * PAGE_SIZE
    nbp = past // MOBA_BLOCK
    own_start = nbp * MOBA_BLOCK
    n_oc = past - own_start
    ppb = MOBA_BLOCK // PAGE_SIZE
    table = rel_bias.astype(f32)
    scale = dh ** -0.5
    pos_q = past + jnp.arange(S, dtype=jnp.int32)
    qh = jnp.swapaxes(q, 1, 2).astype(f32) * scale
    k_past = k_pool[layer, page_table].reshape(bsz, past, H, dh)
    k_own = jnp.concatenate([k_past[:, own_start:], k.astype(k_past.dtype)], axis=1)
    v_oc = v_pool[layer, page_table[:, own_start // PAGE_SIZE:]].reshape(bsz, n_oc, H, dh)
    v_own = jnp.concatenate([v_oc, v.astype(v_oc.dtype)], axis=1)
    d_own = pos_q[:, None] - (own_start + jnp.arange(n_oc + S, dtype=jnp.int32))[None, :]
    s_own = jnp.einsum('bhqd,bkhd->bhqk', qh, k_own) + table[:, _t5_bucket(d_own)][None]
    s_own = jnp.where(d_own >= 0, s_own, -jnp.inf)
    if nbp == 0:
        return jnp.swapaxes(_attend(s_own, v_own, None, None), 1, 2)
    n_sel = min(MOBA_TOPK, nbp)
    kb = k_past[:, :own_start].reshape(bsz, nbp, MOBA_BLOCK, H, dh)
    k_mean = jnp.mean(kb.astype(f32), axis=2)
    gate = jnp.einsum('bshd,bnhd->bhsn', q.astype(f32), k_mean)
    _, idx = lax.top_k(gate, n_sel)
    bi = jnp.arange(bsz)[:, None, None, None]
    hi = jnp.arange(H)[None, :, None, None]
    hi5 = hi[..., None]
    k_sel = kb[bi, idx, :, hi, :]
    phys = page_table[bi[..., None], idx[..., None] * ppb + jnp.arange(ppb, dtype=jnp.int32)]
    v_sel = v_pool[layer, phys, :, hi5, :].reshape(bsz, H, S, n_sel, MOBA_BLOCK, dh)
    d_sel = pos_q[None, None, :, None, None] - (idx[..., None] * MOBA_BLOCK + jnp.arange(MOBA_BLOCK, dtype=jnp.int32))
    s_sel = jnp.einsum('bhqd,bhqnkd->bhqnk', qh, k_sel) + table[hi5, _t5_bucket(d_sel)]
    return jnp.swapaxes(_attend(s_own, v_own, s_sel, v_sel), 1, 2)


def _to_chunks(a, nc, L):
    bsz, T, H = a.shape[:3]
    a = a.reshape((bsz, nc, L, H) + a.shape[3:])
    return jnp.moveaxis(a, (1, 3), (0, 2))


def _mlstm(q, k, v, i_pre, f_pre, c0, n0, m0):
    f32 = jnp.float32
    bsz, T, H, dk = q.shape
    L = math.gcd(T, MLSTM_CHUNK)
    nc = T // L
    qs = _to_chunks(q.astype(f32), nc, L)
    ks = _to_chunks(k.astype(f32) * (dk ** -0.5), nc, L)
    vs = _to_chunks(v.astype(f32), nc, L)
    lf = _to_chunks(jax.nn.log_sigmoid(f_pre.astype(f32)), nc, L)
    li = _to_chunks(i_pre.astype(f32), nc, L)
    causal = jnp.tril(jnp.ones((L, L), dtype=bool))

    def step(carry, inp):
        c, n, m = carry
        qc, kc, vc, lfc, lic = inp
        b = jnp.cumsum(lfc, axis=-1)
        log_d = jnp.where(causal, b[..., :, None] - b[..., None, :] + lic[..., None, :], -jnp.inf)
        m_inter = b + m[..., None]
        m_t = jnp.maximum(m_inter, jnp.max(log_d, axis=-1))
        s = jnp.einsum('bhtd,bhsd->bhts', qc, kc) * jnp.exp(log_d - m_t[..., None])
        w_inter = jnp.exp(m_inter - m_t)
        num = jnp.einsum('bhts,bhse->bhte', s, vc) + w_inter[..., None] * jnp.einsum('bhtd,bhde->bhte', qc, c)
        den = jnp.sum(s, axis=-1) + w_inter * jnp.einsum('bhtd,bhd->bht', qc, n)
        h = num / jnp.maximum(jnp.abs(den), jnp.exp(-m_t))[..., None]
        m_new = m_t[..., -1]
        decay = jnp.exp(b[..., -1] + m - m_new)
        wk = jnp.exp(b[..., -1:] - b + lic - m_new[..., None])
        c_new = decay[..., None, None] * c + jnp.einsum('bhsd,bhse->bhde', kc * wk[..., None], vc)
        n_new = decay[..., None] * n + jnp.einsum('bhs,bhsd->bhd', wk, kc)
        return (c_new, n_new, m_new), h

    (c1, n1, m1), hs = lax.scan(step, (c0.astype(f32), n0.astype(f32), m0.astype(f32)), (qs, ks, vs, lf, li))
    h = jnp.moveaxis(hs, (0, 2), (1, 3)).reshape(bsz, T, H, v.shape[-1])
    return h, (c1, n1, m1)


def _short_conv(u, buf, w):
    T = u.shape[1]
    uf = jnp.concatenate([buf.astype(u.dtype), u], axis=1)
    y = uf[:, 0:T] * w[0]
    for j in range(1, CONV_K):
        y = y + uf[:, j:j + T] * w[j]
    return y, uf[:, T:]


def _even_layer(x, w_in, w_out, gate_bias, norm_g, rel_bias, past):
    f32 = jnp.float32
    bsz, T, _ = x.shape
    z = jnp.einsum('btd,de->bte', x, w_in)
    qa, ka, va, ga, qm, km, vm, om, gm, im, fm = _split(z, EVEN_SPLITS)
    qa, ka, va = [a.reshape(bsz, T, A_HEADS, A_HEAD_DIM) for a in (qa, ka, va)]
    qm, km, vm = [a.reshape(bsz, T, B_HEADS, B_HEAD_DIM) for a in (qm, km, vm)]
    if past is None:
        attn = _moba_prompt(qa, ka, va, rel_bias)
        c0 = jnp.zeros((bsz, B_HEADS, B_HEAD_DIM, B_HEAD_DIM), f32)
        n0 = jnp.zeros((bsz, B_HEADS, B_HEAD_DIM), f32)
        m0 = jnp.zeros((bsz, B_HEADS), f32)
    else:
        k_pool, v_pool, layer, page_table, c0, n0, m0 = past
        attn = _moba_sample(qa, ka, va, k_pool, v_pool, layer, page_table, rel_bias)
    hm, (c1, n1, m1) = _mlstm(qm, km, vm, im + gate_bias[:B_HEADS], fm + gate_bias[B_HEADS:], c0, n0, m0)
    hm = _head_norm(hm, norm_g).reshape(bsz, T, B_WIDTH) * jax.nn.sigmoid(om.astype(f32))
    mix = jnp.concatenate([attn.reshape(bsz, T, A_WIDTH) * jax.nn.silu(ga.astype(f32)),
                           hm * jax.nn.silu(gm.astype(f32))], axis=-1).astype(x.dtype)
    return jnp.einsum('bte,ed->btd', mix, w_out), (ka, va, c1, n1, m1)


def _odd_layer(x, w_in, w_out, conv_w, buf):
    bsz = x.shape[0]
    z = jnp.einsum('btd,de->bte', x, w_in)
    b_gate, c_gate, xin, g = _split(z, (C_WIDTH,) * 4)
    u = c_gate * xin
    if buf is None:
        buf = jnp.zeros((bsz, CONV_K - 1, C_WIDTH), u.dtype)
    y, new_buf = _short_conv(u, buf, conv_w)
    mix = (b_gate * y * jax.nn.silu(g)).astype(x.dtype)
    return jnp.einsum('bte,ed->btd', mix, w_out), new_buf


def setup_inputs(seed: int = 0) -> dict:
    key = jax.random.key(seed)
    ks = jax.random.split(key, 20)
    nrm = jax.random.normal
    f32 = jnp.float32
    n_pages = PAST_LEN // PAGE_SIZE
    n_pool = (DEC_BATCH * n_pages * 5) // 4
    x_prompt = nrm(ks[0], (BATCH, SEQ, D_MODEL), f32)
    x_sample = nrm(ks[1], (DEC_BATCH, DEC_SEQ, D_MODEL), f32)
    cache_k = nrm(ks[2], (N_EVEN, n_pool, PAGE_SIZE, A_HEADS, A_HEAD_DIM), f32)
    cache_v = nrm(ks[3], (N_EVEN, n_pool, PAGE_SIZE, A_HEADS, A_HEAD_DIM), f32)
    page_table = jax.random.permutation(ks[4], n_pool)[:DEC_BATCH * n_pages].reshape(DEC_BATCH, n_pages).astype(jnp.int32)
    state_C = 0.1 * nrm(ks[5], (N_EVEN, DEC_BATCH, B_HEADS, B_HEAD_DIM, B_HEAD_DIM), f32)
    state_n = 0.5 * nrm(ks[6], (N_EVEN, DEC_BATCH, B_HEADS, B_HEAD_DIM), f32)
    state_m = nrm(ks[7], (N_EVEN, DEC_BATCH, B_HEADS), f32)
    state_conv = nrm(ks[8], (N_ODD, DEC_BATCH, CONV_K - 1, C_WIDTH), f32)
    w_in_even = nrm(ks[9], (N_EVEN, D_MODEL, EVEN_IN), f32) * D_MODEL ** -0.5
    w_out_even = nrm(ks[10], (N_EVEN, EVEN_MIX, D_MODEL), f32) * (EVEN_MIX ** -0.5 * BETA)
    i_bias = 0.1 * nrm(ks[11], (N_EVEN, B_HEADS), f32)
    f_bias = jnp.linspace(3.0, 6.0, B_HEADS, dtype=f32)[None, :] + 0.1 * nrm(ks[12], (N_EVEN, B_HEADS), f32)
    mlstm_gate_bias = jnp.concatenate([i_bias, f_bias], axis=-1)
    mlstm_norm_g = 1.0 + 0.05 * nrm(ks[13], (N_EVEN, B_WIDTH), f32)
    rel_bias = 0.5 * nrm(ks[14], (A_HEADS, REL_BUCKETS), f32)
    w_in_odd = nrm(ks[15], (N_ODD, D_MODEL, ODD_IN), f32) * D_MODEL ** -0.5
    w_out_odd = nrm(ks[16], (N_ODD, C_WIDTH, D_MODEL), f32) * (C_WIDTH ** -0.5 * BETA)
    conv_w = nrm(ks[17], (N_ODD, CONV_K, C_WIDTH), f32) * CONV_K ** -0.5
    ln_g = 1.0 + 0.05 * nrm(ks[18], (DEPTH, D_MODEL), f32)
    ln_b = 0.02 * nrm(ks[19], (DEPTH, D_MODEL), f32)
    return {'x_prompt': x_prompt, 'x_sample': x_sample, 'cache_k': cache_k, 'cache_v': cache_v,
            'page_table': page_table, 'state_C': state_C, 'state_n': state_n, 'state_m': state_m,
            'state_conv': state_conv, 'w_in_even': w_in_even, 'w_out_even': w_out_even,
            'mlstm_gate_bias': mlstm_gate_bias, 'mlstm_norm_g': mlstm_norm_g, 'rel_bias': rel_bias,
            'w_in_odd': w_in_odd, 'w_out_odd': w_out_odd, 'conv_w': conv_w, 'ln_g': ln_g, 'ln_b': ln_b}


def reference(x_prompt, x_sample, cache_k, cache_v, page_table, state_C, state_n, state_m, state_conv,
              w_in_even, w_out_even, mlstm_gate_bias, mlstm_norm_g, rel_bias,
              w_in_odd, w_out_odd, conv_w, ln_g, ln_b):
    xp, xs = x_prompt, x_sample
    pk, pv, pc, pn, pm, pb = [], [], [], [], [], []
    sk, sv, sc, sn, sm, sb = [], [], [], [], [], []
    for layer in range(DEPTH):
        if layer % 2 == 0:
            e = layer // 2
            yp, (k1, v1, c1, n1, m1) = _even_layer(xp, w_in_even[e], w_out_even[e], mlstm_gate_bias[e],
                                                   mlstm_norm_g[e], rel_bias, None)
            ys, (k2, v2, c2, n2, m2) = _even_layer(xs, w_in_even[e], w_out_even[e], mlstm_gate_bias[e],
                                                   mlstm_norm_g[e], rel_bias,
                                                   (cache_k, cache_v, e, page_table, state_C[e], state_n[e], state_m[e]))
            pk.append(k1); pv.append(v1); pc.append(c1); pn.append(n1); pm.append(m1)
            sk.append(k2); sv.append(v2); sc.append(c2); sn.append(n2); sm.append(m2)
        else:
            o = layer // 2
            yp, b1 = _odd_layer(xp, w_in_odd[o], w_out_odd[o], conv_w[o], None)
            ys, b2 = _odd_layer(xs, w_in_odd[o], w_out_odd[o], conv_w[o], state_conv[o])
            pb.append(b1); sb.append(b2)
        xp = _layernorm(ALPHA * xp + yp, ln_g[layer], ln_b[layer])
        xs = _layernorm(ALPHA * xs + ys, ln_g[layer], ln_b[layer])
    return (xp, xs,
            jnp.stack(pk), jnp.stack(pv), jnp.stack(pc), jnp.stack(pn), jnp.stack(pm), jnp.stack(pb),
            jnp.stack(sk), jnp.stack(sv), jnp.stack(sc), jnp.stack(sn), jnp.stack(sm), jnp.stack(sb))
```

```python
import functools
import math

import numpy as np
import jax
import jax.numpy as jnp
from jax import lax
from jax.experimental import pallas as pl
from jax.experimental.pallas import tpu as pltpu

F32 = jnp.float32
BF16 = jnp.bfloat16

D_MODEL = 2048
DEPTH = 4
PAGE_SIZE = 128
A_HEADS = 8
A_HEAD_DIM = 128
A_WIDTH = A_HEADS * A_HEAD_DIM
MOBA_BLOCK = 256
MOBA_TOPK = 3
REL_BUCKETS = 32
REL_MAX_DIST = 128
B_HEADS = 4
B_HEAD_DIM = 256
B_WIDTH = B_HEADS * B_HEAD_DIM
MLSTM_CHUNK = 64
MLSTM_PROMPT_CHUNK = 256
C_WIDTH = D_MODEL
CONV_K = 3
N_GROUPS_EVEN = 9
EVEN_MAIN = N_GROUPS_EVEN * 1024
ALPHA = (2.0 * DEPTH) ** 0.25
LN_EPS = 1e-5

LANES = 128
SUBLANES = 8
VMEM_LIMIT = 48 * 1024 * 1024

NEG = -1e30


def _params(*sem):
    return pltpu.CompilerParams(dimension_semantics=sem, vmem_limit_bytes=VMEM_LIMIT)


def _nt(a, b):
    return lax.dot_general(a, b, (((1,), (1,)), ((), ())), preferred_element_type=F32)


def _silu(x):
    return x * jax.nn.sigmoid(x)


def _bucket_np(d):
    d = np.asarray(d, np.int64)
    n = np.maximum(d, 0)
    max_exact = REL_BUCKETS // 2
    nf = np.maximum(n, 1).astype(np.float64)
    large = max_exact + (np.log(nf / max_exact) / math.log(REL_MAX_DIST / max_exact)
                         * (REL_BUCKETS - max_exact)).astype(np.int64)
    large = np.minimum(large, REL_BUCKETS - 1)
    out = np.where(n < max_exact, n, large)
    return np.where(d < 0, -1, out).astype(np.int32)


def _bias_kernel(tab_ref, idx_ref, o_ref):
    h = pl.program_id(0)
    idx = idx_ref[...]
    out = jnp.full(idx.shape, NEG, F32)
    for b in range(REL_BUCKETS):
        out = jnp.where(idx == b, tab_ref[h, b], out)
    o_ref[...] = out


def _bias_tiles(rel_bias, idx):
    r, c = idx.shape
    return pl.pallas_call(
        _bias_kernel,
        grid=(A_HEADS,),
        in_specs=[pl.BlockSpec(memory_space=pltpu.SMEM),
                  pl.BlockSpec((r, c), lambda h: (0, 0))],
        out_specs=pl.BlockSpec((None, r, c), lambda h: (h, 0, 0)),
        out_shape=jax.ShapeDtypeStruct((A_HEADS, r, c), F32),
        compiler_params=_params("parallel"),
        name="bias_tiles",
    )(rel_bias.astype(F32), jnp.asarray(idx))


def _prompt_bias_idx():
    qi = np.arange(MOBA_BLOCK)[:, None]
    kj = np.arange(2 * MOBA_BLOCK)[None, :]
    return _bucket_np(qi + MOBA_BLOCK - kj)


def _sample_bias_idx(s_len):
    s = np.arange(s_len)[:, None]
    lane = np.arange(2 * PAGE_SIZE * A_HEADS)[None, :]
    off = lane // A_HEADS
    far = _bucket_np(MOBA_BLOCK + s - off)
    lo = np.arange(LANES)[None, :]
    own = _bucket_np(np.where(lo < A_HEADS * s_len, s - (lo % s_len), -1))
    return np.concatenate([far, own], axis=1)


def _matmul_kernel(x_ref, w_ref, o_ref):
    o_ref[...] = jnp.dot(x_ref[...], w_ref[...], preferred_element_type=F32).astype(o_ref.dtype)


def _inproj_even(x_bf, w_bf, tm, tn=512):
    m, k = x_bf.shape
    per = 1024 // tn
    return pl.pallas_call(
        _matmul_kernel,
        grid=(m // tm, N_GROUPS_EVEN * per),
        in_specs=[pl.BlockSpec((tm, k), lambda i, j: (i, 0)),
                  pl.BlockSpec((k, tn), lambda i, j: (0, j))],
        out_specs=pl.BlockSpec((None, tm, tn), lambda i, j: (j // per, i, j % per)),
        out_shape=jax.ShapeDtypeStruct((N_GROUPS_EVEN, m, 1024), F32),
        compiler_params=_params("parallel", "arbitrary"),
        name="inproj_even",
    )(x_bf, w_bf)


def _gates_kernel(x_ref, w_ref, b_ref, o_ref):
    o_ref[...] = jnp.dot(x_ref[...], w_ref[...], preferred_element_type=F32) + b_ref[...]


def _gates(x_bf, wg_bf, gbias, tm):
    m, k = x_bf.shape
    return pl.pallas_call(
        _gates_kernel,
        grid=(m // tm,),
        in_specs=[pl.BlockSpec((tm, k), lambda i: (i, 0)),
                  pl.BlockSpec((k, LANES), lambda i: (0, 0)),
                  pl.BlockSpec((1, LANES), lambda i: (0, 0))],
        out_specs=pl.BlockSpec((tm, LANES), lambda i: (i, 0)),
        out_shape=jax.ShapeDtypeStruct((m, LANES), F32),
        compiler_params=_params("parallel"),
        name="mlstm_gates",
    )(x_bf, wg_bf, gbias)


def _moba_prompt_kernel(q_ref, k_ref, v_ref, g_ref, bias_ref, o_ref,
                        kmean_ref, m_ref, l_ref, acc_ref, *, nblk, scale):
    blk = MOBA_BLOCK
    i = pl.program_id(2)

    @pl.when(i == 0)
    def _():
        kmean_ref[...] = jnp.zeros_like(kmean_ref)
        for j in range(nblk):
            kmean_ref[j:j + 1, :] = jnp.mean(k_ref[j * blk:(j + 1) * blk, :], axis=0, keepdims=True)

    q = q_ref[...]
    gate = lax.dot_general(q, kmean_ref[...], (((1,), (1,)), ((), ())),
                           precision=lax.Precision.HIGHEST, preferred_element_type=F32)
    lane = lax.broadcasted_iota(jnp.int32, gate.shape, 1)
    valid = lane < i
    sel_cols = []
    for j in range(nblk - 1):
        gj = gate[:, j:j + 1]
        beats = ((gate > gj) | ((gate == gj) & (lane < j))) & valid
        rank = jnp.sum(beats.astype(F32), axis=1, keepdims=True)
        sel_cols.append(rank < MOBA_TOPK)

    m_ref[...] = jnp.full_like(m_ref, NEG)
    l_ref[...] = jnp.zeros_like(l_ref)
    acc_ref[...] = jnp.zeros_like(acc_ref)
    qs = (q * scale).astype(BF16)
    far_bias = bias_ref[0:1, 0:1]

    for j in range(nblk):
        @pl.when(j <= i)
        def _(j=j):
            kj = k_ref[j * blk:(j + 1) * blk, :].astype(BF16)
            vj = v_ref[j * blk:(j + 1) * blk, :].astype(BF16)
            s = _nt(qs, kj)
            diff = jnp.zeros((blk, blk), jnp.int32) + (i - j)
            own = diff == 0
            bias = jnp.where(own, bias_ref[:, blk:2 * blk],
                             jnp.where(diff == 1, bias_ref[:, 0:blk], far_bias))
            keep = (sel_cols[j] | own) if j < nblk - 1 else own
            s = jnp.where(keep, s + bias, NEG)
            m_old = m_ref[...]
            m_new = jnp.maximum(m_old, jnp.max(s, axis=1, keepdims=True))
            alpha = jnp.exp(m_old - m_new)
            p = jnp.exp(s - m_new)
            l_ref[...] = alpha * l_ref[...] + jnp.sum(p, axis=1, keepdims=True)
            acc_ref[...] = alpha * acc_ref[...] + jnp.dot(p.astype(BF16), vj, preferred_element_type=F32)
            m_ref[...] = m_new

    g = g_ref[...]
    o_ref[...] = (acc_ref[...] / l_ref[...] * _silu(g)).astype(o_ref.dtype)


def _moba_prompt(z9, bias_p, bsz, t):
    blk = MOBA_BLOCK
    nblk = t // blk
    dh = A_HEAD_DIM
    kern = functools.partial(_moba_prompt_kernel, nblk=nblk, scale=dh ** -0.5)
    return pl.pallas_call(
        kern,
        grid=(bsz, A_HEADS, nblk),
        in_specs=[pl.BlockSpec((None, None, blk, dh), lambda b, h, i: (0, b, i, h)),
                  pl.BlockSpec((None, None, t, dh), lambda b, h, i: (1, b, 0, h)),
                  pl.BlockSpec((None, None, t, dh), lambda b, h, i: (2, b, 0, h)),
                  pl.BlockSpec((None, None, blk, dh), lambda b, h, i: (3, b, i, h)),
                  pl.BlockSpec((None, blk, 2 * blk), lambda b, h, i: (h, 0, 0))],
        out_specs=pl.BlockSpec((None, blk, dh), lambda b, h, i: (b, i, h)),
        out_shape=jax.ShapeDtypeStruct((bsz, t, A_WIDTH), BF16),
        scratch_shapes=[pltpu.VMEM((LANES, dh), F32),
                        pltpu.VMEM((blk, 1), F32),
                        pltpu.VMEM((blk, 1), F32),
                        pltpu.VMEM((blk, dh), F32)],
        compiler_params=_params("parallel", "parallel", "arbitrary"),
        name="moba_prompt",
    )(z9, z9, z9, z9, bias_p)


def _stack_heads(x):
    return jnp.concatenate([x[:, h * A_HEAD_DIM:(h + 1) * A_HEAD_DIM] for h in range(A_HEADS)], axis=0)


def _moba_sample_kernel(pt_ref, q_ref, kn_ref, vn_ref, ga_ref, bias_ref,
                        k0_ref, k1_ref, v0_ref, v1_ref, o_ref,
                        qm_ref, gate_ref, m_ref, l_ref, oblk_ref, *, nb, s_len, scale):
    del pt_ref
    n = pl.program_id(1)
    rows = A_HEADS * s_len
    s_shift = s_len.bit_length() - 1
    far_w = PAGE_SIZE * A_HEADS

    @pl.when(n == 0)
    def _():
        qm_ref[...] = _stack_heads(q_ref[...])
        gate_ref[...] = jnp.full_like(gate_ref, NEG)
        m_ref[...] = jnp.full_like(m_ref, NEG)
        l_ref[...] = jnp.zeros_like(l_ref)

    qb = qm_ref[...].astype(BF16)
    row = lax.broadcasted_iota(jnp.int32, (rows, far_w), 0)
    lane = lax.broadcasted_iota(jnp.int32, (rows, far_w), 1)
    useful = (lane & (A_HEADS - 1)) == (row >> s_shift)
    lane_b = lax.broadcasted_iota(jnp.int32, (rows, LANES), 1)

    raw0 = _nt(qb, k0_ref[...].reshape(far_w, A_HEAD_DIM).astype(BF16))
    raw1 = _nt(qb, k1_ref[...].reshape(far_w, A_HEAD_DIM).astype(BF16))
    gsum = jnp.sum(jnp.where(useful, raw0 + raw1, 0.0), axis=1, keepdims=True)
    gate_ref[...] = jnp.where(lane_b == n, gsum * (1.0 / MOBA_BLOCK), gate_ref[...])

    last = n == nb - 1
    last_v = (jnp.zeros((rows, far_w), jnp.int32) + n) == nb - 1
    far_bias = bias_ref[:, 0:1]
    b0 = jnp.where(last_v, bias_ref[:, 0:far_w], far_bias)
    b1 = jnp.where(last_v, bias_ref[:, far_w:2 * far_w], far_bias)
    s0 = jnp.where(useful, raw0 * scale + b0, NEG)
    s1 = jnp.where(useful, raw1 * scale + b1, NEG)
    mloc = jnp.maximum(jnp.max(s0, axis=1, keepdims=True), jnp.max(s1, axis=1, keepdims=True))
    p0 = jnp.exp(s0 - mloc)
    p1 = jnp.exp(s1 - mloc)
    lloc = jnp.sum(p0, axis=1, keepdims=True) + jnp.sum(p1, axis=1, keepdims=True)
    oloc = (jnp.dot(p0.astype(BF16), v0_ref[...].reshape(far_w, A_HEAD_DIM).astype(BF16),
                    preferred_element_type=F32)
            + jnp.dot(p1.astype(BF16), v1_ref[...].reshape(far_w, A_HEAD_DIM).astype(BF16),
                      preferred_element_type=F32))
    m_ref[...] = jnp.where(lane_b == n, mloc, m_ref[...])
    l_ref[...] = jnp.where(lane_b == n, lloc, l_ref[...])
    oblk_ref[n] = oloc

    @pl.when(last)
    def _():
        row_b = lax.broadcasted_iota(jnp.int32, (rows, LANES), 0)
        lane_f = lane_b.astype(F32)
        g = jnp.where(lane_b < nb, gate_ref[...], -jnp.inf)
        sel = jnp.zeros((rows, LANES), jnp.bool_)
        for _r in range(min(MOBA_TOPK, nb)):
            mx = jnp.max(g, axis=1, keepdims=True)
            idx = jnp.min(jnp.where(g == mx, lane_f, float(LANES)), axis=1, keepdims=True)
            hit = lane_f == idx
            sel = sel | hit
            g = jnp.where(hit, -jnp.inf, g)
        pad = jnp.zeros((LANES - rows, A_HEAD_DIM), F32)
        kn = jnp.concatenate([_stack_heads(kn_ref[...]), pad], axis=0).astype(BF16)
        vn = jnp.concatenate([_stack_heads(vn_ref[...]), pad], axis=0).astype(BF16)
        so = _nt(qb, kn) * scale + bias_ref[:, 2 * far_w:2 * far_w + LANES]
        ok = ((lane_b < rows) & ((lane_b >> s_shift) == (row_b >> s_shift))
              & ((lane_b & (s_len - 1)) <= (row_b & (s_len - 1))))
        so = jnp.where(ok, so, NEG)
        mm = m_ref[...]
        mtot = jnp.maximum(jnp.max(jnp.where(sel, mm, NEG), axis=1, keepdims=True),
                           jnp.max(so, axis=1, keepdims=True))
        w = jnp.where(sel, jnp.exp(mm - mtot), 0.0)
        po = jnp.exp(so - mtot)
        lsum = jnp.sum(w * l_ref[...], axis=1, keepdims=True) + jnp.sum(po, axis=1, keepdims=True)
        acc = jnp.dot(po.astype(BF16), vn, preferred_element_type=F32)
        for nn in range(nb):
            acc = acc + w[:, nn:nn + 1] * oblk_ref[nn]
        out = acc / lsum
        res = jnp.concatenate([out[h * s_len:(h + 1) * s_len, :] for h in range(A_HEADS)], axis=1)
        o_ref[...] = (res * _silu(ga_ref[...])).astype(o_ref.dtype)


def _moba_sample(z9, bias_s, cache_k, cache_v, layer, page_table, bsz, s_len):
    n_pages = page_table.shape[1]
    ppb = MOBA_BLOCK // PAGE_SIZE
    assert ppb == 2 and n_pages % ppb == 0 and s_len == SUBLANES and A_HEADS == SUBLANES
    nb = n_pages // ppb
    assert nb <= LANES
    rows = A_HEADS * s_len
    dh = A_HEAD_DIM
    kern = functools.partial(_moba_sample_kernel, nb=nb, s_len=s_len, scale=dh ** -0.5)
    tok = lambda g: pl.BlockSpec((None, None, s_len, A_WIDTH), lambda b, n, pt: (g, b, 0, 0))
    page = lambda o: pl.BlockSpec((None, None, PAGE_SIZE, A_HEADS, dh),
                                  lambda b, n, pt: (layer, pt[b, ppb * n + o], 0, 0, 0))
    grid_spec = pltpu.PrefetchScalarGridSpec(
        num_scalar_prefetch=1,
        grid=(bsz, nb),
        in_specs=[tok(0), tok(1), tok(2), tok(3),
                  pl.BlockSpec(bias_s.shape, lambda b, n, pt: (0, 0)),
                  page(0), page(1), page(0), page(1)],
        out_specs=pl.BlockSpec((None, s_len, A_WIDTH), lambda b, n, pt: (b, 0, 0)),
        scratch_shapes=[pltpu.VMEM((rows, dh), F32),
                        pltpu.VMEM((rows, LANES), F32),
                        pltpu.VMEM((rows, LANES), F32),
                        pltpu.VMEM((rows, LANES), F32),
                        pltpu.VMEM((nb, rows, dh), F32)])
    return pl.pallas_call(
        kern,
        grid_spec=grid_spec,
        out_shape=jax.ShapeDtypeStruct((bsz, s_len, A_WIDTH), BF16),
        compiler_params=_params("parallel", "arbitrary"),
        name="moba_sample",
    )(page_table, z9, z9, z9, z9, bias_s, cache_k, cache_k, cache_v, cache_v)


def _mlstm_kernel(q_ref, k_ref, v_ref, og_ref, gm_ref, g_ref, gt_ref, ng_ref, c0_ref, n0_ref, m0_ref,
                  h_ref, c_out, n_out, m_out, c_scr, n_scr, m_scr, *, L, Lp):
    hh = pl.program_id(1)
    c = pl.program_id(2)
    dk = B_HEAD_DIM

    @pl.when(c == 0)
    def _():
        c_scr[...] = c0_ref[...]
        n_scr[...] = n0_ref[...]
        m_scr[...] = m0_ref[...]

    def padded(x):
        if L == Lp:
            return x
        return jnp.concatenate([x, jnp.zeros((Lp - L, x.shape[1]), x.dtype)], axis=0)

    q = padded(q_ref[...])
    k = padded(k_ref[...]) * (dk ** -0.5)
    v = padded(v_ref[...])
    g = padded(g_ref[...])
    lane = lax.broadcasted_iota(jnp.int32, g.shape, 1)
    li_col = jnp.sum(jnp.where(lane == hh, g, 0.0), axis=1, keepdims=True)
    fp_col = jnp.sum(jnp.where(lane == hh + B_HEADS, g, 0.0), axis=1, keepdims=True)
    li_row = gt_ref[pl.ds(hh, 1), :]
    fp_row = gt_ref[pl.ds(hh + B_HEADS, 1), :]
    lf_col = jax.nn.log_sigmoid(fp_col)
    lf_row = jax.nn.log_sigmoid(fp_row)
    if L != Lp:
        rv = lax.broadcasted_iota(jnp.int32, (Lp, 1), 0) < L
        cv = lax.broadcasted_iota(jnp.int32, (1, Lp), 1) < L
        li_col = jnp.where(rv, li_col, NEG)
        lf_col = jnp.where(rv, lf_col, 0.0)
        li_row = jnp.where(cv, li_row, NEG)
        lf_row = jnp.where(cv, lf_row, 0.0)

    r = lax.broadcasted_iota(jnp.int32, (Lp, Lp), 0)
    cc = lax.broadcasted_iota(jnp.int32, (Lp, Lp), 1)
    tri = r >= cc
    b_col = jnp.sum(jnp.where(tri, lf_row, 0.0), axis=1, keepdims=True)
    b_row = jnp.sum(jnp.where(r <= cc, lf_col, 0.0), axis=0, keepdims=True)
    m_prev = m_scr[...]
    log_d = jnp.where(tri, b_col - b_row + li_row, NEG)
    m_inter = b_col + m_prev
    m_t = jnp.maximum(m_inter, jnp.max(log_d, axis=1, keepdims=True))
    qb = q.astype(BF16)
    kb = k.astype(BF16)
    vb = v.astype(BF16)
    s = _nt(qb, kb) * jnp.exp(log_d - m_t)
    w_inter = jnp.exp(m_inter - m_t)
    c_old = c_scr[...]
    n_old = n_scr[...]
    num = (jnp.dot(s.astype(BF16), vb, preferred_element_type=F32)
           + w_inter * jnp.dot(qb, c_old.astype(BF16), preferred_element_type=F32))
    den = jnp.sum(s, axis=1, keepdims=True) + w_inter * jnp.sum(q * n_old, axis=1, keepdims=True)
    h = num / jnp.maximum(jnp.abs(den), jnp.exp(-m_t))

    b_last = b_row[:, L - 1:L]
    m_new = jnp.maximum(b_last + m_prev, jnp.max(b_last - b_row + li_row, axis=1, keepdims=True))
    decay = jnp.exp(b_last + m_prev - m_new)
    wk = jnp.exp(b_last - b_col + li_col - m_new)
    kw = k * wk
    c_new = decay * c_old + lax.dot_general(kw.astype(BF16), vb, (((0,), (0,)), ((), ())),
                                            preferred_element_type=F32)
    n_new = decay * n_old + jnp.sum(kw, axis=0, keepdims=True)
    c_scr[...] = c_new
    n_scr[...] = n_new
    m_scr[...] = m_new

    hv = h[:L]
    mu = jnp.mean(hv, axis=1, keepdims=True)
    var = jnp.mean(jnp.square(hv - mu), axis=1, keepdims=True)
    hn = (hv - mu) * lax.rsqrt(var + LN_EPS) * ng_ref[...]
    h_ref[...] = (hn * jax.nn.sigmoid(og_ref[...]) * _silu(gm_ref[...])).astype(h_ref.dtype)

    @pl.when(c == pl.num_programs(2) - 1)
    def _():
        c_out[...] = c_new
        n_out[...] = n_new
        m_out[...] = m_new


def _mlstm(z9, gates, norm_g, c0, n0, m0, bsz, t, L):
    Lp = max(L, LANES)
    nc = t // L
    dh = B_HEAD_DIM
    g3 = gates.reshape(bsz, t, LANES)
    gt = jnp.swapaxes(gates[:, :2 * B_HEADS].reshape(bsz, nc, L, 2 * B_HEADS), 2, 3)
    if Lp != L:
        gt = jnp.pad(gt, ((0, 0), (0, 0), (0, 0), (0, Lp - L)))
    ng = norm_g.reshape(B_HEADS, 1, dh).astype(F32)
    n0 = n0.reshape(bsz, B_HEADS, 1, dh)
    m0 = m0.reshape(bsz, B_HEADS, 1, 1)
    kern = functools.partial(_mlstm_kernel, L=L, Lp=Lp)
    zspec = lambda gidx: pl.BlockSpec((None, None, L, dh), lambda b, h, c: (gidx, b, c, h))
    st = lambda r, cdim: pl.BlockSpec((None, None, r, cdim), lambda b, h, c: (b, h, 0, 0))
    h, c1, n1, m1 = pl.pallas_call(
        kern,
        grid=(bsz, B_HEADS, nc),
        in_specs=[zspec(4), zspec(5), zspec(6), zspec(7), zspec(8),
                  pl.BlockSpec((None, L, LANES), lambda b, h, c: (b, c, 0)),
                  pl.BlockSpec((None, None, 2 * B_HEADS, Lp), lambda b, h, c: (b, c, 0, 0)),
                  pl.BlockSpec((None, 1, dh), lambda b, h, c: (h, 0, 0)),
                  st(dh, dh), st(1, dh), st(1, 1)],
        out_specs=[pl.BlockSpec((None, L, dh), lambda b, h, c: (b, c, h)),
                   st(dh, dh), st(1, dh), st(1, 1)],
        out_shape=[jax.ShapeDtypeStruct((bsz, t, B_WIDTH), BF16),
                   jax.ShapeDtypeStruct((bsz, B_HEADS, dh, dh), F32),
                   jax.ShapeDtypeStruct((bsz, B_HEADS, 1, dh), F32),
                   jax.ShapeDtypeStruct((bsz, B_HEADS, 1, 1), F32)],
        scratch_shapes=[pltpu.VMEM((dh, dh), F32), pltpu.VMEM((1, dh), F32), pltpu.VMEM((1, 1), F32)],
        compiler_params=_params("parallel", "parallel", "arbitrary"),
        name="mlstm",
    )(z9, z9, z9, z9, z9, g3, gt, ng, c0, n0, m0)
    return h, c1, n1.reshape(bsz, B_HEADS, dh), m1.reshape(bsz, B_HEADS)


def _outln_kernel(*refs, n_parts):
    mix = refs[:n_parts]
    ws = refs[n_parts:2 * n_parts]
    x_ref, g_ref, b_ref, o_ref, ob_ref = refs[2 * n_parts:]
    y = jnp.dot(mix[0][...], ws[0][...], preferred_element_type=F32)
    for p in range(1, n_parts):
        y = y + jnp.dot(mix[p][...], ws[p][...], preferred_element_type=F32)
    z = ALPHA * x_ref[...] + y
    mu = jnp.mean(z, axis=1, keepdims=True)
    var = jnp.mean(jnp.square(z - mu), axis=1, keepdims=True)
    xn = (z - mu) * lax.rsqrt(var + LN_EPS) * g_ref[...] + b_ref[...]
    o_ref[...] = xn
    ob_ref[...] = xn.astype(BF16)


def _out_ln(mix_parts, w_parts, x, ln_g, ln_b, tm):
    m, d = x.shape
    n_parts = len(mix_parts)
    in_specs = ([pl.BlockSpec((tm, mp.shape[1]), lambda i: (i, 0)) for mp in mix_parts]
                + [pl.BlockSpec(wp.shape, lambda i: (0, 0)) for wp in w_parts]
                + [pl.BlockSpec((tm, d), lambda i: (i, 0)),
                   pl.BlockSpec((1, d), lambda i: (0, 0)),
                   pl.BlockSpec((1, d), lambda i: (0, 0))])
    return pl.pallas_call(
        functools.partial(_outln_kernel, n_parts=n_parts),
        grid=(m // tm,),
        in_specs=in_specs,
        out_specs=[pl.BlockSpec((tm, d), lambda i: (i, 0)), pl.BlockSpec((tm, d), lambda i: (i, 0))],
        out_shape=[jax.ShapeDtypeStruct((m, d), F32), jax.ShapeDtypeStruct((m, d), BF16)],
        compiler_params=_params("parallel"),
        name="outproj_ln",
    )(*mix_parts, *w_parts, x, ln_g.reshape(1, d).astype(F32), ln_b.reshape(1, d).astype(F32))


def _odd_kernel(*refs, tm, carry_mode, seq_tiles, seq_rows):
    if carry_mode:
        x_ref, wb, wc, wx, wg, cw_ref, mix_ref, tail_ref, carry = refs
    else:
        x_ref, wb, wc, wx, wg, cw_ref, plast_ref, pprev_ref, mix_ref, tail_ref = refs
    x = x_ref[...]
    zb = jnp.dot(x, wb[...], preferred_element_type=F32)
    zc = jnp.dot(x, wc[...], preferred_element_type=F32)
    zx = jnp.dot(x, wx[...], preferred_element_type=F32)
    zg = jnp.dot(x, wg[...], preferred_element_type=F32)
    u = zc * zx
    rows = lax.broadcasted_iota(jnp.int32, u.shape, 0)
    r1 = pltpu.roll(u, 1, 0)
    r2 = pltpu.roll(u, 2, 0)
    if carry_mode:
        @pl.when(pl.program_id(1) % seq_tiles == 0)
        def _():
            carry[...] = jnp.zeros_like(carry)
        last = carry[SUBLANES - 1:SUBLANES, :]
        prev = carry[SUBLANES - 2:SUBLANES - 1, :]
        rm = rows
    else:
        last = plast_ref[...]
        prev = pprev_ref[...]
        rm = rows & (seq_rows - 1)
    u1 = jnp.where(rm == 0, last, r1)
    u2 = jnp.where(rm == 0, prev, jnp.where(rm == 1, last, r2))
    cw = cw_ref[...]
    y = cw[0:1, :] * u2 + cw[1:2, :] * u1 + cw[2:3, :] * u
    mix_ref[...] = (zb * y * _silu(zg)).astype(mix_ref.dtype)
    if carry_mode:
        tail = u[tm - SUBLANES:, :]
        carry[...] = tail
        tail_ref[...] = tail
    else:
        tail_ref[...] = u


def _odd_mix(x_bf, w_bf, conv_w, tm, tn, seq_len, state=None):
    m, k = x_bf.shape
    nn = C_WIDTH // tn
    carry_mode = state is None
    wspec = lambda gidx: pl.BlockSpec((k, tn), lambda j, i: (0, gidx * nn + j))
    in_specs = [pl.BlockSpec((tm, k), lambda j, i: (i, 0)),
                wspec(0), wspec(1), wspec(2), wspec(3),
                pl.BlockSpec((CONV_K, tn), lambda j, i: (0, j))]
    args = [x_bf, w_bf, w_bf, w_bf, w_bf, conv_w.astype(F32)]
    if carry_mode:
        assert seq_len % tm == 0
        tail_shape = jax.ShapeDtypeStruct((m // tm, SUBLANES, C_WIDTH), F32)
        tail_spec = pl.BlockSpec((None, SUBLANES, tn), lambda j, i: (i, 0, j))
        scratch = [pltpu.VMEM((SUBLANES, tn), F32)]
    else:
        assert tm % seq_len == 0 and seq_len & (seq_len - 1) == 0
        in_specs += [pl.BlockSpec((tm, tn), lambda j, i: (i, j))] * 2
        args += list(state)
        tail_shape = jax.ShapeDtypeStruct((m, C_WIDTH), F32)
        tail_spec = pl.BlockSpec((tm, tn), lambda j, i: (i, j))
        scratch = []
    kern = functools.partial(_odd_kernel, tm=tm, carry_mode=carry_mode,
                             seq_tiles=max(seq_len // tm, 1), seq_rows=seq_len)
    return pl.pallas_call(
        kern,
        grid=(nn, m // tm),
        in_specs=in_specs,
        out_specs=[pl.BlockSpec((tm, tn), lambda j, i: (i, j)), tail_spec],
        out_shape=[jax.ShapeDtypeStruct((m, C_WIDTH), BF16), tail_shape],
        scratch_shapes=scratch,
        compiler_params=_params("parallel", "arbitrary"),
        name="odd_mix",
    )(*args)


def _even_layer(x, x_bf, bsz, t, w_main, w_gate, gbias, w_out_a, w_out_b, norm_g, ln_g, ln_b,
                bias_tile, past, tm):
    z9 = _inproj_even(x_bf, w_main, tm).reshape(N_GROUPS_EVEN, bsz, t, 1024)
    gates = _gates(x_bf, w_gate, gbias, tm)
    if past is None:
        attn = _moba_prompt(z9, bias_tile, bsz, t)
        c0 = jnp.zeros((bsz, B_HEADS, B_HEAD_DIM, B_HEAD_DIM), F32)
        n0 = jnp.zeros((bsz, B_HEADS, B_HEAD_DIM), F32)
        m0 = jnp.zeros((bsz, B_HEADS), F32)
        L = math.gcd(t, MLSTM_PROMPT_CHUNK)
    else:
        cache_k, cache_v, layer, page_table, c0, n0, m0 = past
        attn = _moba_sample(z9, bias_tile, cache_k, cache_v, layer, page_table, bsz, t)
        L = math.gcd(t, MLSTM_CHUNK)
    hm, c1, n1, m1 = _mlstm(z9, gates, norm_g, c0, n0, m0, bsz, t, L)
    xn, xn_bf = _out_ln([attn.reshape(bsz * t, A_WIDTH), hm.reshape(bsz * t, B_WIDTH)],
                        [w_out_a, w_out_b], x, ln_g, ln_b, min(tm, 256))
    ka = z9[1].reshape(bsz, t, A_HEADS, A_HEAD_DIM)
    va = z9[2].reshape(bsz, t, A_HEADS, A_HEAD_DIM)
    return xn, xn_bf, (ka, va, c1, n1, m1)


def _odd_layer(x, x_bf, bsz, t, w_in, w_out, conv_w, ln_g, ln_b, buf, tm):
    if buf is None:
        mix, tail = _odd_mix(x_bf, w_in, conv_w, tm, 256, t)
        seq_tiles = t // tm
        new_buf = tail.reshape(bsz, seq_tiles, SUBLANES, C_WIDTH)[:, -1, SUBLANES - (CONV_K - 1):, :]
    else:
        plast = jnp.repeat(buf[:, 1, :], t, axis=0)
        pprev = jnp.repeat(buf[:, 0, :], t, axis=0)
        mix, u = _odd_mix(x_bf, w_in, conv_w, bsz * t, 256, t, state=(plast, pprev))
        new_buf = u.reshape(bsz, t, C_WIDTH)[:, t - (CONV_K - 1):, :]
    xn, xn_bf = _out_ln([mix], [w_out], x, ln_g, ln_b, min(tm, 256))
    return xn, xn_bf, new_buf


def kernel(x_prompt, x_sample, cache_k, cache_v, page_table, state_C, state_n, state_m, state_conv,
           w_in_even, w_out_even, mlstm_gate_bias, mlstm_norm_g, rel_bias,
           w_in_odd, w_out_odd, conv_w, ln_g, ln_b):
    bp, tp, d = x_prompt.shape
    bs, ts, _ = x_sample.shape
    bias_p = _bias_tiles(rel_bias, _prompt_bias_idx())
    bias_s = _bias_tiles(rel_bias, _sample_bias_idx(ts)).reshape(A_HEADS * ts, -1)

    xp = x_prompt.reshape(bp * tp, d)
    xs = x_sample.reshape(bs * ts, d)
    xp_bf = xp.astype(BF16)
    xs_bf = xs.astype(BF16)
    tm_p = 512
    tm_s = bs * ts

    pk, pv, pc, pn, pm, pb = [], [], [], [], [], []
    sk, sv, sc, sn, sm, sb = [], [], [], [], [], []
    for layer in range(DEPTH):
        if layer % 2 == 0:
            e = layer // 2
            w = w_in_even[e]
            w_main = w[:, :EVEN_MAIN].astype(BF16)
            w_gate = jnp.pad(w[:, EVEN_MAIN:], ((0, 0), (0, LANES - 2 * B_HEADS))).astype(BF16)
            gbias = jnp.pad(mlstm_gate_bias[e], (0, LANES - 2 * B_HEADS)).reshape(1, LANES).astype(F32)
            w_out_a = w_out_even[e][:A_WIDTH].astype(BF16)
            w_out_b = w_out_even[e][A_WIDTH:].astype(BF16)
            xp, xp_bf, (k1, v1, c1, n1, m1) = _even_layer(
                xp, xp_bf, bp, tp, w_main, w_gate, gbias, w_out_a, w_out_b, mlstm_norm_g[e],
                ln_g[layer], ln_b[layer], bias_p, None, tm_p)
            xs, xs_bf, (k2, v2, c2, n2, m2) = _even_layer(
                xs, xs_bf, bs, ts, w_main, w_gate, gbias, w_out_a, w_out_b, mlstm_norm_g[e],
                ln_g[layer], ln_b[layer], bias_s,
                (cache_k, cache_v, e, page_table, state_C[e], state_n[e], state_m[e]), tm_s)
            pk.append(k1); pv.append(v1); pc.append(c1); pn.append(n1); pm.append(m1)
            sk.append(k2); sv.append(v2); sc.append(c2); sn.append(n2); sm.append(m2)
        else:
            o = layer // 2
            w_in = w_in_odd[o].astype(BF16)
            w_out = w_out_odd[o].astype(BF16)
            xp, xp_bf, b1 = _odd_layer(xp, xp_bf, bp, tp, w_in, w_out, conv_w[o],
                                       ln_g[layer], ln_b[layer], None, tm_p)
            xs, xs_bf, b2 = _odd_layer(xs, xs_bf, bs, ts, w_in, w_out, conv_w[o],
                                       ln_g[layer], ln_b[layer], state_conv[o], tm_s)
            pb.append(b1); sb.append(b2)
    return (xp.reshape(bp, tp, d), xs.reshape(bs, ts, d),
            jnp.stack(pk), jnp.stack(pv), jnp.stack(pc), jnp.stack(pn), jnp.stack(pm), jnp.stack(pb),
            jnp.stack(sk), jnp.stack(sv), jnp.stack(sc), jnp.stack(sn), jnp.stack(sm), jnp.stack(sb))
```

```python
import functools
import math

import numpy as np
import jax
import jax.numpy as jnp
from jax import lax
from jax.experimental import pallas as pl
from jax.experimental.pallas import tpu as pltpu

F32 = jnp.float32
BF16 = jnp.bfloat16

D_MODEL = 2048
DEPTH = 4
PAGE_SIZE = 128
A_HEADS = 8
A_HEAD_DIM = 128
A_WIDTH = A_HEADS * A_HEAD_DIM
MOBA_BLOCK = 256
MOBA_TOPK = 3
REL_BUCKETS = 32
REL_MAX_DIST = 128
B_HEADS = 4
B_HEAD_DIM = 256
B_WIDTH = B_HEADS * B_HEAD_DIM
MLSTM_CHUNK = 64
MLSTM_PROMPT_CHUNK = 256
C_WIDTH = D_MODEL
CONV_K = 3
N_GROUPS_EVEN = 9
EVEN_MAIN = N_GROUPS_EVEN * 1024
ALPHA = (2.0 * DEPTH) ** 0.25
LN_EPS = 1e-5

LANES = 128
SUBLANES = 8
VMEM_LIMIT = 56 * 1024 * 1024

NEG = -1e30


def _params(*sem):
    return pltpu.CompilerParams(dimension_semantics=sem, vmem_limit_bytes=VMEM_LIMIT)


def _nt(a, b):
    return lax.dot_general(a, b, (((1,), (1,)), ((), ())), preferred_element_type=F32)


def _silu(x):
    return x * jax.nn.sigmoid(x)


def _bucket_np(d):
    d = np.asarray(d, np.int64)
    n = np.maximum(d, 0)
    max_exact = REL_BUCKETS // 2
    nf = np.maximum(n, 1).astype(np.float64)
    large = max_exact + (np.log(nf / max_exact) / math.log(REL_MAX_DIST / max_exact)
                         * (REL_BUCKETS - max_exact)).astype(np.int64)
    large = np.minimum(large, REL_BUCKETS - 1)
    out = np.where(n < max_exact, n, large)
    return np.where(d < 0, -1, out).astype(np.int32)


def _bias_kernel(tab_ref, idx_ref, o_ref):
    h = pl.program_id(0)
    idx = idx_ref[...]
    out = jnp.full(idx.shape, NEG, F32)
    for b in range(REL_BUCKETS):
        out = jnp.where(idx == b, tab_ref[h, b], out)
    o_ref[...] = out


def _bias_tiles(rel_bias, idx):
    r, c = idx.shape
    return pl.pallas_call(
        _bias_kernel,
        grid=(A_HEADS,),
        in_specs=[pl.BlockSpec(memory_space=pltpu.SMEM),
                  pl.BlockSpec((r, c), lambda h: (0, 0))],
        out_specs=pl.BlockSpec((None, r, c), lambda h: (h, 0, 0)),
        out_shape=jax.ShapeDtypeStruct((A_HEADS, r, c), F32),
        compiler_params=_params("parallel"),
        name="bias_tiles",
    )(rel_bias.astype(F32), jnp.asarray(idx))


def _prompt_bias_idx():
    qi = np.arange(MOBA_BLOCK)[:, None]
    kj = np.arange(2 * MOBA_BLOCK)[None, :]
    return _bucket_np(qi + MOBA_BLOCK - kj)


def _sample_bias_idx(s_len):
    s = np.arange(s_len)[:, None]
    lane = np.arange(2 * PAGE_SIZE * A_HEADS)[None, :]
    off = lane // A_HEADS
    far = _bucket_np(MOBA_BLOCK + s - off)
    lo = np.arange(LANES)[None, :]
    own = _bucket_np(np.where(lo < A_HEADS * s_len, s - (lo % s_len), -1))
    return np.concatenate([far, own], axis=1)


def _inproj_kernel(x_ref, w_ref, o_ref, wb_ref):
    @pl.when(pl.program_id(1) == 0)
    def _():
        wb_ref[...] = w_ref[...].astype(BF16)

    o_ref[...] = jnp.dot(x_ref[...], wb_ref[...], preferred_element_type=F32)


def _inproj_even(x_bf, w_all, e, tm, tn=1024):
    m, k = x_bf.shape
    per = 1024 // tn
    return pl.pallas_call(
        _inproj_kernel,
        grid=(N_GROUPS_EVEN * per, m // tm),
        in_specs=[pl.BlockSpec((tm, k), lambda j, i: (i, 0)),
                  pl.BlockSpec((None, k, tn), lambda j, i: (e, 0, j))],
        out_specs=pl.BlockSpec((None, tm, tn), lambda j, i: (j // per, i, j % per)),
        out_shape=jax.ShapeDtypeStruct((N_GROUPS_EVEN, m, 1024), F32),
        scratch_shapes=[pltpu.VMEM((k, tn), BF16)],
        compiler_params=_params("parallel", "arbitrary"),
        name="inproj_even",
    )(x_bf, w_all)


def _gates_kernel(x_ref, w_ref, b_ref, o_ref):
    o_ref[...] = jnp.dot(x_ref[...], w_ref[...], preferred_element_type=F32) + b_ref[...]


def _gates(x_bf, wg_bf, gbias, tm):
    m, k = x_bf.shape
    return pl.pallas_call(
        _gates_kernel,
        grid=(m // tm,),
        in_specs=[pl.BlockSpec((tm, k), lambda i: (i, 0)),
                  pl.BlockSpec((k, LANES), lambda i: (0, 0)),
                  pl.BlockSpec((1, LANES), lambda i: (0, 0))],
        out_specs=pl.BlockSpec((tm, LANES), lambda i: (i, 0)),
        out_shape=jax.ShapeDtypeStruct((m, LANES), F32),
        compiler_params=_params("parallel"),
        name="mlstm_gates",
    )(x_bf, wg_bf, gbias)


def _moba_prompt_kernel(q_ref, k_ref, v_ref, g_ref, bias_ref, o_ref, kmean_ref, kb_ref, vb_ref,
                        *, nblk, scale):
    blk = MOBA_BLOCK
    kb_ref[...] = k_ref[...].astype(BF16)
    vb_ref[...] = v_ref[...].astype(BF16)
    for j in range(nblk):
        kmean_ref[j:j + 1, :] = jnp.mean(k_ref[j * blk:(j + 1) * blk, :], axis=0, keepdims=True)
    far_bias = bias_ref[0:1, 0:1]
    eye = (lax.broadcasted_iota(jnp.int32, (blk, blk), 0)
           == lax.broadcasted_iota(jnp.int32, (blk, blk), 1)).astype(BF16)
    rowi = lax.broadcasted_iota(jnp.int32, (nblk, blk), 0)

    for i in range(nblk):
        q = q_ref[i * blk:(i + 1) * blk, :]
        negm = None
        if i > MOBA_TOPK:
            gt = lax.dot_general(kmean_ref[...], q, (((1,), (1,)), ((), ())),
                                 precision=lax.Precision.HIGHEST, preferred_element_type=F32)
            valid = rowi < i
            sel_t = jnp.zeros((nblk, blk), F32)
            for j in range(i):
                gj = gt[j:j + 1, :]
                beats = ((gt > gj) | ((gt == gj) & (rowi < j))) & valid
                rank = jnp.sum(beats.astype(F32), axis=0, keepdims=True)
                sel_t = jnp.where(rowi == j, (rank < MOBA_TOPK).astype(F32), sel_t)
            sel_t = jnp.concatenate([sel_t, jnp.zeros((LANES - nblk, blk), F32)], axis=0)
            sel = _nt(eye, sel_t.astype(BF16))
            negm = (sel - 1.0) * (-NEG)
        w = (i + 1) * blk
        s = _nt((q * scale).astype(BF16), kb_ref[0:w, :])
        pieces = []
        for j in range(i + 1):
            sj = s[:, j * blk:(j + 1) * blk]
            if j == i:
                sj = sj + bias_ref[:, blk:2 * blk]
            elif j == i - 1:
                sj = sj + bias_ref[:, 0:blk]
                if negm is not None:
                    sj = sj + negm[:, j:j + 1]
            else:
                sj = sj + (far_bias if negm is None else far_bias + negm[:, j:j + 1])
            pieces.append(sj)
        m = jnp.max(functools.reduce(jnp.maximum, pieces), axis=1, keepdims=True)
        ps = [jnp.exp(sj - m) for sj in pieces]
        l = jnp.sum(functools.reduce(jnp.add, ps), axis=1, keepdims=True)
        pcat = ps[0].astype(BF16) if i == 0 else jnp.concatenate([p.astype(BF16) for p in ps], axis=1)
        acc = jnp.dot(pcat, vb_ref[0:w, :], preferred_element_type=F32)
        g = g_ref[i * blk:(i + 1) * blk, :]
        o_ref[i * blk:(i + 1) * blk, :] = (acc / l * _silu(g)).astype(o_ref.dtype)


def _moba_prompt(z9, bias_p, bsz, t):
    blk = MOBA_BLOCK
    nblk = t // blk
    assert nblk <= SUBLANES
    dh = A_HEAD_DIM
    kern = functools.partial(_moba_prompt_kernel, nblk=nblk, scale=dh ** -0.5)
    zspec = lambda g: pl.BlockSpec((None, None, t, dh), lambda b, h: (g, b, 0, h))
    return pl.pallas_call(
        kern,
        grid=(bsz, A_HEADS),
        in_specs=[zspec(0), zspec(1), zspec(2), zspec(3),
                  pl.BlockSpec((None, blk, 2 * blk), lambda b, h: (h, 0, 0))],
        out_specs=pl.BlockSpec((None, t, dh), lambda b, h: (b, 0, h)),
        out_shape=jax.ShapeDtypeStruct((bsz, t, A_WIDTH), BF16),
        scratch_shapes=[pltpu.VMEM((nblk, dh), F32),
                        pltpu.VMEM((t, dh), BF16),
                        pltpu.VMEM((t, dh), BF16)],
        compiler_params=_params("parallel", "parallel"),
        name="moba_prompt",
    )(z9, z9, z9, z9, bias_p)


def _stack_heads(x):
    return jnp.concatenate([x[:, h * A_HEAD_DIM:(h + 1) * A_HEAD_DIM] for h in range(A_HEADS)], axis=0)


def _moba_sample_kernel(pt_ref, q_ref, kn_ref, vn_ref, ga_ref, bias_ref, *rest,
                        nb, bps, s_len, scale):
    del pt_ref
    npg = 2 * bps
    k_refs = rest[:npg]
    v_refs = rest[npg:2 * npg]
    o_ref, qm_ref, gate_ref, m_ref, l_ref, oblk_ref = rest[2 * npg:]
    n = pl.program_id(1)
    rows = A_HEADS * s_len
    s_shift = s_len.bit_length() - 1
    far_w = PAGE_SIZE * A_HEADS

    @pl.when(n == 0)
    def _():
        qm_ref[...] = _stack_heads(q_ref[...])
        gate_ref[...] = jnp.full_like(gate_ref, NEG)
        m_ref[...] = jnp.full_like(m_ref, NEG)
        l_ref[...] = jnp.zeros_like(l_ref)

    qb = qm_ref[...].astype(BF16)
    row = lax.broadcasted_iota(jnp.int32, (rows, far_w), 0)
    lane = lax.broadcasted_iota(jnp.int32, (rows, far_w), 1)
    useful = (lane & (A_HEADS - 1)) == (row >> s_shift)
    lane_b = lax.broadcasted_iota(jnp.int32, (rows, LANES), 1)
    last = n == pl.num_programs(1) - 1
    far_bias = bias_ref[:, 0:1]

    def pages(ref):
        return ref[...].reshape(far_w, A_HEAD_DIM).astype(BF16)

    gate_new = gate_ref[...]
    m_new = m_ref[...]
    l_new = l_ref[...]
    for c in range(bps):
        blk = n * bps + c
        raw0 = _nt(qb, pages(k_refs[2 * c]))
        raw1 = _nt(qb, pages(k_refs[2 * c + 1]))
        gsum = jnp.sum(jnp.where(useful, raw0 + raw1, 0.0), axis=1, keepdims=True)
        if c == bps - 1:
            last_v = (jnp.zeros((rows, far_w), jnp.int32) + blk) == nb - 1
            b0 = jnp.where(last_v, bias_ref[:, 0:far_w], far_bias)
            b1 = jnp.where(last_v, bias_ref[:, far_w:2 * far_w], far_bias)
        else:
            b0 = b1 = far_bias
        s0 = jnp.where(useful, raw0 * scale + b0, NEG)
        s1 = jnp.where(useful, raw1 * scale + b1, NEG)
        mloc = jnp.max(jnp.maximum(s0, s1), axis=1, keepdims=True)
        p0 = jnp.exp(s0 - mloc)
        p1 = jnp.exp(s1 - mloc)
        lloc = jnp.sum(p0 + p1, axis=1, keepdims=True)
        oblk_ref[blk] = (jnp.dot(p0.astype(BF16), pages(v_refs[2 * c]), preferred_element_type=F32)
                         + jnp.dot(p1.astype(BF16), pages(v_refs[2 * c + 1]), preferred_element_type=F32))
        here = lane_b == blk
        gate_new = jnp.where(here, gsum * (1.0 / MOBA_BLOCK), gate_new)
        m_new = jnp.where(here, mloc, m_new)
        l_new = jnp.where(here, lloc, l_new)
    gate_ref[...] = gate_new
    m_ref[...] = m_new
    l_ref[...] = l_new

    @pl.when(last)
    def _():
        row_b = lax.broadcasted_iota(jnp.int32, (rows, LANES), 0)
        lane_f = lane_b.astype(F32)
        g = jnp.where(lane_b < nb, gate_ref[...], -jnp.inf)
        sel = jnp.zeros((rows, LANES), jnp.bool_)
        for _r in range(min(MOBA_TOPK, nb)):
            mx = jnp.max(g, axis=1, keepdims=True)
            idx = jnp.min(jnp.where(g == mx, lane_f, float(LANES)), axis=1, keepdims=True)
            hit = lane_f == idx
            sel = sel | hit
            g = jnp.where(hit, -jnp.inf, g)
        pad = jnp.zeros((LANES - rows, A_HEAD_DIM), F32)
        kn = jnp.concatenate([_stack_heads(kn_ref[...]), pad], axis=0).astype(BF16)
        vn = jnp.concatenate([_stack_heads(vn_ref[...]), pad], axis=0).astype(BF16)
        so = _nt(qb, kn) * scale + bias_ref[:, 2 * far_w:2 * far_w + LANES]
        ok = ((lane_b < rows) & ((lane_b >> s_shift) == (row_b >> s_shift))
              & ((lane_b & (s_len - 1)) <= (row_b & (s_len - 1))))
        so = jnp.where(ok, so, NEG)
        mm = m_ref[...]
        mtot = jnp.maximum(jnp.max(jnp.where(sel, mm, NEG), axis=1, keepdims=True),
                           jnp.max(so, axis=1, keepdims=True))
        w = jnp.where(sel, jnp.exp(mm - mtot), 0.0)
        po = jnp.exp(so - mtot)
        lsum = jnp.sum(w * l_ref[...], axis=1, keepdims=True) + jnp.sum(po, axis=1, keepdims=True)
        acc = jnp.dot(po.astype(BF16), vn, preferred_element_type=F32)
        for nn in range(nb):
            acc = acc + w[:, nn:nn + 1] * oblk_ref[nn]
        out = acc / lsum
        res = jnp.concatenate([out[h * s_len:(h + 1) * s_len, :] for h in range(A_HEADS)], axis=1)
        o_ref[...] = (res * _silu(ga_ref[...])).astype(o_ref.dtype)


def _moba_sample(z9, bias_s, cache_k, cache_v, layer, page_table, bsz, s_len, bps=4):
    n_pages = page_table.shape[1]
    ppb = MOBA_BLOCK // PAGE_SIZE
    assert ppb == 2 and n_pages % ppb == 0 and s_len == SUBLANES and A_HEADS == SUBLANES
    nb = n_pages // ppb
    assert nb <= LANES and nb % bps == 0
    rows = A_HEADS * s_len
    dh = A_HEAD_DIM
    kern = functools.partial(_moba_sample_kernel, nb=nb, bps=bps, s_len=s_len, scale=dh ** -0.5)
    tok = lambda g: pl.BlockSpec((None, None, s_len, A_WIDTH), lambda b, n, pt: (g, b, 0, 0))
    page = lambda o: pl.BlockSpec((None, None, PAGE_SIZE, A_HEADS, dh),
                                  lambda b, n, pt: (layer, pt[b, ppb * bps * n + o], 0, 0, 0))
    pages = [page(o) for o in range(ppb * bps)]
    grid_spec = pltpu.PrefetchScalarGridSpec(
        num_scalar_prefetch=1,
        grid=(bsz, nb // bps),
        in_specs=[tok(0), tok(1), tok(2), tok(3),
                  pl.BlockSpec(bias_s.shape, lambda b, n, pt: (0, 0))] + pages + pages,
        out_specs=pl.BlockSpec((None, s_len, A_WIDTH), lambda b, n, pt: (b, 0, 0)),
        scratch_shapes=[pltpu.VMEM((rows, dh), F32),
                        pltpu.VMEM((rows, LANES), F32),
                        pltpu.VMEM((rows, LANES), F32),
                        pltpu.VMEM((rows, LANES), F32),
                        pltpu.VMEM((nb, rows, dh), F32)])
    return pl.pallas_call(
        kern,
        grid_spec=grid_spec,
        out_shape=jax.ShapeDtypeStruct((bsz, s_len, A_WIDTH), BF16),
        compiler_params=_params("parallel", "arbitrary"),
        name="moba_sample",
    )(page_table, z9, z9, z9, z9, bias_s, *([cache_k] * (ppb * bps)), *([cache_v] * (ppb * bps)))


def _mlstm_kernel(q_ref, k_ref, v_ref, og_ref, gm_ref, g_ref, gt_ref, ng_ref, c0_ref, n0_ref, m0_ref,
                  h_ref, c_out, n_out, m_out, c_scr, n_scr, m_scr, *, L, Lp):
    hh = pl.program_id(1)
    c = pl.program_id(2)
    dk = B_HEAD_DIM

    @pl.when(c == 0)
    def _():
        c_scr[...] = c0_ref[...]
        n_scr[...] = n0_ref[...]
        m_scr[...] = m0_ref[...]

    def padded(x):
        if L == Lp:
            return x
        return jnp.concatenate([x, jnp.zeros((Lp - L, x.shape[1]), x.dtype)], axis=0)

    q = padded(q_ref[...])
    k = padded(k_ref[...]) * (dk ** -0.5)
    v = padded(v_ref[...])
    g = padded(g_ref[...])
    lane = lax.broadcasted_iota(jnp.int32, g.shape, 1)
    li_col = jnp.sum(jnp.where(lane == hh, g, 0.0), axis=1, keepdims=True)
    fp_col = jnp.sum(jnp.where(lane == hh + B_HEADS, g, 0.0), axis=1, keepdims=True)
    li_row = gt_ref[pl.ds(hh, 1), :]
    fp_row = gt_ref[pl.ds(hh + B_HEADS, 1), :]
    lf_col = jax.nn.log_sigmoid(fp_col)
    lf_row = jax.nn.log_sigmoid(fp_row)
    if L != Lp:
        rv = lax.broadcasted_iota(jnp.int32, (Lp, 1), 0) < L
        cv = lax.broadcasted_iota(jnp.int32, (1, Lp), 1) < L
        li_col = jnp.where(rv, li_col, NEG)
        lf_col = jnp.where(rv, lf_col, 0.0)
        li_row = jnp.where(cv, li_row, NEG)
        lf_row = jnp.where(cv, lf_row, 0.0)

    r = lax.broadcasted_iota(jnp.int32, (Lp, Lp), 0)
    cc = lax.broadcasted_iota(jnp.int32, (Lp, Lp), 1)
    tri = r >= cc
    b_col = jnp.sum(jnp.where(tri, lf_row, 0.0), axis=1, keepdims=True)
    b_row = jnp.sum(jnp.where(r <= cc, lf_col, 0.0), axis=0, keepdims=True)
    m_prev = m_scr[...]
    log_d = jnp.where(tri, b_col - b_row + li_row, NEG)
    m_inter = b_col + m_prev
    m_t = jnp.maximum(m_inter, jnp.max(log_d, axis=1, keepdims=True))
    qb = q.astype(BF16)
    kb = k.astype(BF16)
    vb = v.astype(BF16)
    s = _nt(qb, kb) * jnp.exp(log_d - m_t)
    w_inter = jnp.exp(m_inter - m_t)
    c_old = c_scr[...]
    n_old = n_scr[...]
    num = (jnp.dot(s.astype(BF16), vb, preferred_element_type=F32)
           + w_inter * jnp.dot(qb, c_old.astype(BF16), preferred_element_type=F32))
    den = jnp.sum(s, axis=1, keepdims=True) + w_inter * jnp.sum(q * n_old, axis=1, keepdims=True)
    h = num / jnp.maximum(jnp.abs(den), jnp.exp(-m_t))

    b_last = b_row[:, L - 1:L]
    m_new = jnp.maximum(b_last + m_prev, jnp.max(b_last - b_row + li_row, axis=1, keepdims=True))
    decay = jnp.exp(b_last + m_prev - m_new)
    wk = jnp.exp(b_last - b_col + li_col - m_new)
    kw = k * wk
    c_new = decay * c_old + lax.dot_general(kw.astype(BF16), vb, (((0,), (0,)), ((), ())),
                                            preferred_element_type=F32)
    n_new = decay * n_old + jnp.sum(kw, axis=0, keepdims=True)
    c_scr[...] = c_new
    n_scr[...] = n_new
    m_scr[...] = m_new

    hv = h[:L]
    mu = jnp.mean(hv, axis=1, keepdims=True)
    var = jnp.mean(jnp.square(hv - mu), axis=1, keepdims=True)
    hn = (hv - mu) * lax.rsqrt(var + LN_EPS) * ng_ref[...]
    h_ref[...] = (hn * jax.nn.sigmoid(og_ref[...]) * _silu(gm_ref[...])).astype(h_ref.dtype)

    @pl.when(c == pl.num_programs(2) - 1)
    def _():
        c_out[...] = c_new
        n_out[...] = n_new
        m_out[...] = m_new


def _mlstm(z9, gates, norm_g, c0, n0, m0, bsz, t, L):
    Lp = max(L, LANES)
    nc = t // L
    dh = B_HEAD_DIM
    g3 = gates.reshape(bsz, t, LANES)
    gt = jnp.swapaxes(gates[:, :2 * B_HEADS].reshape(bsz, nc, L, 2 * B_HEADS), 2, 3)
    if Lp != L:
        gt = jnp.pad(gt, ((0, 0), (0, 0), (0, 0), (0, Lp - L)))
    ng = norm_g.reshape(B_HEADS, 1, dh).astype(F32)
    n0 = n0.reshape(bsz, B_HEADS, 1, dh)
    m0 = m0.reshape(bsz, B_HEADS, 1, 1)
    kern = functools.partial(_mlstm_kernel, L=L, Lp=Lp)
    zspec = lambda gidx: pl.BlockSpec((None, None, L, dh), lambda b, h, c: (gidx, b, c, h))
    st = lambda r, cdim: pl.BlockSpec((None, None, r, cdim), lambda b, h, c: (b, h, 0, 0))
    h, c1, n1, m1 = pl.pallas_call(
        kern,
        grid=(bsz, B_HEADS, nc),
        in_specs=[zspec(4), zspec(5), zspec(6), zspec(7), zspec(8),
                  pl.BlockSpec((None, L, LANES), lambda b, h, c: (b, c, 0)),
                  pl.BlockSpec((None, None, 2 * B_HEADS, Lp), lambda b, h, c: (b, c, 0, 0)),
                  pl.BlockSpec((None, 1, dh), lambda b, h, c: (h, 0, 0)),
                  st(dh, dh), st(1, dh), st(1, 1)],
        out_specs=[pl.BlockSpec((None, L, dh), lambda b, h, c: (b, c, h)),
                   st(dh, dh), st(1, dh), st(1, 1)],
        out_shape=[jax.ShapeDtypeStruct((bsz, t, B_WIDTH), BF16),
                   jax.ShapeDtypeStruct((bsz, B_HEADS, dh, dh), F32),
                   jax.ShapeDtypeStruct((bsz, B_HEADS, 1, dh), F32),
                   jax.ShapeDtypeStruct((bsz, B_HEADS, 1, 1), F32)],
        scratch_shapes=[pltpu.VMEM((dh, dh), F32), pltpu.VMEM((1, dh), F32), pltpu.VMEM((1, 1), F32)],
        compiler_params=_params("parallel", "parallel", "arbitrary"),
        name="mlstm",
    )(z9, z9, z9, z9, z9, g3, gt, ng, c0, n0, m0)
    return h, c1, n1.reshape(bsz, B_HEADS, dh), m1.reshape(bsz, B_HEADS)


def _outln_kernel(*refs, n_parts):
    mix = refs[:n_parts]
    ws = refs[n_parts:2 * n_parts]
    x_ref, g_ref, b_ref, o_ref, ob_ref = refs[2 * n_parts:]
    y = jnp.dot(mix[0][...], ws[0][...], preferred_element_type=F32)
    for p in range(1, n_parts):
        y = y + jnp.dot(mix[p][...], ws[p][...], preferred_element_type=F32)
    z = ALPHA * x_ref[...] + y
    mu = jnp.mean(z, axis=1, keepdims=True)
    var = jnp.mean(jnp.square(z - mu), axis=1, keepdims=True)
    xn = (z - mu) * lax.rsqrt(var + LN_EPS) * g_ref[...] + b_ref[...]
    o_ref[...] = xn
    ob_ref[...] = xn.astype(BF16)


def _out_ln(mix_parts, w_parts, x, ln_g, ln_b, tm):
    m, d = x.shape
    n_parts = len(mix_parts)
    in_specs = ([pl.BlockSpec((tm, mp.shape[1]), lambda i: (i, 0)) for mp in mix_parts]
                + [pl.BlockSpec(wp.shape, lambda i: (0, 0)) for wp in w_parts]
                + [pl.BlockSpec((tm, d), lambda i: (i, 0)),
                   pl.BlockSpec((1, d), lambda i: (0, 0)),
                   pl.BlockSpec((1, d), lambda i: (0, 0))])
    return pl.pallas_call(
        functools.partial(_outln_kernel, n_parts=n_parts),
        grid=(m // tm,),
        in_specs=in_specs,
        out_specs=[pl.BlockSpec((tm, d), lambda i: (i, 0)), pl.BlockSpec((tm, d), lambda i: (i, 0))],
        out_shape=[jax.ShapeDtypeStruct((m, d), F32), jax.ShapeDtypeStruct((m, d), BF16)],
        compiler_params=_params("parallel"),
        name="outproj_ln",
    )(*mix_parts, *w_parts, x, ln_g.reshape(1, d).astype(F32), ln_b.reshape(1, d).astype(F32))


def _odd_kernel(*refs, tm, carry_mode, seq_tiles, seq_rows):
    if carry_mode:
        x_ref, wb, wc, wx, wg, cw_ref, mix_ref, tail_ref, wbf, carry = refs
    else:
        x_ref, wb, wc, wx, wg, cw_ref, plast_ref, pprev_ref, mix_ref, tail_ref, wbf = refs

    @pl.when(pl.program_id(1) == 0)
    def _():
        for gidx, wref in enumerate((wb, wc, wx, wg)):
            wbf[gidx] = wref[...].astype(BF16)

    x = x_ref[...]
    zb = jnp.dot(x, wbf[0], preferred_element_type=F32)
    zc = jnp.dot(x, wbf[1], preferred_element_type=F32)
    zx = jnp.dot(x, wbf[2], preferred_element_type=F32)
    zg = jnp.dot(x, wbf[3], preferred_element_type=F32)
    u = zc * zx
    rows = lax.broadcasted_iota(jnp.int32, u.shape, 0)
    r1 = pltpu.roll(u, 1, 0)
    r2 = pltpu.roll(u, 2, 0)
    if carry_mode:
        @pl.when(pl.program_id(1) % seq_tiles == 0)
        def _():
            carry[...] = jnp.zeros_like(carry)
        last = carry[SUBLANES - 1:SUBLANES, :]
        prev = carry[SUBLANES - 2:SUBLANES - 1, :]
        rm = rows
    else:
        last = plast_ref[...]
        prev = pprev_ref[...]
        rm = rows & (seq_rows - 1)
    u1 = jnp.where(rm == 0, last, r1)
    u2 = jnp.where(rm == 0, prev, jnp.where(rm == 1, last, r2))
    cw = cw_ref[...]
    y = cw[0:1, :] * u2 + cw[1:2, :] * u1 + cw[2:3, :] * u
    mix_ref[...] = (zb * y * _silu(zg)).astype(mix_ref.dtype)
    if carry_mode:
        tail = u[tm - SUBLANES:, :]
        carry[...] = tail
        tail_ref[...] = tail
    else:
        tail_ref[...] = u


def _odd_mix(x_bf, w_all, o, conv_w, tm, tn, seq_len, state=None):
    m, k = x_bf.shape
    nn = C_WIDTH // tn
    carry_mode = state is None
    wspec = lambda gidx: pl.BlockSpec((None, k, tn), lambda j, i: (o, 0, gidx * nn + j))
    in_specs = [pl.BlockSpec((tm, k), lambda j, i: (i, 0)),
                wspec(0), wspec(1), wspec(2), wspec(3),
                pl.BlockSpec((CONV_K, tn), lambda j, i: (0, j))]
    args = [x_bf, w_all, w_all, w_all, w_all, conv_w.astype(F32)]
    scratch = [pltpu.VMEM((4, k, tn), BF16)]
    if carry_mode:
        assert seq_len % tm == 0
        tail_shape = jax.ShapeDtypeStruct((m // tm, SUBLANES, C_WIDTH), F32)
        tail_spec = pl.BlockSpec((None, SUBLANES, tn), lambda j, i: (i, 0, j))
        scratch.append(pltpu.VMEM((SUBLANES, tn), F32))
    else:
        assert tm % seq_len == 0 and seq_len & (seq_len - 1) == 0
        in_specs += [pl.BlockSpec((tm, tn), lambda j, i: (i, j))] * 2
        args += list(state)
        tail_shape = jax.ShapeDtypeStruct((m, C_WIDTH), F32)
        tail_spec = pl.BlockSpec((tm, tn), lambda j, i: (i, j))
    kern = functools.partial(_odd_kernel, tm=tm, carry_mode=carry_mode,
                             seq_tiles=max(seq_len // tm, 1), seq_rows=seq_len)
    return pl.pallas_call(
        kern,
        grid=(nn, m // tm),
        in_specs=in_specs,
        out_specs=[pl.BlockSpec((tm, tn), lambda j, i: (i, j)), tail_spec],
        out_shape=[jax.ShapeDtypeStruct((m, C_WIDTH), BF16), tail_shape],
        scratch_shapes=scratch,
        compiler_params=_params("parallel", "arbitrary"),
        name="odd_mix",
    )(*args)


def _row_tiles(m):
    return min(m, 1024), min(m, 512)


def _even_layer(x, x_bf, bsz, t, w_in_all, e, w_gate, gbias, w_out_a, w_out_b, norm_g, ln_g, ln_b,
                bias_tile, past):
    tm, tm_out = _row_tiles(bsz * t)
    z9 = _inproj_even(x_bf, w_in_all, e, tm).reshape(N_GROUPS_EVEN, bsz, t, 1024)
    gates = _gates(x_bf, w_gate, gbias, tm)
    if past is None:
        attn = _moba_prompt(z9, bias_tile, bsz, t)
        c0 = jnp.zeros((bsz, B_HEADS, B_HEAD_DIM, B_HEAD_DIM), F32)
        n0 = jnp.zeros((bsz, B_HEADS, B_HEAD_DIM), F32)
        m0 = jnp.zeros((bsz, B_HEADS), F32)
        L = math.gcd(t, MLSTM_PROMPT_CHUNK)
    else:
        cache_k, cache_v, page_table, c0, n0, m0 = past
        attn = _moba_sample(z9, bias_tile, cache_k, cache_v, e, page_table, bsz, t)
        L = math.gcd(t, MLSTM_CHUNK)
    hm, c1, n1, m1 = _mlstm(z9, gates, norm_g, c0, n0, m0, bsz, t, L)
    xn, xn_bf = _out_ln([attn.reshape(bsz * t, A_WIDTH), hm.reshape(bsz * t, B_WIDTH)],
                        [w_out_a, w_out_b], x, ln_g, ln_b, tm_out)
    ka = z9[1].reshape(bsz, t, A_HEADS, A_HEAD_DIM)
    va = z9[2].reshape(bsz, t, A_HEADS, A_HEAD_DIM)
    return xn, xn_bf, (ka, va, c1, n1, m1)


def _odd_layer(x, x_bf, bsz, t, w_in_all, o, w_out, conv_w, ln_g, ln_b, buf):
    tm, tm_out = _row_tiles(bsz * t)
    if buf is None:
        tm = math.gcd(tm, t)
        mix, tail = _odd_mix(x_bf, w_in_all, o, conv_w, tm, 256, t)
        seq_tiles = t // tm
        new_buf = tail.reshape(bsz, seq_tiles, SUBLANES, C_WIDTH)[:, -1, SUBLANES - (CONV_K - 1):, :]
    else:
        plast = jnp.repeat(buf[:, 1, :], t, axis=0)
        pprev = jnp.repeat(buf[:, 0, :], t, axis=0)
        mix, u = _odd_mix(x_bf, w_in_all, o, conv_w, tm, 256, t, state=(plast, pprev))
        new_buf = u.reshape(bsz, t, C_WIDTH)[:, t - (CONV_K - 1):, :]
    xn, xn_bf = _out_ln([mix], [w_out], x, ln_g, ln_b, tm_out)
    return xn, xn_bf, new_buf


def kernel(x_prompt, x_sample, cache_k, cache_v, page_table, state_C, state_n, state_m, state_conv,
           w_in_even, w_out_even, mlstm_gate_bias, mlstm_norm_g, rel_bias,
           w_in_odd, w_out_odd, conv_w, ln_g, ln_b):
    bp, tp, d = x_prompt.shape
    bs, ts, _ = x_sample.shape
    bias_p = _bias_tiles(rel_bias, _prompt_bias_idx())
    bias_s = _bias_tiles(rel_bias, _sample_bias_idx(ts)).reshape(A_HEADS * ts, -1)

    xp = x_prompt.reshape(bp * tp, d)
    xs = x_sample.reshape(bs * ts, d)
    xp_bf = xp.astype(BF16)
    xs_bf = xs.astype(BF16)

    pk, pv, pc, pn, pm, pb = [], [], [], [], [], []
    sk, sv, sc, sn, sm, sb = [], [], [], [], [], []
    for layer in range(DEPTH):
        if layer % 2 == 0:
            e = layer // 2
            w_gate = jnp.pad(w_in_even[e, :, EVEN_MAIN:], ((0, 0), (0, LANES - 2 * B_HEADS))).astype(BF16)
            gbias = jnp.pad(mlstm_gate_bias[e], (0, LANES - 2 * B_HEADS)).reshape(1, LANES).astype(F32)
            w_out_a = w_out_even[e, :A_WIDTH].astype(BF16)
            w_out_b = w_out_even[e, A_WIDTH:].astype(BF16)
            xp, xp_bf, (k1, v1, c1, n1, m1) = _even_layer(
                xp, xp_bf, bp, tp, w_in_even, e, w_gate, gbias, w_out_a, w_out_b, mlstm_norm_g[e],
                ln_g[layer], ln_b[layer], bias_p, None)
            xs, xs_bf, (k2, v2, c2, n2, m2) = _even_layer(
                xs, xs_bf, bs, ts, w_in_even, e, w_gate, gbias, w_out_a, w_out_b, mlstm_norm_g[e],
                ln_g[layer], ln_b[layer], bias_s,
                (cache_k, cache_v, page_table, state_C[e], state_n[e], state_m[e]))
            pk.append(k1); pv.append(v1); pc.append(c1); pn.append(n1); pm.append(m1)
            sk.append(k2); sv.append(v2); sc.append(c2); sn.append(n2); sm.append(m2)
        else:
            o = layer // 2
            w_out = w_out_odd[o].astype(BF16)
            xp, xp_bf, b1 = _odd_layer(xp, xp_bf, bp, tp, w_in_odd, o, w_out, conv_w[o],
                                       ln_g[layer], ln_b[layer], None)
            xs, xs_bf, b2 = _odd_layer(xs, xs_bf, bs, ts, w_in_odd, o, w_out, conv_w[o],
                                       ln_g[layer], ln_b[layer], state_conv[o])
            pb.append(b1); sb.append(b2)
    return (xp.reshape(bp, tp, d), xs.reshape(bs, ts, d),
            jnp.stack(pk), jnp.stack(pv), jnp.stack(pc), jnp.stack(pn), jnp.stack(pm), jnp.stack(pb),
            jnp.stack(sk), jnp.stack(sv), jnp.stack(sc), jnp.stack(sn), jnp.stack(sm), jnp.stack(sb))
```

```python
import functools
import math

import numpy as np
import jax
import jax.numpy as jnp
from jax import lax
from jax.experimental import pallas as pl
from jax.experimental.pallas import tpu as pltpu

F32 = jnp.float32
BF16 = jnp.bfloat16

D_MODEL = 2048
DEPTH = 4
PAGE_SIZE = 128
A_HEADS = 8
A_HEAD_DIM = 128
A_WIDTH = A_HEADS * A_HEAD_DIM
MOBA_BLOCK = 256
MOBA_TOPK = 3
REL_BUCKETS = 32
REL_MAX_DIST = 128
B_HEADS = 4
B_HEAD_DIM = 256
B_WIDTH = B_HEADS * B_HEAD_DIM
MLSTM_CHUNK = 64
MLSTM_PROMPT_CHUNK = 256
C_WIDTH = D_MODEL
CONV_K = 3
N_GROUPS_EVEN = 9
EVEN_MAIN = N_GROUPS_EVEN * 1024
ALPHA = (2.0 * DEPTH) ** 0.25
LN_EPS = 1e-5

LANES = 128
SUBLANES = 8
VMEM_LIMIT = 56 * 1024 * 1024

NEG = -1e30


def _params(*sem):
    return pltpu.CompilerParams(dimension_semantics=sem, vmem_limit_bytes=VMEM_LIMIT)


def _nt(a, b):
    return lax.dot_general(a, b, (((1,), (1,)), ((), ())), preferred_element_type=F32)


def _silu(x):
    return x * jax.nn.sigmoid(x)


def _bucket_np(d):
    d = np.asarray(d, np.int64)
    n = np.maximum(d, 0)
    max_exact = REL_BUCKETS // 2
    nf = np.maximum(n, 1).astype(np.float64)
    large = max_exact + (np.log(nf / max_exact) / math.log(REL_MAX_DIST / max_exact)
                         * (REL_BUCKETS - max_exact)).astype(np.int64)
    large = np.minimum(large, REL_BUCKETS - 1)
    out = np.where(n < max_exact, n, large)
    return np.where(d < 0, -1, out).astype(np.int32)


def _bias_kernel(tab_ref, idx_ref, o_ref):
    h = pl.program_id(0)
    idx = idx_ref[...]
    out = jnp.full(idx.shape, NEG, F32)
    for b in range(REL_BUCKETS):
        out = jnp.where(idx == b, tab_ref[h, b], out)
    o_ref[...] = out


def _bias_tiles(rel_bias, idx):
    r, c = idx.shape
    return pl.pallas_call(
        _bias_kernel,
        grid=(A_HEADS,),
        in_specs=[pl.BlockSpec(memory_space=pltpu.SMEM),
                  pl.BlockSpec((r, c), lambda h: (0, 0))],
        out_specs=pl.BlockSpec((None, r, c), lambda h: (h, 0, 0)),
        out_shape=jax.ShapeDtypeStruct((A_HEADS, r, c), F32),
        compiler_params=_params("parallel"),
        name="bias_tiles",
    )(rel_bias.astype(F32), jnp.asarray(idx))


def _prompt_bias_idx():
    qi = np.arange(MOBA_BLOCK)[:, None]
    kj = np.arange(2 * MOBA_BLOCK)[None, :]
    return _bucket_np(qi + MOBA_BLOCK - kj)


def _sample_bias_idx(s_len):
    s = np.arange(s_len)[:, None]
    lane = np.arange(2 * PAGE_SIZE * A_HEADS)[None, :]
    off = lane // A_HEADS
    far = _bucket_np(MOBA_BLOCK + s - off)
    lo = np.arange(LANES)[None, :]
    own = _bucket_np(np.where(lo < A_HEADS * s_len, s - (lo % s_len), -1))
    return np.concatenate([far, own], axis=1)


def _inproj_kernel(x_ref, w_ref, *rest, heads_out):
    if heads_out:
        o_ref, oh_ref, wb_ref = rest
    else:
        o_ref, wb_ref = rest

    @pl.when(pl.program_id(1) == 0)
    def _():
        wb_ref[...] = w_ref[...].astype(BF16)

    y = _nt(x_ref[...], wb_ref[...])
    o_ref[...] = y
    if heads_out:
        oh_ref[...] = y.reshape(oh_ref.shape)


def _inproj_even(x_bf, wt_all, e, groups, tm, heads_out=False):
    m, k = x_bf.shape
    gw = 1024
    first, skip, n_groups = groups
    gmap = lambda j: first + j + skip * jnp.minimum(j, 1)
    out_specs = [pl.BlockSpec((None, tm, gw), lambda j, i: (j, i, 0))]
    out_shape = [jax.ShapeDtypeStruct((n_groups, m, gw), F32)]
    if heads_out:
        assert n_groups == 1
        out_specs.append(pl.BlockSpec((tm, A_HEADS, A_HEAD_DIM), lambda j, i: (i, 0, 0)))
        out_shape.append(jax.ShapeDtypeStruct((m, A_HEADS, A_HEAD_DIM), F32))
    return pl.pallas_call(
        functools.partial(_inproj_kernel, heads_out=heads_out),
        grid=(n_groups, m // tm),
        in_specs=[pl.BlockSpec((tm, k), lambda j, i: (i, 0)),
                  pl.BlockSpec((None, gw, k), lambda j, i: (e, gmap(j), 0))],
        out_specs=out_specs,
        out_shape=out_shape,
        scratch_shapes=[pltpu.VMEM((gw, k), BF16)],
        compiler_params=_params("parallel", "arbitrary"),
        name="inproj_even",
    )(x_bf, wt_all)


def _gates_kernel(x_ref, w_ref, b_ref, o_ref):
    w = w_ref[...]
    wp = jnp.concatenate([w, jnp.zeros((LANES - w.shape[0], w.shape[1]), F32)], axis=0)
    o_ref[...] = _nt(x_ref[...], wp.astype(BF16)) + b_ref[...]


def _gates(x_bf, wt_all, e, gbias, tm):
    m, k = x_bf.shape
    ng = 2 * B_HEADS
    assert ng == SUBLANES and EVEN_MAIN % ng == 0
    return pl.pallas_call(
        _gates_kernel,
        grid=(m // tm,),
        in_specs=[pl.BlockSpec((tm, k), lambda i: (i, 0)),
                  pl.BlockSpec((None, ng, k), lambda i: (e, EVEN_MAIN // ng, 0)),
                  pl.BlockSpec((1, LANES), lambda i: (0, 0))],
        out_specs=pl.BlockSpec((tm, LANES), lambda i: (i, 0)),
        out_shape=jax.ShapeDtypeStruct((m, LANES), F32),
        compiler_params=_params("parallel"),
        name="mlstm_gates",
    )(x_bf, wt_all, gbias)


def _moba_prompt_kernel(q_ref, k_ref, v_ref, g_ref, bias_ref, o_ref, kmean_ref, kb_ref, vb_ref,
                        *, nblk, scale):
    blk = MOBA_BLOCK
    kb_ref[...] = k_ref[...].astype(BF16)
    vb_ref[...] = v_ref[...].astype(BF16)
    for j in range(nblk):
        kmean_ref[j:j + 1, :] = jnp.mean(k_ref[j * blk:(j + 1) * blk, :], axis=0, keepdims=True)
    far_bias = bias_ref[0:1, 0:1]
    eye = (lax.broadcasted_iota(jnp.int32, (blk, blk), 0)
           == lax.broadcasted_iota(jnp.int32, (blk, blk), 1)).astype(BF16)
    rowi = lax.broadcasted_iota(jnp.int32, (nblk, blk), 0)

    for i in range(nblk):
        q = q_ref[i * blk:(i + 1) * blk, :]
        negm = None
        if i > MOBA_TOPK:
            gt = lax.dot_general(kmean_ref[...], q, (((1,), (1,)), ((), ())),
                                 precision=lax.Precision.HIGHEST, preferred_element_type=F32)
            valid = rowi < i
            sel_t = jnp.zeros((nblk, blk), F32)
            for j in range(i):
                gj = gt[j:j + 1, :]
                beats = ((gt > gj) | ((gt == gj) & (rowi < j))) & valid
                rank = jnp.sum(beats.astype(F32), axis=0, keepdims=True)
                sel_t = jnp.where(rowi == j, (rank < MOBA_TOPK).astype(F32), sel_t)
            sel_t = jnp.concatenate([sel_t, jnp.zeros((LANES - nblk, blk), F32)], axis=0)
            sel = _nt(eye, sel_t.astype(BF16))
            negm = (sel - 1.0) * (-NEG)
        w = (i + 1) * blk
        s = _nt((q * scale).astype(BF16), kb_ref[0:w, :])
        pieces = []
        for j in range(i + 1):
            sj = s[:, j * blk:(j + 1) * blk]
            if j == i:
                sj = sj + bias_ref[:, blk:2 * blk]
            elif j == i - 1:
                sj = sj + bias_ref[:, 0:blk]
                if negm is not None:
                    sj = sj + negm[:, j:j + 1]
            else:
                sj = sj + (far_bias if negm is None else far_bias + negm[:, j:j + 1])
            pieces.append(sj)
        m = jnp.max(functools.reduce(jnp.maximum, pieces), axis=1, keepdims=True)
        ps = [jnp.exp(sj - m) for sj in pieces]
        l = jnp.sum(functools.reduce(jnp.add, ps), axis=1, keepdims=True)
        pcat = ps[0].astype(BF16) if i == 0 else jnp.concatenate([p.astype(BF16) for p in ps], axis=1)
        acc = jnp.dot(pcat, vb_ref[0:w, :], preferred_element_type=F32)
        g = g_ref[i * blk:(i + 1) * blk, :]
        o_ref[i * blk:(i + 1) * blk, :] = (acc / l * _silu(g)).astype(o_ref.dtype)


def _moba_prompt(zm, zk, zv, bias_p, bsz, t):
    blk = MOBA_BLOCK
    nblk = t // blk
    assert nblk <= SUBLANES
    dh = A_HEAD_DIM
    kern = functools.partial(_moba_prompt_kernel, nblk=nblk, scale=dh ** -0.5)
    zspec = lambda g: pl.BlockSpec((None, None, t, dh), lambda b, h: (g, b, 0, h))
    return pl.pallas_call(
        kern,
        grid=(bsz, A_HEADS),
        in_specs=[zspec(0), zspec(0), zspec(0), zspec(1),
                  pl.BlockSpec((None, blk, 2 * blk), lambda b, h: (h, 0, 0))],
        out_specs=pl.BlockSpec((None, t, dh), lambda b, h: (b, 0, h)),
        out_shape=jax.ShapeDtypeStruct((bsz, t, A_WIDTH), BF16),
        scratch_shapes=[pltpu.VMEM((nblk, dh), F32),
                        pltpu.VMEM((t, dh), BF16),
                        pltpu.VMEM((t, dh), BF16)],
        compiler_params=_params("parallel", "parallel"),
        name="moba_prompt",
    )(zm, zk, zv, zm, bias_p)


def _stack_heads(x):
    return jnp.concatenate([x[:, h * A_HEAD_DIM:(h + 1) * A_HEAD_DIM] for h in range(A_HEADS)], axis=0)


def _moba_sample_kernel(pt_ref, q_ref, kn_ref, vn_ref, ga_ref, bias_ref, *rest,
                        nb, bps, s_len, scale):
    del pt_ref
    npg = 2 * bps
    k_refs = rest[:npg]
    v_refs = rest[npg:2 * npg]
    o_ref, qm_ref, gate_ref, m_ref, l_ref, oblk_ref = rest[2 * npg:]
    n = pl.program_id(1)
    rows = A_HEADS * s_len
    s_shift = s_len.bit_length() - 1
    far_w = PAGE_SIZE * A_HEADS

    @pl.when(n == 0)
    def _():
        qm_ref[...] = _stack_heads(q_ref[...])
        gate_ref[...] = jnp.full_like(gate_ref, NEG)
        m_ref[...] = jnp.full_like(m_ref, NEG)
        l_ref[...] = jnp.zeros_like(l_ref)

    qb = qm_ref[...].astype(BF16)
    row = lax.broadcasted_iota(jnp.int32, (rows, far_w), 0)
    lane = lax.broadcasted_iota(jnp.int32, (rows, far_w), 1)
    useful = (lane & (A_HEADS - 1)) == (row >> s_shift)
    lane_b = lax.broadcasted_iota(jnp.int32, (rows, LANES), 1)
    last = n == pl.num_programs(1) - 1
    far_bias = bias_ref[:, 0:1]

    def pages(ref):
        return ref[...].reshape(far_w, A_HEAD_DIM).astype(BF16)

    gate_new = gate_ref[...]
    m_new = m_ref[...]
    l_new = l_ref[...]
    for c in range(bps):
        blk = n * bps + c
        raw0 = _nt(qb, pages(k_refs[2 * c]))
        raw1 = _nt(qb, pages(k_refs[2 * c + 1]))
        gsum = jnp.sum(jnp.where(useful, raw0 + raw1, 0.0), axis=1, keepdims=True)
        if c == bps - 1:
            last_v = (jnp.zeros((rows, far_w), jnp.int32) + blk) == nb - 1
            b0 = jnp.where(last_v, bias_ref[:, 0:far_w], far_bias)
            b1 = jnp.where(last_v, bias_ref[:, far_w:2 * far_w], far_bias)
        else:
            b0 = b1 = far_bias
        s0 = jnp.where(useful, raw0 * scale + b0, NEG)
        s1 = jnp.where(useful, raw1 * scale + b1, NEG)
        mloc = jnp.max(jnp.maximum(s0, s1), axis=1, keepdims=True)
        p0 = jnp.exp(s0 - mloc)
        p1 = jnp.exp(s1 - mloc)
        lloc = jnp.sum(p0 + p1, axis=1, keepdims=True)
        oblk_ref[blk] = (jnp.dot(p0.astype(BF16), pages(v_refs[2 * c]), preferred_element_type=F32)
                         + jnp.dot(p1.astype(BF16), pages(v_refs[2 * c + 1]), preferred_element_type=F32))
        here = lane_b == blk
        gate_new = jnp.where(here, gsum * (1.0 / MOBA_BLOCK), gate_new)
        m_new = jnp.where(here, mloc, m_new)
        l_new = jnp.where(here, lloc, l_new)
    gate_ref[...] = gate_new
    m_ref[...] = m_new
    l_ref[...] = l_new

    @pl.when(last)
    def _():
        row_b = lax.broadcasted_iota(jnp.int32, (rows, LANES), 0)
        lane_f = lane_b.astype(F32)
        g = jnp.where(lane_b < nb, gate_ref[...], -jnp.inf)
        sel = jnp.zeros((rows, LANES), jnp.bool_)
        for _r in range(min(MOBA_TOPK, nb)):
            mx = jnp.max(g, axis=1, keepdims=True)
            idx = jnp.min(jnp.where(g == mx, lane_f, float(LANES)), axis=1, keepdims=True)
            hit = lane_f == idx
            sel = sel | hit
            g = jnp.where(hit, -jnp.inf, g)
        pad = jnp.zeros((LANES - rows, A_HEAD_DIM), F32)
        kn = jnp.concatenate([_stack_heads(kn_ref[...]), pad], axis=0).astype(BF16)
        vn = jnp.concatenate([_stack_heads(vn_ref[...]), pad], axis=0).astype(BF16)
        so = _nt(qb, kn) * scale + bias_ref[:, 2 * far_w:2 * far_w + LANES]
        ok = ((lane_b < rows) & ((lane_b >> s_shift) == (row_b >> s_shift))
              & ((lane_b & (s_len - 1)) <= (row_b & (s_len - 1))))
        so = jnp.where(ok, so, NEG)
        mm = m_ref[...]
        mtot = jnp.maximum(jnp.max(jnp.where(sel, mm, NEG), axis=1, keepdims=True),
                           jnp.max(so, axis=1, keepdims=True))
        w = jnp.where(sel, jnp.exp(mm - mtot), 0.0)
        po = jnp.exp(so - mtot)
        lsum = jnp.sum(w * l_ref[...], axis=1, keepdims=True) + jnp.sum(po, axis=1, keepdims=True)
        acc = jnp.dot(po.astype(BF16), vn, preferred_element_type=F32)
        for nn in range(nb):
            acc = acc + w[:, nn:nn + 1] * oblk_ref[nn]
        out = acc / lsum
        res = jnp.concatenate([out[h * s_len:(h + 1) * s_len, :] for h in range(A_HEADS)], axis=1)
        o_ref[...] = (res * _silu(ga_ref[...])).astype(o_ref.dtype)


def _moba_sample(zm, zk, zv, bias_s, cache_k, cache_v, layer, page_table, bsz, s_len, bps=4):
    n_pages = page_table.shape[1]
    ppb = MOBA_BLOCK // PAGE_SIZE
    assert ppb == 2 and n_pages % ppb == 0 and s_len == SUBLANES and A_HEADS == SUBLANES
    nb = n_pages // ppb
    assert nb <= LANES and nb % bps == 0
    rows = A_HEADS * s_len
    dh = A_HEAD_DIM
    kern = functools.partial(_moba_sample_kernel, nb=nb, bps=bps, s_len=s_len, scale=dh ** -0.5)
    tok = lambda g: pl.BlockSpec((None, None, s_len, A_WIDTH), lambda b, n, pt: (g, b, 0, 0))
    page = lambda o: pl.BlockSpec((None, None, PAGE_SIZE, A_HEADS, dh),
                                  lambda b, n, pt: (layer, pt[b, ppb * bps * n + o], 0, 0, 0))
    pages = [page(o) for o in range(ppb * bps)]
    grid_spec = pltpu.PrefetchScalarGridSpec(
        num_scalar_prefetch=1,
        grid=(bsz, nb // bps),
        in_specs=[tok(0), tok(0), tok(0), tok(1),
                  pl.BlockSpec(bias_s.shape, lambda b, n, pt: (0, 0))] + pages + pages,
        out_specs=pl.BlockSpec((None, s_len, A_WIDTH), lambda b, n, pt: (b, 0, 0)),
        scratch_shapes=[pltpu.VMEM((rows, dh), F32),
                        pltpu.VMEM((rows, LANES), F32),
                        pltpu.VMEM((rows, LANES), F32),
                        pltpu.VMEM((rows, LANES), F32),
                        pltpu.VMEM((nb, rows, dh), F32)])
    return pl.pallas_call(
        kern,
        grid_spec=grid_spec,
        out_shape=jax.ShapeDtypeStruct((bsz, s_len, A_WIDTH), BF16),
        compiler_params=_params("parallel", "arbitrary"),
        name="moba_sample",
    )(page_table, zm, zk, zv, zm, bias_s, *([cache_k] * (ppb * bps)), *([cache_v] * (ppb * bps)))


def _mlstm_kernel(q_ref, k_ref, v_ref, og_ref, gm_ref, g_ref, gt_ref, ng_ref, c0_ref, n0_ref, m0_ref,
                  h_ref, c_out, n_out, m_out, c_scr, n_scr, m_scr, *, L, Lp):
    hh = pl.program_id(1)
    c = pl.program_id(2)
    dk = B_HEAD_DIM

    @pl.when(c == 0)
    def _():
        c_scr[...] = c0_ref[...]
        n_scr[...] = n0_ref[...]
        m_scr[...] = m0_ref[...]

    def padded(x):
        if L == Lp:
            return x
        return jnp.concatenate([x, jnp.zeros((Lp - L, x.shape[1]), x.dtype)], axis=0)

    q = padded(q_ref[...])
    k = padded(k_ref[...]) * (dk ** -0.5)
    v = padded(v_ref[...])
    g = padded(g_ref[...])
    lane = lax.broadcasted_iota(jnp.int32, g.shape, 1)
    li_col = jnp.sum(jnp.where(lane == hh, g, 0.0), axis=1, keepdims=True)
    fp_col = jnp.sum(jnp.where(lane == hh + B_HEADS, g, 0.0), axis=1, keepdims=True)
    li_row = gt_ref[pl.ds(hh, 1), :]
    fp_row = gt_ref[pl.ds(hh + B_HEADS, 1), :]
    lf_col = jax.nn.log_sigmoid(fp_col)
    lf_row = jax.nn.log_sigmoid(fp_row)
    if L != Lp:
        rv = lax.broadcasted_iota(jnp.int32, (Lp, 1), 0) < L
        cv = lax.broadcasted_iota(jnp.int32, (1, Lp), 1) < L
        li_col = jnp.where(rv, li_col, NEG)
        lf_col = jnp.where(rv, lf_col, 0.0)
        li_row = jnp.where(cv, li_row, NEG)
        lf_row = jnp.where(cv, lf_row, 0.0)

    r = lax.broadcasted_iota(jnp.int32, (Lp, Lp), 0)
    cc = lax.broadcasted_iota(jnp.int32, (Lp, Lp), 1)
    tri = r >= cc
    b_col = jnp.sum(jnp.where(tri, lf_row, 0.0), axis=1, keepdims=True)
    b_row = jnp.sum(jnp.where(r <= cc, lf_col, 0.0), axis=0, keepdims=True)
    m_prev = m_scr[...]
    log_d = jnp.where(tri, b_col - b_row + li_row, NEG)
    m_inter = b_col + m_prev
    m_t = jnp.maximum(m_inter, jnp.max(log_d, axis=1, keepdims=True))
    qb = q.astype(BF16)
    kb = k.astype(BF16)
    vb = v.astype(BF16)
    s = _nt(qb, kb) * jnp.exp(log_d - m_t)
    w_inter = jnp.exp(m_inter - m_t)
    c_old = c_scr[...]
    n_old = n_scr[...]
    num = (jnp.dot(s.astype(BF16), vb, preferred_element_type=F32)
           + w_inter * jnp.dot(qb, c_old.astype(BF16), preferred_element_type=F32))
    den = jnp.sum(s, axis=1, keepdims=True) + w_inter * jnp.sum(q * n_old, axis=1, keepdims=True)
    h = num / jnp.maximum(jnp.abs(den), jnp.exp(-m_t))

    b_last = b_row[:, L - 1:L]
    m_new = jnp.maximum(b_last + m_prev, jnp.max(b_last - b_row + li_row, axis=1, keepdims=True))
    decay = jnp.exp(b_last + m_prev - m_new)
    wk = jnp.exp(b_last - b_col + li_col - m_new)
    kw = k * wk
    c_new = decay * c_old + lax.dot_general(kw.astype(BF16), vb, (((0,), (0,)), ((), ())),
                                            preferred_element_type=F32)
    n_new = decay * n_old + jnp.sum(kw, axis=0, keepdims=True)
    c_scr[...] = c_new
    n_scr[...] = n_new
    m_scr[...] = m_new

    hv = h[:L]
    mu = jnp.mean(hv, axis=1, keepdims=True)
    var = jnp.mean(jnp.square(hv - mu), axis=1, keepdims=True)
    hn = (hv - mu) * lax.rsqrt(var + LN_EPS) * ng_ref[...]
    h_ref[...] = (hn * jax.nn.sigmoid(og_ref[...]) * _silu(gm_ref[...])).astype(h_ref.dtype)

    @pl.when(c == pl.num_programs(2) - 1)
    def _():
        c_out[...] = c_new
        n_out[...] = n_new
        m_out[...] = m_new


def _mlstm(zm, gates, norm_g, c0, n0, m0, bsz, t, L):
    Lp = max(L, LANES)
    nc = t // L
    dh = B_HEAD_DIM
    g3 = gates.reshape(bsz, t, LANES)
    gt = jnp.swapaxes(gates[:, :2 * B_HEADS].reshape(bsz, nc, L, 2 * B_HEADS), 2, 3)
    if Lp != L:
        gt = jnp.pad(gt, ((0, 0), (0, 0), (0, 0), (0, Lp - L)))
    ng = norm_g.reshape(B_HEADS, 1, dh).astype(F32)
    n0 = n0.reshape(bsz, B_HEADS, 1, dh)
    m0 = m0.reshape(bsz, B_HEADS, 1, 1)
    kern = functools.partial(_mlstm_kernel, L=L, Lp=Lp)
    zspec = lambda gidx: pl.BlockSpec((None, None, L, dh), lambda b, h, c: (gidx, b, c, h))
    st = lambda r, cdim: pl.BlockSpec((None, None, r, cdim), lambda b, h, c: (b, h, 0, 0))
    h, c1, n1, m1 = pl.pallas_call(
        kern,
        grid=(bsz, B_HEADS, nc),
        in_specs=[zspec(2), zspec(3), zspec(4), zspec(5), zspec(6),
                  pl.BlockSpec((None, L, LANES), lambda b, h, c: (b, c, 0)),
                  pl.BlockSpec((None, None, 2 * B_HEADS, Lp), lambda b, h, c: (b, c, 0, 0)),
                  pl.BlockSpec((None, 1, dh), lambda b, h, c: (h, 0, 0)),
                  st(dh, dh), st(1, dh), st(1, 1)],
        out_specs=[pl.BlockSpec((None, L, dh), lambda b, h, c: (b, c, h)),
                   st(dh, dh), st(1, dh), st(1, 1)],
        out_shape=[jax.ShapeDtypeStruct((bsz, t, B_WIDTH), BF16),
                   jax.ShapeDtypeStruct((bsz, B_HEADS, dh, dh), F32),
                   jax.ShapeDtypeStruct((bsz, B_HEADS, 1, dh), F32),
                   jax.ShapeDtypeStruct((bsz, B_HEADS, 1, 1), F32)],
        scratch_shapes=[pltpu.VMEM((dh, dh), F32), pltpu.VMEM((1, dh), F32), pltpu.VMEM((1, 1), F32)],
        compiler_params=_params("parallel", "parallel", "arbitrary"),
        name="mlstm",
    )(zm, zm, zm, zm, zm, g3, gt, ng, c0, n0, m0)
    return h, c1, n1.reshape(bsz, B_HEADS, dh), m1.reshape(bsz, B_HEADS)


def _outln_kernel(*refs, n_parts):
    mix = refs[:n_parts]
    ws = refs[n_parts:2 * n_parts]
    x_ref, g_ref, b_ref, o_ref, ob_ref = refs[2 * n_parts:]
    tm = x_ref.shape[0]
    sub = min(tm, LANES)
    for r in range(tm // sub):
        rows = pl.ds(r * sub, sub)
        y = jnp.dot(mix[0][rows, :], ws[0][...], preferred_element_type=F32)
        for p in range(1, n_parts):
            y = y + jnp.dot(mix[p][rows, :], ws[p][...], preferred_element_type=F32)
        z = ALPHA * x_ref[rows, :] + y
        mu = jnp.mean(z, axis=1, keepdims=True)
        var = jnp.mean(jnp.square(z - mu), axis=1, keepdims=True)
        xn = (z - mu) * lax.rsqrt(var + LN_EPS) * g_ref[...] + b_ref[...]
        o_ref[rows, :] = xn
        ob_ref[rows, :] = xn.astype(BF16)


def _out_ln(mix_parts, w_parts, x, ln_g, ln_b, tm):
    m, d = x.shape
    n_parts = len(mix_parts)
    in_specs = ([pl.BlockSpec((tm, mp.shape[1]), lambda i: (i, 0)) for mp in mix_parts]
                + [pl.BlockSpec(wp.shape, lambda i: (0, 0)) for wp in w_parts]
                + [pl.BlockSpec((tm, d), lambda i: (i, 0)),
                   pl.BlockSpec((1, d), lambda i: (0, 0)),
                   pl.BlockSpec((1, d), lambda i: (0, 0))])
    return pl.pallas_call(
        functools.partial(_outln_kernel, n_parts=n_parts),
        grid=(m // tm,),
        in_specs=in_specs,
        out_specs=[pl.BlockSpec((tm, d), lambda i: (i, 0)), pl.BlockSpec((tm, d), lambda i: (i, 0))],
        out_shape=[jax.ShapeDtypeStruct((m, d), F32), jax.ShapeDtypeStruct((m, d), BF16)],
        compiler_params=_params("parallel"),
        name="outproj_ln",
    )(*mix_parts, *w_parts, x, ln_g.reshape(1, d).astype(F32), ln_b.reshape(1, d).astype(F32))


def _odd_kernel(*refs, tm, carry_mode, seq_tiles, seq_rows):
    if carry_mode:
        x_ref, wb, wc, wx, wg, cw_ref, mix_ref, tail_ref, wbf, carry = refs
    else:
        x_ref, wb, wc, wx, wg, cw_ref, plast_ref, pprev_ref, mix_ref, tail_ref, wbf = refs

    @pl.when(pl.program_id(1) == 0)
    def _():
        for gidx, wref in enumerate((wb, wc, wx, wg)):
            wbf[gidx] = wref[...].astype(BF16)

    x = x_ref[...]
    zb = jnp.dot(x, wbf[0], preferred_element_type=F32)
    zc = jnp.dot(x, wbf[1], preferred_element_type=F32)
    zx = jnp.dot(x, wbf[2], preferred_element_type=F32)
    zg = jnp.dot(x, wbf[3], preferred_element_type=F32)
    u = zc * zx
    rows = lax.broadcasted_iota(jnp.int32, u.shape, 0)
    r1 = pltpu.roll(u, 1, 0)
    r2 = pltpu.roll(u, 2, 0)
    if carry_mode:
        @pl.when(pl.program_id(1) % seq_tiles == 0)
        def _():
            carry[...] = jnp.zeros_like(carry)
        last = carry[SUBLANES - 1:SUBLANES, :]
        prev = carry[SUBLANES - 2:SUBLANES - 1, :]
        rm = rows
    else:
        last = plast_ref[...]
        prev = pprev_ref[...]
        rm = rows & (seq_rows - 1)
    u1 = jnp.where(rm == 0, last, r1)
    u2 = jnp.where(rm == 0, prev, jnp.where(rm == 1, last, r2))
    cw = cw_ref[...]
    y = cw[0:1, :] * u2 + cw[1:2, :] * u1 + cw[2:3, :] * u
    mix_ref[...] = (zb * y * _silu(zg)).astype(mix_ref.dtype)
    if carry_mode:
        tail = u[tm - SUBLANES:, :]
        carry[...] = tail
        tail_ref[...] = tail
    else:
        tail_ref[...] = u


def _odd_mix(x_bf, w_all, o, conv_w, tm, tn, seq_len, state=None):
    m, k = x_bf.shape
    nn = C_WIDTH // tn
    carry_mode = state is None
    wspec = lambda gidx: pl.BlockSpec((None, k, tn), lambda j, i: (o, 0, gidx * nn + j))
    in_specs = [pl.BlockSpec((tm, k), lambda j, i: (i, 0)),
                wspec(0), wspec(1), wspec(2), wspec(3),
                pl.BlockSpec((CONV_K, tn), lambda j, i: (0, j))]
    args = [x_bf, w_all, w_all, w_all, w_all, conv_w.astype(F32)]
    scratch = [pltpu.VMEM((4, k, tn), BF16)]
    if carry_mode:
        assert seq_len % tm == 0
        tail_shape = jax.ShapeDtypeStruct((m // tm, SUBLANES, C_WIDTH), F32)
        tail_spec = pl.BlockSpec((None, SUBLANES, tn), lambda j, i: (i, 0, j))
        scratch.append(pltpu.VMEM((SUBLANES, tn), F32))
    else:
        assert tm % seq_len == 0 and seq_len & (seq_len - 1) == 0
        in_specs += [pl.BlockSpec((tm, tn), lambda j, i: (i, j))] * 2
        args += list(state)
        tail_shape = jax.ShapeDtypeStruct((m, C_WIDTH), F32)
        tail_spec = pl.BlockSpec((tm, tn), lambda j, i: (i, j))
    kern = functools.partial(_odd_kernel, tm=tm, carry_mode=carry_mode,
                             seq_tiles=max(seq_len // tm, 1), seq_rows=seq_len)
    return pl.pallas_call(
        kern,
        grid=(nn, m // tm),
        in_specs=in_specs,
        out_specs=[pl.BlockSpec((tm, tn), lambda j, i: (i, j)), tail_spec],
        out_shape=[jax.ShapeDtypeStruct((m, C_WIDTH), BF16), tail_shape],
        scratch_shapes=scratch,
        compiler_params=_params("parallel", "arbitrary"),
        name="odd_mix",
    )(*args)


def _row_tiles(m):
    return min(m, 1024), min(m, 512)


def _even_layer(x, x_bf, bsz, t, wt_all, e, gbias, w_out_a, w_out_b, norm_g, ln_g, ln_b,
                bias_tile, past):
    tm, tm_out = _row_tiles(bsz * t)
    (zm,) = _inproj_even(x_bf, wt_all, e, (0, 2, N_GROUPS_EVEN - 2), tm)
    zk, ka = _inproj_even(x_bf, wt_all, e, (1, 0, 1), tm, heads_out=True)
    zv, va = _inproj_even(x_bf, wt_all, e, (2, 0, 1), tm, heads_out=True)
    zm = zm.reshape(N_GROUPS_EVEN - 2, bsz, t, 1024)
    zk = zk.reshape(1, bsz, t, 1024)
    zv = zv.reshape(1, bsz, t, 1024)
    gates = _gates(x_bf, wt_all, e, gbias, tm)
    if past is None:
        attn = _moba_prompt(zm, zk, zv, bias_tile, bsz, t)
        c0 = jnp.zeros((bsz, B_HEADS, B_HEAD_DIM, B_HEAD_DIM), F32)
        n0 = jnp.zeros((bsz, B_HEADS, B_HEAD_DIM), F32)
        m0 = jnp.zeros((bsz, B_HEADS), F32)
        L = math.gcd(t, MLSTM_PROMPT_CHUNK)
    else:
        cache_k, cache_v, page_table, c0, n0, m0 = past
        attn = _moba_sample(zm, zk, zv, bias_tile, cache_k, cache_v, e, page_table, bsz, t)
        L = math.gcd(t, MLSTM_CHUNK)
    hm, c1, n1, m1 = _mlstm(zm, gates, norm_g, c0, n0, m0, bsz, t, L)
    xn, xn_bf = _out_ln([attn.reshape(bsz * t, A_WIDTH), hm.reshape(bsz * t, B_WIDTH)],
                        [w_out_a, w_out_b], x, ln_g, ln_b, tm_out)
    ka = ka.reshape(bsz, t, A_HEADS, A_HEAD_DIM)
    va = va.reshape(bsz, t, A_HEADS, A_HEAD_DIM)
    return xn, xn_bf, (ka, va, c1, n1, m1)


def _odd_layer(x, x_bf, bsz, t, w_in_all, o, w_out, conv_w, ln_g, ln_b, buf):
    tm, tm_out = _row_tiles(bsz * t)
    if buf is None:
        tm = math.gcd(tm, t)
        mix, tail = _odd_mix(x_bf, w_in_all, o, conv_w, tm, 256, t)
        seq_tiles = t // tm
        new_buf = tail.reshape(bsz, seq_tiles, SUBLANES, C_WIDTH)[:, -1, SUBLANES - (CONV_K - 1):, :]
    else:
        plast = jnp.repeat(buf[:, 1, :], t, axis=0)
        pprev = jnp.repeat(buf[:, 0, :], t, axis=0)
        mix, u = _odd_mix(x_bf, w_in_all, o, conv_w, tm, 256, t, state=(plast, pprev))
        new_buf = u.reshape(bsz, t, C_WIDTH)[:, t - (CONV_K - 1):, :]
    xn, xn_bf = _out_ln([mix], [w_out], x, ln_g, ln_b, tm_out)
    return xn, xn_bf, new_buf


def kernel(x_prompt, x_sample, cache_k, cache_v, page_table, state_C, state_n, state_m, state_conv,
           w_in_even, w_out_even, mlstm_gate_bias, mlstm_norm_g, rel_bias,
           w_in_odd, w_out_odd, conv_w, ln_g, ln_b):
    bp, tp, d = x_prompt.shape
    bs, ts, _ = x_sample.shape
    bias_p = _bias_tiles(rel_bias, _prompt_bias_idx())
    bias_s = _bias_tiles(rel_bias, _sample_bias_idx(ts)).reshape(A_HEADS * ts, -1)

    xp = x_prompt.reshape(bp * tp, d)
    xs = x_sample.reshape(bs * ts, d)
    xp_bf = xp.astype(BF16)
    xs_bf = xs.astype(BF16)
    wt_even = jnp.swapaxes(w_in_even, 1, 2)

    pk, pv, pc, pn, pm, pb = [], [], [], [], [], []
    sk, sv, sc, sn, sm, sb = [], [], [], [], [], []
    for layer in range(DEPTH):
        if layer % 2 == 0:
            e = layer // 2
            gbias = jnp.pad(mlstm_gate_bias[e], (0, LANES - 2 * B_HEADS)).reshape(1, LANES).astype(F32)
            w_out_a = w_out_even[e, :A_WIDTH].astype(BF16)
            w_out_b = w_out_even[e, A_WIDTH:].astype(BF16)
            xp, xp_bf, (k1, v1, c1, n1, m1) = _even_layer(
                xp, xp_bf, bp, tp, wt_even, e, gbias, w_out_a, w_out_b, mlstm_norm_g[e],
                ln_g[layer], ln_b[layer], bias_p, None)
            xs, xs_bf, (k2, v2, c2, n2, m2) = _even_layer(
                xs, xs_bf, bs, ts, wt_even, e, gbias, w_out_a, w_out_b, mlstm_norm_g[e],
                ln_g[layer], ln_b[layer], bias_s,
                (cache_k, cache_v, page_table, state_C[e], state_n[e], state_m[e]))
            pk.append(k1); pv.append(v1); pc.append(c1); pn.append(n1); pm.append(m1)
            sk.append(k2); sv.append(v2); sc.append(c2); sn.append(n2); sm.append(m2)
        else:
            o = layer // 2
            w_out = w_out_odd[o].astype(BF16)
            xp, xp_bf, b1 = _odd_layer(xp, xp_bf, bp, tp, w_in_odd, o, w_out, conv_w[o],
                                       ln_g[layer], ln_b[layer], None)
            xs, xs_bf, b2 = _odd_layer(xs, xs_bf, bs, ts, w_in_odd, o, w_out, conv_w[o],
                                       ln_g[layer], ln_b[layer], state_conv[o])
            pb.append(b1); sb.append(b2)
    return (xp.reshape(bp, tp, d), xs.reshape(bs, ts, d),
            jnp.stack(pk), jnp.stack(pv), jnp.stack(pc), jnp.stack(pn), jnp.stack(pm), jnp.stack(pb),
            jnp.stack(sk), jnp.stack(sv), jnp.stack(sc), jnp.stack(sn), jnp.stack(sm), jnp.stack(sb))
```

```python
import functools
import math

import numpy as np
import jax
import jax.numpy as jnp
from jax import lax
from jax.experimental import pallas as pl
from jax.experimental.pallas import tpu as pltpu

F32 = jnp.float32
BF16 = jnp.bfloat16

D_MODEL = 2048
DEPTH = 4
PAGE_SIZE = 128
A_HEADS = 8
A_HEAD_DIM = 128
A_WIDTH = A_HEADS * A_HEAD_DIM
MOBA_BLOCK = 256
MOBA_TOPK = 3
REL_BUCKETS = 32
REL_MAX_DIST = 128
B_HEADS = 4
B_HEAD_DIM = 256
B_WIDTH = B_HEADS * B_HEAD_DIM
MLSTM_CHUNK = 64
MLSTM_PROMPT_CHUNK = 256
C_WIDTH = D_MODEL
CONV_K = 3
N_GROUPS_EVEN = 9
EVEN_MAIN = N_GROUPS_EVEN * 1024
ALPHA = (2.0 * DEPTH) ** 0.25
LN_EPS = 1e-5

LANES = 128
SUBLANES = 8
VMEM_LIMIT = 56 * 1024 * 1024

NEG = -1e30


def _params(*sem):
    return pltpu.CompilerParams(dimension_semantics=sem, vmem_limit_bytes=VMEM_LIMIT)


def _nt(a, b):
    return lax.dot_general(a, b, (((1,), (1,)), ((), ())), preferred_element_type=F32)


def _silu(x):
    return x * jax.nn.sigmoid(x)


def _bucket_np(d):
    d = np.asarray(d, np.int64)
    n = np.maximum(d, 0)
    max_exact = REL_BUCKETS // 2
    nf = np.maximum(n, 1).astype(np.float64)
    large = max_exact + (np.log(nf / max_exact) / math.log(REL_MAX_DIST / max_exact)
                         * (REL_BUCKETS - max_exact)).astype(np.int64)
    large = np.minimum(large, REL_BUCKETS - 1)
    out = np.where(n < max_exact, n, large)
    return np.where(d < 0, -1, out).astype(np.int32)


def _bias_kernel(tab_ref, idx_ref, o_ref):
    h = pl.program_id(0)
    idx = idx_ref[...]
    out = jnp.full(idx.shape, NEG, F32)
    for b in range(REL_BUCKETS):
        out = jnp.where(idx == b, tab_ref[h, b], out)
    o_ref[...] = out


def _bias_tiles(rel_bias, idx):
    r, c = idx.shape
    return pl.pallas_call(
        _bias_kernel,
        grid=(A_HEADS,),
        in_specs=[pl.BlockSpec(memory_space=pltpu.SMEM),
                  pl.BlockSpec((r, c), lambda h: (0, 0))],
        out_specs=pl.BlockSpec((None, r, c), lambda h: (h, 0, 0)),
        out_shape=jax.ShapeDtypeStruct((A_HEADS, r, c), F32),
        compiler_params=_params("parallel"),
        name="bias_tiles",
    )(rel_bias.astype(F32), jnp.asarray(idx))


def _prompt_bias_idx():
    qi = np.arange(MOBA_BLOCK)[:, None]
    kj = np.arange(2 * MOBA_BLOCK)[None, :]
    return _bucket_np(qi + MOBA_BLOCK - kj)


def _sample_bias_idx(s_len):
    s = np.arange(s_len)[:, None]
    lane = np.arange(2 * PAGE_SIZE * A_HEADS)[None, :]
    off = lane // A_HEADS
    far = _bucket_np(MOBA_BLOCK + s - off)
    lo = np.arange(LANES)[None, :]
    own = _bucket_np(np.where(lo < A_HEADS * s_len, s - (lo % s_len), -1))
    return np.concatenate([far, own], axis=1)


def _inproj_kernel(x_ref, w_ref, *rest, heads_out):
    if heads_out:
        o_ref, oh_ref, wb_ref = rest
    else:
        o_ref, wb_ref = rest

    @pl.when(pl.program_id(1) == 0)
    def _():
        wb_ref[...] = w_ref[...].astype(BF16)

    y = _nt(x_ref[...], wb_ref[...])
    o_ref[...] = y
    if heads_out:
        oh_ref[...] = y.reshape(oh_ref.shape)


def _inproj_even(x_bf, wt_all, e, groups, tm, heads_out=False):
    m, k = x_bf.shape
    gw = 1024
    first, skip, n_groups = groups
    gmap = lambda j: first + j + skip * jnp.minimum(j, 1)
    out_specs = [pl.BlockSpec((None, tm, gw), lambda j, i: (j, i, 0))]
    out_shape = [jax.ShapeDtypeStruct((n_groups, m, gw), F32)]
    if heads_out:
        assert n_groups == 1
        out_specs.append(pl.BlockSpec((tm, A_HEADS, A_HEAD_DIM), lambda j, i: (i, 0, 0)))
        out_shape.append(jax.ShapeDtypeStruct((m, A_HEADS, A_HEAD_DIM), F32))
    return pl.pallas_call(
        functools.partial(_inproj_kernel, heads_out=heads_out),
        grid=(n_groups, m // tm),
        in_specs=[pl.BlockSpec((tm, k), lambda j, i: (i, 0)),
                  pl.BlockSpec((None, gw, k), lambda j, i: (e, gmap(j), 0))],
        out_specs=out_specs,
        out_shape=out_shape,
        scratch_shapes=[pltpu.VMEM((gw, k), BF16)],
        compiler_params=_params("parallel", "arbitrary"),
        name="inproj_even",
    )(x_bf, wt_all)


def _gates_kernel(x_ref, w_ref, b_ref, o_ref):
    w = w_ref[...]
    wp = jnp.concatenate([w, jnp.zeros((LANES - w.shape[0], w.shape[1]), F32)], axis=0)
    o_ref[...] = _nt(x_ref[...], wp.astype(BF16)) + b_ref[...]


def _gates(x_bf, wt_all, e, gbias, tm):
    m, k = x_bf.shape
    ng = 2 * B_HEADS
    assert ng == SUBLANES and EVEN_MAIN % ng == 0
    return pl.pallas_call(
        _gates_kernel,
        grid=(m // tm,),
        in_specs=[pl.BlockSpec((tm, k), lambda i: (i, 0)),
                  pl.BlockSpec((None, ng, k), lambda i: (e, EVEN_MAIN // ng, 0)),
                  pl.BlockSpec((1, LANES), lambda i: (0, 0))],
        out_specs=pl.BlockSpec((tm, LANES), lambda i: (i, 0)),
        out_shape=jax.ShapeDtypeStruct((m, LANES), F32),
        compiler_params=_params("parallel"),
        name="mlstm_gates",
    )(x_bf, wt_all, gbias)


def _moba_prompt_kernel(q_ref, k_ref, v_ref, g_ref, bias_ref, o_ref, kmean_ref, kb_ref, vb_ref,
                        *, nblk, scale):
    blk = MOBA_BLOCK
    kb_ref[...] = k_ref[...].astype(BF16)
    vb_ref[...] = v_ref[...].astype(BF16)
    for j in range(nblk):
        kmean_ref[j:j + 1, :] = jnp.mean(k_ref[j * blk:(j + 1) * blk, :], axis=0, keepdims=True)
    far_bias = bias_ref[0:1, 0:1]
    eye = (lax.broadcasted_iota(jnp.int32, (blk, blk), 0)
           == lax.broadcasted_iota(jnp.int32, (blk, blk), 1)).astype(BF16)
    rowi = lax.broadcasted_iota(jnp.int32, (nblk, blk), 0)

    for i in range(nblk):
        q = q_ref[i * blk:(i + 1) * blk, :]
        negm = None
        if i > MOBA_TOPK:
            gt = lax.dot_general(kmean_ref[...], q, (((1,), (1,)), ((), ())),
                                 precision=lax.Precision.HIGHEST, preferred_element_type=F32)
            valid = rowi < i
            sel_t = jnp.zeros((nblk, blk), F32)
            for j in range(i):
                gj = gt[j:j + 1, :]
                beats = ((gt > gj) | ((gt == gj) & (rowi < j))) & valid
                rank = jnp.sum(beats.astype(F32), axis=0, keepdims=True)
                sel_t = jnp.where(rowi == j, (rank < MOBA_TOPK).astype(F32), sel_t)
            sel_t = jnp.concatenate([sel_t, jnp.zeros((LANES - nblk, blk), F32)], axis=0)
            sel = _nt(eye, sel_t.astype(BF16))
            negm = (sel - 1.0) * (-NEG)
        w = (i + 1) * blk
        s = _nt((q * scale).astype(BF16), kb_ref[0:w, :])
        pieces = []
        for j in range(i + 1):
            sj = s[:, j * blk:(j + 1) * blk]
            if j == i:
                sj = sj + bias_ref[:, blk:2 * blk]
            elif j == i - 1:
                sj = sj + bias_ref[:, 0:blk]
                if negm is not None:
                    sj = sj + negm[:, j:j + 1]
            else:
                sj = sj + (far_bias if negm is None else far_bias + negm[:, j:j + 1])
            pieces.append(sj)
        m = jnp.max(functools.reduce(jnp.maximum, pieces), axis=1, keepdims=True)
        ps = [jnp.exp(sj - m) for sj in pieces]
        l = jnp.sum(functools.reduce(jnp.add, ps), axis=1, keepdims=True)
        pcat = ps[0].astype(BF16) if i == 0 else jnp.concatenate([p.astype(BF16) for p in ps], axis=1)
        acc = jnp.dot(pcat, vb_ref[0:w, :], preferred_element_type=F32)
        g = g_ref[i * blk:(i + 1) * blk, :]
        o_ref[i * blk:(i + 1) * blk, :] = (acc / l * _silu(g)).astype(o_ref.dtype)


def _moba_prompt(zm, zk, zv, bias_p, bsz, t):
    blk = MOBA_BLOCK
    nblk = t // blk
    assert nblk <= SUBLANES
    dh = A_HEAD_DIM
    kern = functools.partial(_moba_prompt_kernel, nblk=nblk, scale=dh ** -0.5)
    zspec = lambda g: pl.BlockSpec((None, None, t, dh), lambda b, h: (g, b, 0, h))
    return pl.pallas_call(
        kern,
        grid=(bsz, A_HEADS),
        in_specs=[zspec(0), zspec(0), zspec(0), zspec(1),
                  pl.BlockSpec((None, blk, 2 * blk), lambda b, h: (h, 0, 0))],
        out_specs=pl.BlockSpec((None, t, dh), lambda b, h: (b, 0, h)),
        out_shape=jax.ShapeDtypeStruct((bsz, t, A_WIDTH), BF16),
        scratch_shapes=[pltpu.VMEM((nblk, dh), F32),
                        pltpu.VMEM((t, dh), BF16),
                        pltpu.VMEM((t, dh), BF16)],
        compiler_params=_params("parallel", "parallel"),
        name="moba_prompt",
    )(zm, zk, zv, zm, bias_p)


def _stack_heads(x):
    return jnp.concatenate([x[:, h * A_HEAD_DIM:(h + 1) * A_HEAD_DIM] for h in range(A_HEADS)], axis=0)


def _moba_sample_kernel(pt_ref, q_ref, kn_ref, vn_ref, ga_ref, bias_ref, *rest,
                        nb, bps, s_len, scale):
    del pt_ref
    npg = 2 * bps
    k_refs = rest[:npg]
    v_refs = rest[npg:2 * npg]
    o_ref, qm_ref, gate_ref, m_ref, l_ref, oblk_ref = rest[2 * npg:]
    n = pl.program_id(1)
    rows = A_HEADS * s_len
    s_shift = s_len.bit_length() - 1
    far_w = PAGE_SIZE * A_HEADS

    @pl.when(n == 0)
    def _():
        qm_ref[...] = _stack_heads(q_ref[...])
        gate_ref[...] = jnp.full_like(gate_ref, NEG)
        m_ref[...] = jnp.full_like(m_ref, NEG)
        l_ref[...] = jnp.zeros_like(l_ref)

    qb = qm_ref[...].astype(BF16)
    row = lax.broadcasted_iota(jnp.int32, (rows, far_w), 0)
    lane = lax.broadcasted_iota(jnp.int32, (rows, far_w), 1)
    useful = (lane & (A_HEADS - 1)) == (row >> s_shift)
    lane_b = lax.broadcasted_iota(jnp.int32, (rows, LANES), 1)
    last = n == pl.num_programs(1) - 1
    far_bias = bias_ref[:, 0:1]

    def pages(ref):
        return ref[...].reshape(far_w, A_HEAD_DIM).astype(BF16)

    gate_new = gate_ref[...]
    m_new = m_ref[...]
    l_new = l_ref[...]
    for c in range(bps):
        blk = n * bps + c
        raw0 = _nt(qb, pages(k_refs[2 * c]))
        raw1 = _nt(qb, pages(k_refs[2 * c + 1]))
        gsum = jnp.sum(jnp.where(useful, raw0 + raw1, 0.0), axis=1, keepdims=True)
        if c == bps - 1:
            last_v = (jnp.zeros((rows, far_w), jnp.int32) + blk) == nb - 1
            b0 = jnp.where(last_v, bias_ref[:, 0:far_w], far_bias)
            b1 = jnp.where(last_v, bias_ref[:, far_w:2 * far_w], far_bias)
        else:
            b0 = b1 = far_bias
        s0 = jnp.where(useful, raw0 * scale + b0, NEG)
        s1 = jnp.where(useful, raw1 * scale + b1, NEG)
        mloc = jnp.max(jnp.maximum(s0, s1), axis=1, keepdims=True)
        p0 = jnp.exp(s0 - mloc)
        p1 = jnp.exp(s1 - mloc)
        lloc = jnp.sum(p0 + p1, axis=1, keepdims=True)
        oblk_ref[blk] = (jnp.dot(p0.astype(BF16), pages(v_refs[2 * c]), preferred_element_type=F32)
                         + jnp.dot(p1.astype(BF16), pages(v_refs[2 * c + 1]), preferred_element_type=F32))
        here = lane_b == blk
        gate_new = jnp.where(here, gsum * (1.0 / MOBA_BLOCK), gate_new)
        m_new = jnp.where(here, mloc, m_new)
        l_new = jnp.where(here, lloc, l_new)
    gate_ref[...] = gate_new
    m_ref[...] = m_new
    l_ref[...] = l_new

    @pl.when(last)
    def _():
        row_b = lax.broadcasted_iota(jnp.int32, (rows, LANES), 0)
        lane_f = lane_b.astype(F32)
        g = jnp.where(lane_b < nb, gate_ref[...], -jnp.inf)
        sel = jnp.zeros((rows, LANES), jnp.bool_)
        for _r in range(min(MOBA_TOPK, nb)):
            mx = jnp.max(g, axis=1, keepdims=True)
            idx = jnp.min(jnp.where(g == mx, lane_f, float(LANES)), axis=1, keepdims=True)
            hit = lane_f == idx
            sel = sel | hit
            g = jnp.where(hit, -jnp.inf, g)
        pad = jnp.zeros((LANES - rows, A_HEAD_DIM), F32)
        kn = jnp.concatenate([_stack_heads(kn_ref[...]), pad], axis=0).astype(BF16)
        vn = jnp.concatenate([_stack_heads(vn_ref[...]), pad], axis=0).astype(BF16)
        so = _nt(qb, kn) * scale + bias_ref[:, 2 * far_w:2 * far_w + LANES]
        ok = ((lane_b < rows) & ((lane_b >> s_shift) == (row_b >> s_shift))
              & ((lane_b & (s_len - 1)) <= (row_b & (s_len - 1))))
        so = jnp.where(ok, so, NEG)
        mm = m_ref[...]
        mtot = jnp.maximum(jnp.max(jnp.where(sel, mm, NEG), axis=1, keepdims=True),
                           jnp.max(so, axis=1, keepdims=True))
        w = jnp.where(sel, jnp.exp(mm - mtot), 0.0)
        po = jnp.exp(so - mtot)
        lsum = jnp.sum(w * l_ref[...], axis=1, keepdims=True) + jnp.sum(po, axis=1, keepdims=True)
        acc = jnp.dot(po.astype(BF16), vn, preferred_element_type=F32)
        for nn in range(nb):
            acc = acc + w[:, nn:nn + 1] * oblk_ref[nn]
        out = acc / lsum
        res = jnp.concatenate([out[h * s_len:(h + 1) * s_len, :] for h in range(A_HEADS)], axis=1)
        o_ref[...] = (res * _silu(ga_ref[...])).astype(o_ref.dtype)


def _moba_sample(zm, zk, zv, bias_s, cache_k, cache_v, layer, page_table, bsz, s_len, bps=8):
    n_pages = page_table.shape[1]
    ppb = MOBA_BLOCK // PAGE_SIZE
    assert ppb == 2 and n_pages % ppb == 0 and s_len == SUBLANES and A_HEADS == SUBLANES
    nb = n_pages // ppb
    assert nb <= LANES and nb % bps == 0
    rows = A_HEADS * s_len
    dh = A_HEAD_DIM
    kern = functools.partial(_moba_sample_kernel, nb=nb, bps=bps, s_len=s_len, scale=dh ** -0.5)
    tok = lambda g: pl.BlockSpec((None, None, s_len, A_WIDTH), lambda b, n, pt: (g, b, 0, 0))
    page = lambda o: pl.BlockSpec((None, None, PAGE_SIZE, A_HEADS, dh),
                                  lambda b, n, pt: (layer, pt[b, ppb * bps * n + o], 0, 0, 0))
    pages = [page(o) for o in range(ppb * bps)]
    grid_spec = pltpu.PrefetchScalarGridSpec(
        num_scalar_prefetch=1,
        grid=(bsz, nb // bps),
        in_specs=[tok(0), tok(0), tok(0), tok(1),
                  pl.BlockSpec(bias_s.shape, lambda b, n, pt: (0, 0))] + pages + pages,
        out_specs=pl.BlockSpec((None, s_len, A_WIDTH), lambda b, n, pt: (b, 0, 0)),
        scratch_shapes=[pltpu.VMEM((rows, dh), F32),
                        pltpu.VMEM((rows, LANES), F32),
                        pltpu.VMEM((rows, LANES), F32),
                        pltpu.VMEM((rows, LANES), F32),
                        pltpu.VMEM((nb, rows, dh), F32)])
    return pl.pallas_call(
        kern,
        grid_spec=grid_spec,
        out_shape=jax.ShapeDtypeStruct((bsz, s_len, A_WIDTH), BF16),
        compiler_params=_params("parallel", "arbitrary"),
        name="moba_sample",
    )(page_table, zm, zk, zv, zm, bias_s, *([cache_k] * (ppb * bps)), *([cache_v] * (ppb * bps)))


def _mlstm_kernel(q_ref, k_ref, v_ref, og_ref, gm_ref, g_ref, gt_ref, ng_ref, c0_ref, n0_ref, m0_ref,
                  h_ref, c_out, n_out, m_out, c_scr, n_scr, m_scr, *, L, Lp):
    c = pl.program_id(1)
    dk = B_HEAD_DIM

    @pl.when(c == 0)
    def _():
        c_scr[...] = c0_ref[...]
        n_scr[...] = n0_ref[...]
        m_scr[...] = m0_ref[...]

    def padded(x):
        if L == Lp:
            return x
        return jnp.concatenate([x, jnp.zeros((Lp - L, x.shape[1]), x.dtype)], axis=0)

    g = padded(g_ref[...])
    gt = gt_ref[...]
    r = lax.broadcasted_iota(jnp.int32, (Lp, Lp), 0)
    cc = lax.broadcasted_iota(jnp.int32, (Lp, Lp), 1)
    tri = r >= cc
    tri_t = r <= cc
    rv = lax.broadcasted_iota(jnp.int32, (Lp, 1), 0) < L
    cv = lax.broadcasted_iota(jnp.int32, (1, Lp), 1) < L

    for hh in range(B_HEADS):
        cols = slice(hh * dk, (hh + 1) * dk)
        q = padded(q_ref[:, cols])
        k = padded(k_ref[:, cols]) * (dk ** -0.5)
        v = padded(v_ref[:, cols])
        li_col = g[:, hh:hh + 1]
        li_row = gt[hh:hh + 1, :]
        lf_col = jax.nn.log_sigmoid(g[:, hh + B_HEADS:hh + B_HEADS + 1])
        lf_row = jax.nn.log_sigmoid(gt[hh + B_HEADS:hh + B_HEADS + 1, :])
        if L != Lp:
            li_col = jnp.where(rv, li_col, NEG)
            lf_col = jnp.where(rv, lf_col, 0.0)
            li_row = jnp.where(cv, li_row, NEG)
            lf_row = jnp.where(cv, lf_row, 0.0)

        b_col = jnp.sum(jnp.where(tri, lf_row, 0.0), axis=1, keepdims=True)
        b_row = jnp.sum(jnp.where(tri_t, lf_col, 0.0), axis=0, keepdims=True)
        m_prev = m_scr[hh]
        log_d = jnp.where(tri, b_col - b_row + li_row, NEG)
        m_inter = b_col + m_prev
        m_t = jnp.maximum(m_inter, jnp.max(log_d, axis=1, keepdims=True))
        qb = q.astype(BF16)
        kb = k.astype(BF16)
        vb = v.astype(BF16)
        s = _nt(qb, kb) * jnp.exp(log_d - m_t)
        w_inter = jnp.exp(m_inter - m_t)
        c_old = c_scr[hh]
        n_old = n_scr[hh]
        num = (jnp.dot(s.astype(BF16), vb, preferred_element_type=F32)
               + w_inter * jnp.dot(qb, c_old.astype(BF16), preferred_element_type=F32))
        den = jnp.sum(s, axis=1, keepdims=True) + w_inter * jnp.sum(q * n_old, axis=1, keepdims=True)
        h = num / jnp.maximum(jnp.abs(den), jnp.exp(-m_t))

        b_last = b_row[:, L - 1:L]
        m_new = jnp.maximum(b_last + m_prev, jnp.max(b_last - b_row + li_row, axis=1, keepdims=True))
        decay = jnp.exp(b_last + m_prev - m_new)
        wk = jnp.exp(b_last - b_col + li_col - m_new)
        kw = k * wk
        c_scr[hh] = decay * c_old + lax.dot_general(kw.astype(BF16), vb, (((0,), (0,)), ((), ())),
                                                    preferred_element_type=F32)
        n_scr[hh] = decay * n_old + jnp.sum(kw, axis=0, keepdims=True)
        m_scr[hh] = m_new

        hv = h[:L]
        mu = jnp.mean(hv, axis=1, keepdims=True)
        var = jnp.mean(jnp.square(hv - mu), axis=1, keepdims=True)
        hn = (hv - mu) * lax.rsqrt(var + LN_EPS) * ng_ref[hh]
        h_ref[:, cols] = (hn * jax.nn.sigmoid(og_ref[:, cols]) * _silu(gm_ref[:, cols])).astype(h_ref.dtype)

    @pl.when(c == pl.num_programs(1) - 1)
    def _():
        c_out[...] = c_scr[...]
        n_out[...] = n_scr[...]
        m_out[...] = m_scr[...]


def _mlstm(zm, gates, norm_g, c0, n0, m0, bsz, t, L):
    Lp = max(L, LANES)
    nc = t // L
    dh = B_HEAD_DIM
    g3 = gates.reshape(bsz, t, LANES)
    gt = jnp.swapaxes(gates[:, :2 * B_HEADS].reshape(bsz, nc, L, 2 * B_HEADS), 2, 3)
    if Lp != L:
        gt = jnp.pad(gt, ((0, 0), (0, 0), (0, 0), (0, Lp - L)))
    ng = norm_g.reshape(B_HEADS, 1, dh).astype(F32)
    n0 = n0.reshape(bsz, B_HEADS, 1, dh)
    m0 = m0.reshape(bsz, B_HEADS, 1, 1)
    kern = functools.partial(_mlstm_kernel, L=L, Lp=Lp)
    zspec = lambda gidx: pl.BlockSpec((None, None, L, B_WIDTH), lambda b, c: (gidx, b, c, 0))
    st = lambda r, cdim: pl.BlockSpec((None, B_HEADS, r, cdim), lambda b, c: (b, 0, 0, 0))
    h, c1, n1, m1 = pl.pallas_call(
        kern,
        grid=(bsz, nc),
        in_specs=[zspec(2), zspec(3), zspec(4), zspec(5), zspec(6),
                  pl.BlockSpec((None, L, LANES), lambda b, c: (b, c, 0)),
                  pl.BlockSpec((None, None, 2 * B_HEADS, Lp), lambda b, c: (b, c, 0, 0)),
                  pl.BlockSpec((B_HEADS, 1, dh), lambda b, c: (0, 0, 0)),
                  st(dh, dh), st(1, dh), st(1, 1)],
        out_specs=[pl.BlockSpec((None, L, B_WIDTH), lambda b, c: (b, c, 0)),
                   st(dh, dh), st(1, dh), st(1, 1)],
        out_shape=[jax.ShapeDtypeStruct((bsz, t, B_WIDTH), BF16),
                   jax.ShapeDtypeStruct((bsz, B_HEADS, dh, dh), F32),
                   jax.ShapeDtypeStruct((bsz, B_HEADS, 1, dh), F32),
                   jax.ShapeDtypeStruct((bsz, B_HEADS, 1, 1), F32)],
        scratch_shapes=[pltpu.VMEM((B_HEADS, dh, dh), F32), pltpu.VMEM((B_HEADS, 1, dh), F32),
                        pltpu.VMEM((B_HEADS, 1, 1), F32)],
        compiler_params=_params("parallel", "arbitrary"),
        name="mlstm",
    )(zm, zm, zm, zm, zm, g3, gt, ng, c0, n0, m0)
    return h, c1, n1.reshape(bsz, B_HEADS, dh), m1.reshape(bsz, B_HEADS)


def _outln_kernel(*refs, n_parts):
    mix = refs[:n_parts]
    ws = refs[n_parts:2 * n_parts]
    x_ref, g_ref, b_ref, o_ref, ob_ref = refs[2 * n_parts:]
    tm = x_ref.shape[0]
    sub = min(tm, LANES)
    for r in range(tm // sub):
        rows = pl.ds(r * sub, sub)
        y = jnp.dot(mix[0][rows, :], ws[0][...], preferred_element_type=F32)
        for p in range(1, n_parts):
            y = y + jnp.dot(mix[p][rows, :], ws[p][...], preferred_element_type=F32)
        z = ALPHA * x_ref[rows, :] + y
        mu = jnp.mean(z, axis=1, keepdims=True)
        var = jnp.mean(jnp.square(z - mu), axis=1, keepdims=True)
        xn = (z - mu) * lax.rsqrt(var + LN_EPS) * g_ref[...] + b_ref[...]
        o_ref[rows, :] = xn
        ob_ref[rows, :] = xn.astype(BF16)


def _out_ln(mix_parts, w_parts, x, ln_g, ln_b, tm):
    m, d = x.shape
    n_parts = len(mix_parts)
    in_specs = ([pl.BlockSpec((tm, mp.shape[1]), lambda i: (i, 0)) for mp in mix_parts]
                + [pl.BlockSpec(wp.shape, lambda i: (0, 0)) for wp in w_parts]
                + [pl.BlockSpec((tm, d), lambda i: (i, 0)),
                   pl.BlockSpec((1, d), lambda i: (0, 0)),
                   pl.BlockSpec((1, d), lambda i: (0, 0))])
    return pl.pallas_call(
        functools.partial(_outln_kernel, n_parts=n_parts),
        grid=(m // tm,),
        in_specs=in_specs,
        out_specs=[pl.BlockSpec((tm, d), lambda i: (i, 0)), pl.BlockSpec((tm, d), lambda i: (i, 0))],
        out_shape=[jax.ShapeDtypeStruct((m, d), F32), jax.ShapeDtypeStruct((m, d), BF16)],
        compiler_params=_params("parallel"),
        name="outproj_ln",
    )(*mix_parts, *w_parts, x, ln_g.reshape(1, d).astype(F32), ln_b.reshape(1, d).astype(F32))


def _odd_kernel(*refs, tm, carry_mode, seq_tiles, seq_rows):
    if carry_mode:
        x_ref, wb, wc, wx, wg, cw_ref, mix_ref, tail_ref, wbf, carry = refs
    else:
        x_ref, wb, wc, wx, wg, cw_ref, plast_ref, pprev_ref, mix_ref, tail_ref, wbf = refs

    @pl.when(pl.program_id(1) == 0)
    def _():
        for gidx, wref in enumerate((wb, wc, wx, wg)):
            wbf[gidx] = wref[...].astype(BF16)

    x = x_ref[...]
    zb = jnp.dot(x, wbf[0], preferred_element_type=F32)
    zc = jnp.dot(x, wbf[1], preferred_element_type=F32)
    zx = jnp.dot(x, wbf[2], preferred_element_type=F32)
    zg = jnp.dot(x, wbf[3], preferred_element_type=F32)
    u = zc * zx
    rows = lax.broadcasted_iota(jnp.int32, u.shape, 0)
    r1 = pltpu.roll(u, 1, 0)
    r2 = pltpu.roll(u, 2, 0)
    if carry_mode:
        @pl.when(pl.program_id(1) % seq_tiles == 0)
        def _():
            carry[...] = jnp.zeros_like(carry)
        last = carry[SUBLANES - 1:SUBLANES, :]
        prev = carry[SUBLANES - 2:SUBLANES - 1, :]
        rm = rows
    else:
        last = plast_ref[...]
        prev = pprev_ref[...]
        rm = rows & (seq_rows - 1)
    u1 = jnp.where(rm == 0, last, r1)
    u2 = jnp.where(rm == 0, prev, jnp.where(rm == 1, last, r2))
    cw = cw_ref[...]
    y = cw[0:1, :] * u2 + cw[1:2, :] * u1 + cw[2:3, :] * u
    mix_ref[...] = (zb * y * _silu(zg)).astype(mix_ref.dtype)
    if carry_mode:
        tail = u[tm - SUBLANES:, :]
        carry[...] = tail
        tail_ref[...] = tail
    else:
        tail_ref[...] = u


def _odd_mix(x_bf, w_all, o, conv_w, tm, tn, seq_len, state=None):
    m, k = x_bf.shape
    nn = C_WIDTH // tn
    carry_mode = state is None
    wspec = lambda gidx: pl.BlockSpec((None, k, tn), lambda j, i: (o, 0, gidx * nn + j))
    in_specs = [pl.BlockSpec((tm, k), lambda j, i: (i, 0)),
                wspec(0), wspec(1), wspec(2), wspec(3),
                pl.BlockSpec((CONV_K, tn), lambda j, i: (0, j))]
    args = [x_bf, w_all, w_all, w_all, w_all, conv_w.astype(F32)]
    scratch = [pltpu.VMEM((4, k, tn), BF16)]
    if carry_mode:
        assert seq_len % tm == 0
        tail_shape = jax.ShapeDtypeStruct((m // tm, SUBLANES, C_WIDTH), F32)
        tail_spec = pl.BlockSpec((None, SUBLANES, tn), lambda j, i: (i, 0, j))
        scratch.append(pltpu.VMEM((SUBLANES, tn), F32))
    else:
        assert tm % seq_len == 0 and seq_len & (seq_len - 1) == 0
        in_specs += [pl.BlockSpec((tm, tn), lambda j, i: (i, j))] * 2
        args += list(state)
        tail_shape = jax.ShapeDtypeStruct((m, C_WIDTH), F32)
        tail_spec = pl.BlockSpec((tm, tn), lambda j, i: (i, j))
    kern = functools.partial(_odd_kernel, tm=tm, carry_mode=carry_mode,
                             seq_tiles=max(seq_len // tm, 1), seq_rows=seq_len)
    return pl.pallas_call(
        kern,
        grid=(nn, m // tm),
        in_specs=in_specs,
        out_specs=[pl.BlockSpec((tm, tn), lambda j, i: (i, j)), tail_spec],
        out_shape=[jax.ShapeDtypeStruct((m, C_WIDTH), BF16), tail_shape],
        scratch_shapes=scratch,
        compiler_params=_params("parallel", "arbitrary"),
        name="odd_mix",
    )(*args)


def _row_tiles(m):
    return min(m, 1024), min(m, 512)


def _even_layer(x, x_bf, bsz, t, wt_all, e, gbias, w_out_a, w_out_b, norm_g, ln_g, ln_b,
                bias_tile, past):
    tm, tm_out = _row_tiles(bsz * t)
    (zm,) = _inproj_even(x_bf, wt_all, e, (0, 2, N_GROUPS_EVEN - 2), tm)
    zk, ka = _inproj_even(x_bf, wt_all, e, (1, 0, 1), tm, heads_out=True)
    zv, va = _inproj_even(x_bf, wt_all, e, (2, 0, 1), tm, heads_out=True)
    zm = zm.reshape(N_GROUPS_EVEN - 2, bsz, t, 1024)
    zk = zk.reshape(1, bsz, t, 1024)
    zv = zv.reshape(1, bsz, t, 1024)
    gates = _gates(x_bf, wt_all, e, gbias, tm)
    if past is None:
        attn = _moba_prompt(zm, zk, zv, bias_tile, bsz, t)
        c0 = jnp.zeros((bsz, B_HEADS, B_HEAD_DIM, B_HEAD_DIM), F32)
        n0 = jnp.zeros((bsz, B_HEADS, B_HEAD_DIM), F32)
        m0 = jnp.zeros((bsz, B_HEADS), F32)
        L = math.gcd(t, MLSTM_PROMPT_CHUNK)
    else:
        cache_k, cache_v, page_table, c0, n0, m0 = past
        attn = _moba_sample(zm, zk, zv, bias_tile, cache_k, cache_v, e, page_table, bsz, t)
        L = math.gcd(t, MLSTM_CHUNK)
    hm, c1, n1, m1 = _mlstm(zm, gates, norm_g, c0, n0, m0, bsz, t, L)
    xn, xn_bf = _out_ln([attn.reshape(bsz * t, A_WIDTH), hm.reshape(bsz * t, B_WIDTH)],
                        [w_out_a, w_out_b], x, ln_g, ln_b, tm_out)
    ka = ka.reshape(bsz, t, A_HEADS, A_HEAD_DIM)
    va = va.reshape(bsz, t, A_HEADS, A_HEAD_DIM)
    return xn, xn_bf, (ka, va, c1, n1, m1)


def _odd_layer(x, x_bf, bsz, t, w_in_all, o, w_out, conv_w, ln_g, ln_b, buf):
    tm, tm_out = _row_tiles(bsz * t)
    if buf is None:
        tm = math.gcd(2 * tm, t)
        mix, tail = _odd_mix(x_bf, w_in_all, o, conv_w, tm, 256, t)
        seq_tiles = t // tm
        new_buf = tail.reshape(bsz, seq_tiles, SUBLANES, C_WIDTH)[:, -1, SUBLANES - (CONV_K - 1):, :]
    else:
        plast = jnp.repeat(buf[:, 1, :], t, axis=0)
        pprev = jnp.repeat(buf[:, 0, :], t, axis=0)
        mix, u = _odd_mix(x_bf, w_in_all, o, conv_w, tm, 256, t, state=(plast, pprev))
        new_buf = u.reshape(bsz, t, C_WIDTH)[:, t - (CONV_K - 1):, :]
    xn, xn_bf = _out_ln([mix], [w_out], x, ln_g, ln_b, tm_out)
    return xn, xn_bf, new_buf


def kernel(x_prompt, x_sample, cache_k, cache_v, page_table, state_C, state_n, state_m, state_conv,
           w_in_even, w_out_even, mlstm_gate_bias, mlstm_norm_g, rel_bias,
           w_in_odd, w_out_odd, conv_w, ln_g, ln_b):
    bp, tp, d = x_prompt.shape
    bs, ts, _ = x_sample.shape
    bias_p = _bias_tiles(rel_bias, _prompt_bias_idx())
    bias_s = _bias_tiles(rel_bias, _sample_bias_idx(ts)).reshape(A_HEADS * ts, -1)

    xp = x_prompt.reshape(bp * tp, d)
    xs = x_sample.reshape(bs * ts, d)
    xp_bf = xp.astype(BF16)
    xs_bf = xs.astype(BF16)
    wt_even = jnp.swapaxes(w_in_even, 1, 2)

    pk, pv, pc, pn, pm, pb = [], [], [], [], [], []
    sk, sv, sc, sn, sm, sb = [], [], [], [], [], []
    for layer in range(DEPTH):
        if layer % 2 == 0:
            e = layer // 2
            gbias = jnp.pad(mlstm_gate_bias[e], (0, LANES - 2 * B_HEADS)).reshape(1, LANES).astype(F32)
            w_out_a = w_out_even[e, :A_WIDTH].astype(BF16)
            w_out_b = w_out_even[e, A_WIDTH:].astype(BF16)
            xp, xp_bf, (k1, v1, c1, n1, m1) = _even_layer(
                xp, xp_bf, bp, tp, wt_even, e, gbias, w_out_a, w_out_b, mlstm_norm_g[e],
                ln_g[layer], ln_b[layer], bias_p, None)
            xs, xs_bf, (k2, v2, c2, n2, m2) = _even_layer(
                xs, xs_bf, bs, ts, wt_even, e, gbias, w_out_a, w_out_b, mlstm_norm_g[e],
                ln_g[layer], ln_b[layer], bias_s,
                (cache_k, cache_v, page_table, state_C[e], state_n[e], state_m[e]))
            pk.append(k1); pv.append(v1); pc.append(c1); pn.append(n1); pm.append(m1)
            sk.append(k2); sv.append(v2); sc.append(c2); sn.append(n2); sm.append(m2)
        else:
            o = layer // 2
            w_out = w_out_odd[o].astype(BF16)
            xp, xp_bf, b1 = _odd_layer(xp, xp_bf, bp, tp, w_in_odd, o, w_out, conv_w[o],
                                       ln_g[layer], ln_b[layer], None)
            xs, xs_bf, b2 = _odd_layer(xs, xs_bf, bs, ts, w_in_odd, o, w_out, conv_w[o],
                                       ln_g[layer], ln_b[layer], state_conv[o])
            pb.append(b1); sb.append(b2)
    return (xp.reshape(bp, tp, d), xs.reshape(bs, ts, d),
            jnp.stack(pk), jnp.stack(pv), jnp.stack(pc), jnp.stack(pn), jnp.stack(pm), jnp.stack(pb),
            jnp.stack(sk), jnp.stack(sv), jnp.stack(sc), jnp.stack(sn), jnp.stack(sm), jnp.stack(sb))
```

```python
import functools
import math

import numpy as np
import jax
import jax.numpy as jnp
from jax import lax
from jax.experimental import pallas as pl
from jax.experimental.pallas import tpu as pltpu

F32 = jnp.float32
BF16 = jnp.bfloat16

D_MODEL = 2048
DEPTH = 4
PAGE_SIZE = 128
A_HEADS = 8
A_HEAD_DIM = 128
A_WIDTH = A_HEADS * A_HEAD_DIM
MOBA_BLOCK = 256
MOBA_TOPK = 3
REL_BUCKETS = 32
REL_MAX_DIST = 128
B_HEADS = 4
B_HEAD_DIM = 256
B_WIDTH = B_HEADS * B_HEAD_DIM
MLSTM_CHUNK = 64
MLSTM_PROMPT_CHUNK = 256
C_WIDTH = D_MODEL
CONV_K = 3
N_GROUPS_EVEN = 9
EVEN_MAIN = N_GROUPS_EVEN * 1024
ALPHA = (2.0 * DEPTH) ** 0.25
LN_EPS = 1e-5

LANES = 128
SUBLANES = 8
VMEM_LIMIT = 56 * 1024 * 1024

NEG = -1e30


def _params(*sem):
    return pltpu.CompilerParams(dimension_semantics=sem, vmem_limit_bytes=VMEM_LIMIT)


def _nt(a, b):
    return lax.dot_general(a, b, (((1,), (1,)), ((), ())), preferred_element_type=F32)


def _silu(x):
    return x * jax.nn.sigmoid(x)


def _bucket_np(d):
    d = np.asarray(d, np.int64)
    n = np.maximum(d, 0)
    max_exact = REL_BUCKETS // 2
    nf = np.maximum(n, 1).astype(np.float64)
    large = max_exact + (np.log(nf / max_exact) / math.log(REL_MAX_DIST / max_exact)
                         * (REL_BUCKETS - max_exact)).astype(np.int64)
    large = np.minimum(large, REL_BUCKETS - 1)
    out = np.where(n < max_exact, n, large)
    return np.where(d < 0, -1, out).astype(np.int32)


def _bias_kernel(tab_ref, idx_ref, o_ref):
    h = pl.program_id(0)
    idx = idx_ref[...]
    out = jnp.full(idx.shape, NEG, F32)
    for b in range(REL_BUCKETS):
        out = jnp.where(idx == b, tab_ref[h, b], out)
    o_ref[...] = out


def _bias_tiles(rel_bias, idx):
    r, c = idx.shape
    return pl.pallas_call(
        _bias_kernel,
        grid=(A_HEADS,),
        in_specs=[pl.BlockSpec(memory_space=pltpu.SMEM),
                  pl.BlockSpec((r, c), lambda h: (0, 0))],
        out_specs=pl.BlockSpec((None, r, c), lambda h: (h, 0, 0)),
        out_shape=jax.ShapeDtypeStruct((A_HEADS, r, c), F32),
        compiler_params=_params("parallel"),
        name="bias_tiles",
    )(rel_bias.astype(F32), jnp.asarray(idx))


def _prompt_bias_idx():
    qi = np.arange(MOBA_BLOCK)[:, None]
    kj = np.arange(2 * MOBA_BLOCK)[None, :]
    return _bucket_np(qi + MOBA_BLOCK - kj)


def _sample_bias_idx(s_len):
    s = np.arange(s_len)[:, None]
    lane = np.arange(2 * PAGE_SIZE * A_HEADS)[None, :]
    off = lane // A_HEADS
    far = _bucket_np(MOBA_BLOCK + s - off)
    lo = np.arange(LANES)[None, :]
    own = _bucket_np(np.where(lo < A_HEADS * s_len, s - (lo % s_len), -1))
    return np.concatenate([far, own], axis=1)


def _inproj_kernel(x_ref, w_ref, *rest, heads_out):
    if heads_out:
        o_ref, oh_ref, wb_ref = rest
    else:
        o_ref, wb_ref = rest

    @pl.when(pl.program_id(1) == 0)
    def _():
        wb_ref[...] = w_ref[...].astype(BF16)

    y = _nt(x_ref[...], wb_ref[...])
    o_ref[...] = y.astype(o_ref.dtype)
    if heads_out:
        oh_ref[...] = y.reshape(oh_ref.shape)


def _inproj_even(x_bf, wt_all, e, groups, tm, heads_out=False):
    m, k = x_bf.shape
    gw = 1024
    first, skip, n_groups = groups
    gmap = lambda j: first + j + skip * jnp.minimum(j, 1)
    out_specs = [pl.BlockSpec((None, tm, gw), lambda j, i: (j, i, 0))]
    out_shape = [jax.ShapeDtypeStruct((n_groups, m, gw), BF16)]
    if heads_out:
        assert n_groups == 1
        out_specs.append(pl.BlockSpec((tm, A_HEADS, A_HEAD_DIM), lambda j, i: (i, 0, 0)))
        out_shape.append(jax.ShapeDtypeStruct((m, A_HEADS, A_HEAD_DIM), F32))
    return pl.pallas_call(
        functools.partial(_inproj_kernel, heads_out=heads_out),
        grid=(n_groups, m // tm),
        in_specs=[pl.BlockSpec((tm, k), lambda j, i: (i, 0)),
                  pl.BlockSpec((None, gw, k), lambda j, i: (e, gmap(j), 0))],
        out_specs=out_specs,
        out_shape=out_shape,
        scratch_shapes=[pltpu.VMEM((gw, k), BF16)],
        compiler_params=_params("parallel", "arbitrary"),
        name="inproj_even",
    )(x_bf, wt_all)


def _gates_kernel(x_ref, w_ref, b_ref, o_ref):
    w = w_ref[...]
    wp = jnp.concatenate([w, jnp.zeros((LANES - w.shape[0], w.shape[1]), F32)], axis=0)
    o_ref[...] = _nt(x_ref[...], wp.astype(BF16)) + b_ref[...]


def _gates(x_bf, wt_all, e, gbias, tm):
    m, k = x_bf.shape
    ng = 2 * B_HEADS
    assert ng == SUBLANES and EVEN_MAIN % ng == 0
    return pl.pallas_call(
        _gates_kernel,
        grid=(m // tm,),
        in_specs=[pl.BlockSpec((tm, k), lambda i: (i, 0)),
                  pl.BlockSpec((None, ng, k), lambda i: (e, EVEN_MAIN // ng, 0)),
                  pl.BlockSpec((1, LANES), lambda i: (0, 0))],
        out_specs=pl.BlockSpec((tm, LANES), lambda i: (i, 0)),
        out_shape=jax.ShapeDtypeStruct((m, LANES), F32),
        compiler_params=_params("parallel"),
        name="mlstm_gates",
    )(x_bf, wt_all, gbias)


def _moba_prompt_kernel(q_ref, k_ref, v_ref, g_ref, bias_ref, o_ref, kmean_ref, negm_ref,
                        *, nblk, scale):
    blk = MOBA_BLOCK
    for j in range(nblk):
        kmean_ref[j:j + 1, :] = jnp.mean(k_ref[j * blk:(j + 1) * blk, :].astype(F32), axis=0, keepdims=True)
    far_bias = bias_ref[0:1, 0:1]
    eye = (lax.broadcasted_iota(jnp.int32, (blk, blk), 0)
           == lax.broadcasted_iota(jnp.int32, (blk, blk), 1)).astype(BF16)
    rowi = lax.broadcasted_iota(jnp.int32, (nblk, blk), 0)

    i0 = MOBA_TOPK + 1
    if i0 < nblk:
        nq = (nblk - i0) * blk
        gt = lax.dot_general(kmean_ref[...], q_ref[i0 * blk:, :].astype(F32), (((1,), (1,)), ((), ())),
                             precision=lax.Precision.HIGHEST, preferred_element_type=F32)
        rowq = lax.broadcasted_iota(jnp.int32, (nblk, nq), 0)
        own = i0 + (lax.broadcasted_iota(jnp.int32, (nblk, nq), 1) >> (blk.bit_length() - 1))
        valid = rowq < own
        sel_t = jnp.zeros((nblk, nq), F32)
        for j in range(nblk - 1):
            gj = gt[j:j + 1, :]
            beats = ((gt > gj) | ((gt == gj) & (rowq < j))) & valid
            rank = jnp.sum(beats.astype(F32), axis=0, keepdims=True)
            sel_t = jnp.where(rowq == j, (rank < MOBA_TOPK).astype(F32), sel_t)
        sel_t = jnp.concatenate([sel_t, jnp.zeros((LANES - nblk, nq), F32)], axis=0).astype(BF16)
        for i in range(i0, nblk):
            cs = (i - i0) * blk
            sel = _nt(eye, sel_t[:, cs:cs + blk])
            negm_ref[i - i0] = (sel - 1.0) * (-NEG)

    for i in range(nblk):
        q = q_ref[i * blk:(i + 1) * blk, :].astype(F32)
        negm = negm_ref[i - MOBA_TOPK - 1] if i > MOBA_TOPK else None
        w = (i + 1) * blk
        s = _nt((q * scale).astype(BF16), k_ref[0:w, :])
        pieces = []
        for j in range(i + 1):
            sj = s[:, j * blk:(j + 1) * blk]
            if j == i:
                sj = sj + bias_ref[:, blk:2 * blk]
            elif j == i - 1:
                sj = sj + bias_ref[:, 0:blk]
                if negm is not None:
                    sj = sj + negm[:, j:j + 1]
            else:
                sj = sj + (far_bias if negm is None else far_bias + negm[:, j:j + 1])
            pieces.append(sj)
        m = jnp.max(functools.reduce(jnp.maximum, pieces), axis=1, keepdims=True)
        ps = [jnp.exp(sj - m) for sj in pieces]
        l = jnp.sum(functools.reduce(jnp.add, ps), axis=1, keepdims=True)
        pcat = ps[0].astype(BF16) if i == 0 else jnp.concatenate([p.astype(BF16) for p in ps], axis=1)
        acc = jnp.dot(pcat, v_ref[0:w, :], preferred_element_type=F32)
        g = g_ref[i * blk:(i + 1) * blk, :].astype(F32)
        o_ref[i * blk:(i + 1) * blk, :] = (acc / l * _silu(g)).astype(o_ref.dtype)


def _moba_prompt(zm, zk, zv, bias_p, bsz, t):
    blk = MOBA_BLOCK
    nblk = t // blk
    assert nblk <= SUBLANES
    dh = A_HEAD_DIM
    kern = functools.partial(_moba_prompt_kernel, nblk=nblk, scale=dh ** -0.5)
    zspec = lambda g: pl.BlockSpec((None, None, t, dh), lambda b, h: (g, b, 0, h))
    return pl.pallas_call(
        kern,
        grid=(bsz, A_HEADS),
        in_specs=[zspec(0), zspec(0), zspec(0), zspec(1),
                  pl.BlockSpec((None, blk, 2 * blk), lambda b, h: (h, 0, 0))],
        out_specs=pl.BlockSpec((None, t, dh), lambda b, h: (b, 0, h)),
        out_shape=jax.ShapeDtypeStruct((bsz, t, A_WIDTH), BF16),
        scratch_shapes=[pltpu.VMEM((nblk, dh), F32),
                        pltpu.VMEM((max(nblk - MOBA_TOPK - 1, 1), blk, LANES), F32)],
        compiler_params=_params("parallel", "parallel"),
        name="moba_prompt",
    )(zm, zk, zv, zm, bias_p)


def _stack_heads(x):
    return jnp.concatenate([x[:, h * A_HEAD_DIM:(h + 1) * A_HEAD_DIM] for h in range(A_HEADS)], axis=0)


def _moba_sample_kernel(pt_ref, q_ref, kn_ref, vn_ref, ga_ref, bias_ref, *rest,
                        nb, bps, s_len, scale):
    del pt_ref
    npg = 2 * bps
    k_refs = rest[:npg]
    v_refs = rest[npg:2 * npg]
    o_ref, qm_ref, gate_ref, m_ref, l_ref, oblk_ref = rest[2 * npg:]
    n = pl.program_id(1)
    rows = A_HEADS * s_len
    s_shift = s_len.bit_length() - 1
    far_w = PAGE_SIZE * A_HEADS

    @pl.when(n == 0)
    def _():
        qm_ref[...] = _stack_heads(q_ref[...].astype(F32))
        gate_ref[...] = jnp.full_like(gate_ref, NEG)
        m_ref[...] = jnp.full_like(m_ref, NEG)
        l_ref[...] = jnp.zeros_like(l_ref)

    qb = qm_ref[...].astype(BF16)
    row = lax.broadcasted_iota(jnp.int32, (rows, far_w), 0)
    lane = lax.broadcasted_iota(jnp.int32, (rows, far_w), 1)
    useful = (lane & (A_HEADS - 1)) == (row >> s_shift)
    lane_b = lax.broadcasted_iota(jnp.int32, (rows, LANES), 1)
    last = n == pl.num_programs(1) - 1
    far_bias = bias_ref[:, 0:1]

    def pages(ref):
        return ref[...].reshape(far_w, A_HEAD_DIM).astype(BF16)

    gate_new = gate_ref[...]
    m_new = m_ref[...]
    l_new = l_ref[...]
    for c in range(bps):
        blk = n * bps + c
        raw0 = _nt(qb, pages(k_refs[2 * c]))
        raw1 = _nt(qb, pages(k_refs[2 * c + 1]))
        gsum = jnp.sum(jnp.where(useful, raw0 + raw1, 0.0), axis=1, keepdims=True)
        if c == bps - 1:
            last_v = (jnp.zeros((rows, far_w), jnp.int32) + blk) == nb - 1
            b0 = jnp.where(last_v, bias_ref[:, 0:far_w], far_bias)
            b1 = jnp.where(last_v, bias_ref[:, far_w:2 * far_w], far_bias)
        else:
            b0 = b1 = far_bias
        s0 = jnp.where(useful, raw0 * scale + b0, NEG)
        s1 = jnp.where(useful, raw1 * scale + b1, NEG)
        mloc = jnp.max(jnp.maximum(s0, s1), axis=1, keepdims=True)
        p0 = jnp.exp(s0 - mloc)
        p1 = jnp.exp(s1 - mloc)
        lloc = jnp.sum(p0 + p1, axis=1, keepdims=True)
        oblk_ref[blk] = (jnp.dot(p0.astype(BF16), pages(v_refs[2 * c]), preferred_element_type=F32)
                         + jnp.dot(p1.astype(BF16), pages(v_refs[2 * c + 1]), preferred_element_type=F32))
        here = lane_b == blk
        gate_new = jnp.where(here, gsum * (1.0 / MOBA_BLOCK), gate_new)
        m_new = jnp.where(here, mloc, m_new)
        l_new = jnp.where(here, lloc, l_new)
    gate_ref[...] = gate_new
    m_ref[...] = m_new
    l_ref[...] = l_new

    @pl.when(last)
    def _():
        row_b = lax.broadcasted_iota(jnp.int32, (rows, LANES), 0)
        lane_f = lane_b.astype(F32)
        g = jnp.where(lane_b < nb, gate_ref[...], -jnp.inf)
        sel = jnp.zeros((rows, LANES), jnp.bool_)
        for _r in range(min(MOBA_TOPK, nb)):
            mx = jnp.max(g, axis=1, keepdims=True)
            idx = jnp.min(jnp.where(g == mx, lane_f, float(LANES)), axis=1, keepdims=True)
            hit = lane_f == idx
            sel = sel | hit
            g = jnp.where(hit, -jnp.inf, g)
        pad = jnp.zeros((LANES - rows, A_HEAD_DIM), F32)
        kn = jnp.concatenate([_stack_heads(kn_ref[...].astype(F32)), pad], axis=0).astype(BF16)
        vn = jnp.concatenate([_stack_heads(vn_ref[...].astype(F32)), pad], axis=0).astype(BF16)
        so = _nt(qb, kn) * scale + bias_ref[:, 2 * far_w:2 * far_w + LANES]
        ok = ((lane_b < rows) & ((lane_b >> s_shift) == (row_b >> s_shift))
              & ((lane_b & (s_len - 1)) <= (row_b & (s_len - 1))))
        so = jnp.where(ok, so, NEG)
        mm = m_ref[...]
        mtot = jnp.maximum(jnp.max(jnp.where(sel, mm, NEG), axis=1, keepdims=True),
                           jnp.max(so, axis=1, keepdims=True))
        w = jnp.where(sel, jnp.exp(mm - mtot), 0.0)
        po = jnp.exp(so - mtot)
        lsum = jnp.sum(w * l_ref[...], axis=1, keepdims=True) + jnp.sum(po, axis=1, keepdims=True)
        acc = jnp.dot(po.astype(BF16), vn, preferred_element_type=F32)
        for nn in range(nb):
            acc = acc + w[:, nn:nn + 1] * oblk_ref[nn]
        out = acc / lsum
        res = jnp.concatenate([out[h * s_len:(h + 1) * s_len, :] for h in range(A_HEADS)], axis=1)
        o_ref[...] = (res * _silu(ga_ref[...].astype(F32))).astype(o_ref.dtype)


def _moba_sample(zm, zk, zv, bias_s, cache_k, cache_v, layer, page_table, bsz, s_len, bps=8):
    n_pages = page_table.shape[1]
    ppb = MOBA_BLOCK // PAGE_SIZE
    assert ppb == 2 and n_pages % ppb == 0 and s_len == SUBLANES and A_HEADS == SUBLANES
    nb = n_pages // ppb
    assert nb <= LANES and nb % bps == 0
    rows = A_HEADS * s_len
    dh = A_HEAD_DIM
    kern = functools.partial(_moba_sample_kernel, nb=nb, bps=bps, s_len=s_len, scale=dh ** -0.5)
    tok = lambda g: pl.BlockSpec((None, None, s_len, A_WIDTH), lambda b, n, pt: (g, b, 0, 0))
    page = lambda o: pl.BlockSpec((None, None, PAGE_SIZE, A_HEADS, dh),
                                  lambda b, n, pt: (layer, pt[b, ppb * bps * n + o], 0, 0, 0))
    pages = [page(o) for o in range(ppb * bps)]
    grid_spec = pltpu.PrefetchScalarGridSpec(
        num_scalar_prefetch=1,
        grid=(bsz, nb // bps),
        in_specs=[tok(0), tok(0), tok(0), tok(1),
                  pl.BlockSpec(bias_s.shape, lambda b, n, pt: (0, 0))] + pages + pages,
        out_specs=pl.BlockSpec((None, s_len, A_WIDTH), lambda b, n, pt: (b, 0, 0)),
        scratch_shapes=[pltpu.VMEM((rows, dh), F32),
                        pltpu.VMEM((rows, LANES), F32),
                        pltpu.VMEM((rows, LANES), F32),
                        pltpu.VMEM((rows, LANES), F32),
                        pltpu.VMEM((nb, rows, dh), F32)])
    return pl.pallas_call(
        kern,
        grid_spec=grid_spec,
        out_shape=jax.ShapeDtypeStruct((bsz, s_len, A_WIDTH), BF16),
        compiler_params=_params("parallel", "arbitrary"),
        name="moba_sample",
    )(page_table, zm, zk, zv, zm, bias_s, *([cache_k] * (ppb * bps)), *([cache_v] * (ppb * bps)))


def _mlstm_kernel(q_ref, k_ref, v_ref, og_ref, gm_ref, g_ref, gt_ref, ng_ref, c0_ref, n0_ref, m0_ref,
                  h_ref, c_out, n_out, m_out, c_scr, n_scr, m_scr, *, L, Lp):
    c = pl.program_id(1)
    dk = B_HEAD_DIM

    @pl.when(c == 0)
    def _():
        c_scr[...] = c0_ref[...]
        n_scr[...] = n0_ref[...]
        m_scr[...] = m0_ref[...]

    def padded(x):
        if L == Lp:
            return x
        return jnp.concatenate([x, jnp.zeros((Lp - L, x.shape[1]), x.dtype)], axis=0)

    g = padded(g_ref[...])
    gt = gt_ref[...]
    r = lax.broadcasted_iota(jnp.int32, (Lp, Lp), 0)
    cc = lax.broadcasted_iota(jnp.int32, (Lp, Lp), 1)
    tri = r >= cc
    tri_t = r <= cc
    rv = lax.broadcasted_iota(jnp.int32, (Lp, 1), 0) < L
    cv = lax.broadcasted_iota(jnp.int32, (1, Lp), 1) < L

    for hh in range(B_HEADS):
        cols = slice(hh * dk, (hh + 1) * dk)
        q = padded(q_ref[:, cols].astype(F32))
        k = padded(k_ref[:, cols].astype(F32)) * (dk ** -0.5)
        v = padded(v_ref[:, cols].astype(F32))
        li_col = g[:, hh:hh + 1]
        li_row = gt[hh:hh + 1, :]
        lf_col = jax.nn.log_sigmoid(g[:, hh + B_HEADS:hh + B_HEADS + 1])
        lf_row = jax.nn.log_sigmoid(gt[hh + B_HEADS:hh + B_HEADS + 1, :])
        if L != Lp:
            li_col = jnp.where(rv, li_col, NEG)
            lf_col = jnp.where(rv, lf_col, 0.0)
            li_row = jnp.where(cv, li_row, NEG)
            lf_row = jnp.where(cv, lf_row, 0.0)

        b_col = jnp.sum(jnp.where(tri, lf_row, 0.0), axis=1, keepdims=True)
        b_row = jnp.sum(jnp.where(tri_t, lf_col, 0.0), axis=0, keepdims=True)
        m_prev = m_scr[hh]
        log_d = jnp.where(tri, b_col - b_row + li_row, NEG)
        m_inter = b_col + m_prev
        m_t = jnp.maximum(m_inter, jnp.max(log_d, axis=1, keepdims=True))
        qb = q.astype(BF16)
        kb = k.astype(BF16)
        vb = v.astype(BF16)
        s = _nt(qb, kb) * jnp.exp(log_d - m_t)
        w_inter = jnp.exp(m_inter - m_t)
        c_old = c_scr[hh]
        n_old = n_scr[hh]
        num = (jnp.dot(s.astype(BF16), vb, preferred_element_type=F32)
               + w_inter * jnp.dot(qb, c_old.astype(BF16), preferred_element_type=F32))
        den = jnp.sum(s, axis=1, keepdims=True) + w_inter * jnp.sum(q * n_old, axis=1, keepdims=True)
        h = num / jnp.maximum(jnp.abs(den), jnp.exp(-m_t))

        b_last = b_row[:, L - 1:L]
        m_new = jnp.maximum(b_last + m_prev, jnp.max(b_last - b_row + li_row, axis=1, keepdims=True))
        decay = jnp.exp(b_last + m_prev - m_new)
        wk = jnp.exp(b_last - b_col + li_col - m_new)
        kw = k * wk
        c_scr[hh] = decay * c_old + lax.dot_general(kw.astype(BF16), vb, (((0,), (0,)), ((), ())),
                                                    preferred_element_type=F32)
        n_scr[hh] = decay * n_old + jnp.sum(kw, axis=0, keepdims=True)
        m_scr[hh] = m_new

        hv = h[:L]
        mu = jnp.mean(hv, axis=1, keepdims=True)
        var = jnp.mean(jnp.square(hv - mu), axis=1, keepdims=True)
        hn = (hv - mu) * lax.rsqrt(var + LN_EPS) * ng_ref[hh]
        h_ref[:, cols] = (hn * jax.nn.sigmoid(og_ref[:, cols].astype(F32))
                          * _silu(gm_ref[:, cols].astype(F32))).astype(h_ref.dtype)

    @pl.when(c == pl.num_programs(1) - 1)
    def _():
        c_out[...] = c_scr[...]
        n_out[...] = n_scr[...]
        m_out[...] = m_scr[...]


def _mlstm(zm, gates, norm_g, c0, n0, m0, bsz, t, L):
    Lp = max(L, LANES)
    nc = t // L
    dh = B_HEAD_DIM
    g3 = gates.reshape(bsz, t, LANES)
    gt = jnp.swapaxes(gates[:, :2 * B_HEADS].reshape(bsz, nc, L, 2 * B_HEADS), 2, 3)
    if Lp != L:
        gt = jnp.pad(gt, ((0, 0), (0, 0), (0, 0), (0, Lp - L)))
    ng = norm_g.reshape(B_HEADS, 1, dh).astype(F32)
    n0 = n0.reshape(bsz, B_HEADS, 1, dh)
    m0 = m0.reshape(bsz, B_HEADS, 1, 1)
    kern = functools.partial(_mlstm_kernel, L=L, Lp=Lp)
    zspec = lambda gidx: pl.BlockSpec((None, None, L, B_WIDTH), lambda b, c: (gidx, b, c, 0))
    st = lambda r, cdim: pl.BlockSpec((None, B_HEADS, r, cdim), lambda b, c: (b, 0, 0, 0))
    h, c1, n1, m1 = pl.pallas_call(
        kern,
        grid=(bsz, nc),
        in_specs=[zspec(2), zspec(3), zspec(4), zspec(5), zspec(6),
                  pl.BlockSpec((None, L, LANES), lambda b, c: (b, c, 0)),
                  pl.BlockSpec((None, None, 2 * B_HEADS, Lp), lambda b, c: (b, c, 0, 0)),
                  pl.BlockSpec((B_HEADS, 1, dh), lambda b, c: (0, 0, 0)),
                  st(dh, dh), st(1, dh), st(1, 1)],
        out_specs=[pl.BlockSpec((None, L, B_WIDTH), lambda b, c: (b, c, 0)),
                   st(dh, dh), st(1, dh), st(1, 1)],
        out_shape=[jax.ShapeDtypeStruct((bsz, t, B_WIDTH), BF16),
                   jax.ShapeDtypeStruct((bsz, B_HEADS, dh, dh), F32),
                   jax.ShapeDtypeStruct((bsz, B_HEADS, 1, dh), F32),
                   jax.ShapeDtypeStruct((bsz, B_HEADS, 1, 1), F32)],
        scratch_shapes=[pltpu.VMEM((B_HEADS, dh, dh), F32), pltpu.VMEM((B_HEADS, 1, dh), F32),
                        pltpu.VMEM((B_HEADS, 1, 1), F32)],
        compiler_params=_params("parallel", "arbitrary"),
        name="mlstm",
    )(zm, zm, zm, zm, zm, g3, gt, ng, c0, n0, m0)
    return h, c1, n1.reshape(bsz, B_HEADS, dh), m1.reshape(bsz, B_HEADS)


def _outln_kernel(*refs, n_parts):
    mix = refs[:n_parts]
    ws = refs[n_parts:2 * n_parts]
    x_ref, g_ref, b_ref, o_ref, ob_ref = refs[2 * n_parts:]
    tm = x_ref.shape[0]
    sub = min(tm, LANES)
    for r in range(tm // sub):
        rows = pl.ds(r * sub, sub)
        y = jnp.dot(mix[0][rows, :], ws[0][...], preferred_element_type=F32)
        for p in range(1, n_parts):
            y = y + jnp.dot(mix[p][rows, :], ws[p][...], preferred_element_type=F32)
        z = ALPHA * x_ref[rows, :] + y
        mu = jnp.mean(z, axis=1, keepdims=True)
        var = jnp.mean(jnp.square(z - mu), axis=1, keepdims=True)
        xn = (z - mu) * lax.rsqrt(var + LN_EPS) * g_ref[...] + b_ref[...]
        o_ref[rows, :] = xn
        ob_ref[rows, :] = xn.astype(BF16)


def _out_ln(mix_parts, w_parts, x, ln_g, ln_b, tm):
    m, d = x.shape
    n_parts = len(mix_parts)
    in_specs = ([pl.BlockSpec((tm, mp.shape[1]), lambda i: (i, 0)) for mp in mix_parts]
                + [pl.BlockSpec(wp.shape, lambda i: (0, 0)) for wp in w_parts]
                + [pl.BlockSpec((tm, d), lambda i: (i, 0)),
                   pl.BlockSpec((1, d), lambda i: (0, 0)),
                   pl.BlockSpec((1, d), lambda i: (0, 0))])
    return pl.pallas_call(
        functools.partial(_outln_kernel, n_parts=n_parts),
        grid=(m // tm,),
        in_specs=in_specs,
        out_specs=[pl.BlockSpec((tm, d), lambda i: (i, 0)), pl.BlockSpec((tm, d), lambda i: (i, 0))],
        out_shape=[jax.ShapeDtypeStruct((m, d), F32), jax.ShapeDtypeStruct((m, d), BF16)],
        compiler_params=_params("parallel"),
        name="outproj_ln",
    )(*mix_parts, *w_parts, x, ln_g.reshape(1, d).astype(F32), ln_b.reshape(1, d).astype(F32))


def _odd_kernel(*refs, tm, carry_mode, seq_tiles, seq_rows):
    if carry_mode:
        x_ref, wb, wc, wx, wg, cw_ref, mix_ref, tail_ref, wbf, carry = refs
    else:
        x_ref, wb, wc, wx, wg, cw_ref, plast_ref, pprev_ref, mix_ref, tail_ref, wbf = refs

    @pl.when(pl.program_id(1) == 0)
    def _():
        for gidx, wref in enumerate((wb, wc, wx, wg)):
            wbf[gidx] = wref[...].astype(BF16)

    x = x_ref[...]
    zb = jnp.dot(x, wbf[0], preferred_element_type=F32)
    zc = jnp.dot(x, wbf[1], preferred_element_type=F32)
    zx = jnp.dot(x, wbf[2], preferred_element_type=F32)
    zg = jnp.dot(x, wbf[3], preferred_element_type=F32)
    u = zc * zx
    rows = lax.broadcasted_iota(jnp.int32, u.shape, 0)
    r1 = pltpu.roll(u, 1, 0)
    r2 = pltpu.roll(u, 2, 0)
    if carry_mode:
        @pl.when(pl.program_id(1) % seq_tiles == 0)
        def _():
            carry[...] = jnp.zeros_like(carry)
        last = carry[SUBLANES - 1:SUBLANES, :]
        prev = carry[SUBLANES - 2:SUBLANES - 1, :]
        rm = rows
    else:
        last = plast_ref[...]
        prev = pprev_ref[...]
        rm = rows & (seq_rows - 1)
    u1 = jnp.where(rm == 0, last, r1)
    u2 = jnp.where(rm == 0, prev, jnp.where(rm == 1, last, r2))
    cw = cw_ref[...]
    y = cw[0:1, :] * u2 + cw[1:2, :] * u1 + cw[2:3, :] * u
    mix_ref[...] = (zb * y * _silu(zg)).astype(mix_ref.dtype)
    if carry_mode:
        tail = u[tm - SUBLANES:, :]
        carry[...] = tail
        tail_ref[...] = tail
    else:
        tail_ref[...] = u


def _odd_mix(x_bf, w_all, o, conv_w, tm, tn, seq_len, state=None):
    m, k = x_bf.shape
    nn = C_WIDTH // tn
    carry_mode = state is None
    wspec = lambda gidx: pl.BlockSpec((None, k, tn), lambda j, i: (o, 0, gidx * nn + j))
    in_specs = [pl.BlockSpec((tm, k), lambda j, i: (i, 0)),
                wspec(0), wspec(1), wspec(2), wspec(3),
                pl.BlockSpec((CONV_K, tn), lambda j, i: (0, j))]
    args = [x_bf, w_all, w_all, w_all, w_all, conv_w.astype(F32)]
    scratch = [pltpu.VMEM((4, k, tn), BF16)]
    if carry_mode:
        assert seq_len % tm == 0
        tail_shape = jax.ShapeDtypeStruct((m // tm, SUBLANES, C_WIDTH), F32)
        tail_spec = pl.BlockSpec((None, SUBLANES, tn), lambda j, i: (i, 0, j))
        scratch.append(pltpu.VMEM((SUBLANES, tn), F32))
    else:
        assert tm % seq_len == 0 and seq_len & (seq_len - 1) == 0
        in_specs += [pl.BlockSpec((tm, tn), lambda j, i: (i, j))] * 2
        args += list(state)
        tail_shape = jax.ShapeDtypeStruct((m, C_WIDTH), F32)
        tail_spec = pl.BlockSpec((tm, tn), lambda j, i: (i, j))
    kern = functools.partial(_odd_kernel, tm=tm, carry_mode=carry_mode,
                             seq_tiles=max(seq_len // tm, 1), seq_rows=seq_len)
    return pl.pallas_call(
        kern,
        grid=(nn, m // tm),
        in_specs=in_specs,
        out_specs=[pl.BlockSpec((tm, tn), lambda j, i: (i, j)), tail_spec],
        out_shape=[jax.ShapeDtypeStruct((m, C_WIDTH), BF16), tail_shape],
        scratch_shapes=scratch,
        compiler_params=_params("parallel", "arbitrary"),
        name="odd_mix",
    )(*args)


def _row_tiles(m):
    return min(m, 1024), min(m, 512)


def _even_layer(x, x_bf, bsz, t, wt_all, e, gbias, w_out_a, w_out_b, norm_g, ln_g, ln_b,
                bias_tile, past):
    tm, tm_out = _row_tiles(bsz * t)
    (zm,) = _inproj_even(x_bf, wt_all, e, (0, 2, N_GROUPS_EVEN - 2), tm)
    zk, ka = _inproj_even(x_bf, wt_all, e, (1, 0, 1), tm, heads_out=True)
    zv, va = _inproj_even(x_bf, wt_all, e, (2, 0, 1), tm, heads_out=True)
    zm = zm.reshape(N_GROUPS_EVEN - 2, bsz, t, 1024)
    zk = zk.reshape(1, bsz, t, 1024)
    zv = zv.reshape(1, bsz, t, 1024)
    gates = _gates(x_bf, wt_all, e, gbias, tm)
    if past is None:
        attn = _moba_prompt(zm, zk, zv, bias_tile, bsz, t)
        c0 = jnp.zeros((bsz, B_HEADS, B_HEAD_DIM, B_HEAD_DIM), F32)
        n0 = jnp.zeros((bsz, B_HEADS, B_HEAD_DIM), F32)
        m0 = jnp.zeros((bsz, B_HEADS), F32)
        L = math.gcd(t, MLSTM_PROMPT_CHUNK)
    else:
        cache_k, cache_v, page_table, c0, n0, m0 = past
        attn = _moba_sample(zm, zk, zv, bias_tile, cache_k, cache_v, e, page_table, bsz, t)
        L = math.gcd(t, MLSTM_CHUNK)
    hm, c1, n1, m1 = _mlstm(zm, gates, norm_g, c0, n0, m0, bsz, t, L)
    xn, xn_bf = _out_ln([attn.reshape(bsz * t, A_WIDTH), hm.reshape(bsz * t, B_WIDTH)],
                        [w_out_a, w_out_b], x, ln_g, ln_b, tm_out)
    ka = ka.reshape(bsz, t, A_HEADS, A_HEAD_DIM)
    va = va.reshape(bsz, t, A_HEADS, A_HEAD_DIM)
    return xn, xn_bf, (ka, va, c1, n1, m1)


def _odd_layer(x, x_bf, bsz, t, w_in_all, o, w_out, conv_w, ln_g, ln_b, buf):
    tm, tm_out = _row_tiles(bsz * t)
    if buf is None:
        tm = math.gcd(2 * tm, t)
        mix, tail = _odd_mix(x_bf, w_in_all, o, conv_w, tm, 256, t)
        seq_tiles = t // tm
        new_buf = tail.reshape(bsz, seq_tiles, SUBLANES, C_WIDTH)[:, -1, SUBLANES - (CONV_K - 1):, :]
    else:
        plast = jnp.repeat(buf[:, 1, :], t, axis=0)
        pprev = jnp.repeat(buf[:, 0, :], t, axis=0)
        mix, u = _odd_mix(x_bf, w_in_all, o, conv_w, tm, 256, t, state=(plast, pprev))
        new_buf = u.reshape(bsz, t, C_WIDTH)[:, t - (CONV_K - 1):, :]
    xn, xn_bf = _out_ln([mix], [w_out], x, ln_g, ln_b, tm_out)
    return xn, xn_bf, new_buf


def kernel(x_prompt, x_sample, cache_k, cache_v, page_table, state_C, state_n, state_m, state_conv,
           w_in_even, w_out_even, mlstm_gate_bias, mlstm_norm_g, rel_bias,
           w_in_odd, w_out_odd, conv_w, ln_g, ln_b):
    bp, tp, d = x_prompt.shape
    bs, ts, _ = x_sample.shape
    bias_p = _bias_tiles(rel_bias, _prompt_bias_idx())
    bias_s = _bias_tiles(rel_bias, _sample_bias_idx(ts)).reshape(A_HEADS * ts, -1)

    xp = x_prompt.reshape(bp * tp, d)
    xs = x_sample.reshape(bs * ts, d)
    xp_bf = xp.astype(BF16)
    xs_bf = xs.astype(BF16)
    wt_even = jnp.swapaxes(w_in_even, 1, 2)

    pk, pv, pc, pn, pm, pb = [], [], [], [], [], []
    sk, sv, sc, sn, sm, sb = [], [], [], [], [], []
    for layer in range(DEPTH):
        if layer % 2 == 0:
            e = layer // 2
            gbias = jnp.pad(mlstm_gate_bias[e], (0, LANES - 2 * B_HEADS)).reshape(1, LANES).astype(F32)
            w_out_a = w_out_even[e, :A_WIDTH].astype(BF16)
            w_out_b = w_out_even[e, A_WIDTH:].astype(BF16)
            xp, xp_bf, (k1, v1, c1, n1, m1) = _even_layer(
                xp, xp_bf, bp, tp, wt_even, e, gbias, w_out_a, w_out_b, mlstm_norm_g[e],
                ln_g[layer], ln_b[layer], bias_p, None)
            xs, xs_bf, (k2, v2, c2, n2, m2) = _even_layer(
                xs, xs_bf, bs, ts, wt_even, e, gbias, w_out_a, w_out_b, mlstm_norm_g[e],
                ln_g[layer], ln_b[layer], bias_s,
                (cache_k, cache_v, page_table, state_C[e], state_n[e], state_m[e]))
            pk.append(k1); pv.append(v1); pc.append(c1); pn.append(n1); pm.append(m1)
            sk.append(k2); sv.append(v2); sc.append(c2); sn.append(n2); sm.append(m2)
        else:
            o = layer // 2
            w_out = w_out_odd[o].astype(BF16)
            xp, xp_bf, b1 = _odd_layer(xp, xp_bf, bp, tp, w_in_odd, o, w_out, conv_w[o],
                                       ln_g[layer], ln_b[layer], None)
            xs, xs_bf, b2 = _odd_layer(xs, xs_bf, bs, ts, w_in_odd, o, w_out, conv_w[o],
                                       ln_g[layer], ln_b[layer], state_conv[o])
            pb.append(b1); sb.append(b2)
    return (xp.reshape(bp, tp, d), xs.reshape(bs, ts, d),
            jnp.stack(pk), jnp.stack(pv), jnp.stack(pc), jnp.stack(pn), jnp.stack(pm), jnp.stack(pb),
            jnp.stack(sk), jnp.stack(sv), jnp.stack(sc), jnp.stack(sn), jnp.stack(sm), jnp.stack(sb))
```

```python
import functools
import math

import numpy as np
import jax
import jax.numpy as jnp
from jax import lax
from jax.experimental import pallas as pl
from jax.experimental.pallas import tpu as pltpu

F32 = jnp.float32
BF16 = jnp.bfloat16

D_MODEL = 2048
DEPTH = 4
PAGE_SIZE = 128
A_HEADS = 8
A_HEAD_DIM = 128
A_WIDTH = A_HEADS * A_HEAD_DIM
MOBA_BLOCK = 256
MOBA_TOPK = 3
REL_BUCKETS = 32
REL_MAX_DIST = 128
B_HEADS = 4
B_HEAD_DIM = 256
B_WIDTH = B_HEADS * B_HEAD_DIM
MLSTM_CHUNK = 64
MLSTM_PROMPT_CHUNK = 256
C_WIDTH = D_MODEL
CONV_K = 3
N_GROUPS_EVEN = 9
EVEN_MAIN = N_GROUPS_EVEN * 1024
ALPHA = (2.0 * DEPTH) ** 0.25
LN_EPS = 1e-5

LANES = 128
SUBLANES = 8
VMEM_LIMIT = 56 * 1024 * 1024

NEG = -1e30


def _params(*sem):
    return pltpu.CompilerParams(dimension_semantics=sem, vmem_limit_bytes=VMEM_LIMIT)


def _nt(a, b):
    return lax.dot_general(a, b, (((1,), (1,)), ((), ())), preferred_element_type=F32)


def _silu(x):
    return x * jax.nn.sigmoid(x)


def _bucket_np(d):
    d = np.asarray(d, np.int64)
    n = np.maximum(d, 0)
    max_exact = REL_BUCKETS // 2
    nf = np.maximum(n, 1).astype(np.float64)
    large = max_exact + (np.log(nf / max_exact) / math.log(REL_MAX_DIST / max_exact)
                         * (REL_BUCKETS - max_exact)).astype(np.int64)
    large = np.minimum(large, REL_BUCKETS - 1)
    out = np.where(n < max_exact, n, large)
    return np.where(d < 0, -1, out).astype(np.int32)


def _bias_kernel(tab_ref, idx_ref, o_ref):
    h = pl.program_id(0)
    idx = idx_ref[...]
    out = jnp.full(idx.shape, NEG, F32)
    for b in range(REL_BUCKETS):
        out = jnp.where(idx == b, tab_ref[h, b], out)
    o_ref[...] = out


def _bias_tiles(rel_bias, idx):
    r, c = idx.shape
    return pl.pallas_call(
        _bias_kernel,
        grid=(A_HEADS,),
        in_specs=[pl.BlockSpec(memory_space=pltpu.SMEM),
                  pl.BlockSpec((r, c), lambda h: (0, 0))],
        out_specs=pl.BlockSpec((None, r, c), lambda h: (h, 0, 0)),
        out_shape=jax.ShapeDtypeStruct((A_HEADS, r, c), F32),
        compiler_params=_params("parallel"),
        name="bias_tiles",
    )(rel_bias.astype(F32), jnp.asarray(idx))


def _prompt_bias_idx():
    qi = np.arange(MOBA_BLOCK)[:, None]
    kj = np.arange(2 * MOBA_BLOCK)[None, :]
    return _bucket_np(qi + MOBA_BLOCK - kj)


def _sample_bias_idx(s_len):
    s = np.arange(s_len)[:, None]
    lane = np.arange(2 * PAGE_SIZE * A_HEADS)[None, :]
    off = lane // A_HEADS
    far = _bucket_np(MOBA_BLOCK + s - off)
    lo = np.arange(LANES)[None, :]
    own = _bucket_np(np.where(lo < A_HEADS * s_len, s - (lo % s_len), -1))
    return np.concatenate([far, own], axis=1)


def _inproj_prep(w_ref, wb_ref):
    @pl.when(pl.program_id(1) == 0)
    def _():
        wb_ref[...] = w_ref[...].astype(BF16)


def _inproj_compute(x_ref, o_ref, oh_ref, wb_ref, part=0, n_parts=1):
    tm = x_ref.shape[0]
    nf = wb_ref.shape[0] // n_parts
    cols = pl.ds(part * nf, nf)
    y = _nt(x_ref[...], wb_ref[cols, :])
    o_ref[:, cols] = y.astype(o_ref.dtype)
    if oh_ref is not None:
        nh = nf // A_HEAD_DIM
        oh_ref[:, pl.ds(part * nh, nh), :] = y.reshape(tm, nh, A_HEAD_DIM)


def _inproj_kernel(x_ref, w_ref, *rest, heads_out):
    if heads_out:
        o_ref, oh_ref, wb_ref = rest
    else:
        (o_ref, wb_ref), oh_ref = rest, None
    _inproj_prep(w_ref, wb_ref)
    _inproj_compute(x_ref, o_ref, oh_ref, wb_ref)


def _inproj_even(x_bf, wt_all, e, groups, tm, heads_out=False):
    m, k = x_bf.shape
    gw = 1024
    first, skip, n_groups = groups
    gmap = lambda j: first + j + skip * jnp.minimum(j, 1)
    out_specs = [pl.BlockSpec((None, tm, gw), lambda j, i: (j, i, 0))]
    out_shape = [jax.ShapeDtypeStruct((n_groups, m, gw), BF16)]
    if heads_out:
        assert n_groups == 1
        out_specs.append(pl.BlockSpec((tm, A_HEADS, A_HEAD_DIM), lambda j, i: (i, 0, 0)))
        out_shape.append(jax.ShapeDtypeStruct((m, A_HEADS, A_HEAD_DIM), F32))
    return pl.pallas_call(
        functools.partial(_inproj_kernel, heads_out=heads_out),
        grid=(n_groups, m // tm),
        in_specs=[pl.BlockSpec((tm, k), lambda j, i: (i, 0)),
                  pl.BlockSpec((None, gw, k), lambda j, i: (e, gmap(j), 0))],
        out_specs=out_specs,
        out_shape=out_shape,
        scratch_shapes=[pltpu.VMEM((gw, k), BF16)],
        compiler_params=_params("parallel", "arbitrary"),
        name="inproj_even",
    )(x_bf, wt_all)


def _gates_kernel(x_ref, w_ref, b_ref, o_ref):
    w = w_ref[...]
    wp = jnp.concatenate([w, jnp.zeros((LANES - w.shape[0], w.shape[1]), F32)], axis=0)
    o_ref[...] = _nt(x_ref[...], wp.astype(BF16)) + b_ref[...]


def _gates(x_bf, wt_all, e, gbias, tm):
    m, k = x_bf.shape
    ng = 2 * B_HEADS
    assert ng == SUBLANES and EVEN_MAIN % ng == 0
    return pl.pallas_call(
        _gates_kernel,
        grid=(m // tm,),
        in_specs=[pl.BlockSpec((tm, k), lambda i: (i, 0)),
                  pl.BlockSpec((None, ng, k), lambda i: (e, EVEN_MAIN // ng, 0)),
                  pl.BlockSpec((1, LANES), lambda i: (0, 0))],
        out_specs=pl.BlockSpec((tm, LANES), lambda i: (i, 0)),
        out_shape=jax.ShapeDtypeStruct((m, LANES), F32),
        compiler_params=_params("parallel"),
        name="mlstm_gates",
    )(x_bf, wt_all, gbias)


def _moba_prompt_kernel(q_ref, k_ref, v_ref, g_ref, bias_ref, o_ref, kmean_ref, negm_ref,
                        *, nblk, scale):
    blk = MOBA_BLOCK
    for j in range(nblk):
        kmean_ref[j:j + 1, :] = jnp.mean(k_ref[j * blk:(j + 1) * blk, :].astype(F32), axis=0, keepdims=True)
    far_bias = bias_ref[0:1, 0:1]
    eye = (lax.broadcasted_iota(jnp.int32, (blk, blk), 0)
           == lax.broadcasted_iota(jnp.int32, (blk, blk), 1)).astype(BF16)
    rowi = lax.broadcasted_iota(jnp.int32, (nblk, blk), 0)

    i0 = MOBA_TOPK + 1
    if i0 < nblk:
        nq = (nblk - i0) * blk
        gt = lax.dot_general(kmean_ref[...], q_ref[i0 * blk:, :].astype(F32), (((1,), (1,)), ((), ())),
                             precision=lax.Precision.HIGHEST, preferred_element_type=F32)
        rowq = lax.broadcasted_iota(jnp.int32, (nblk, nq), 0)
        own = i0 + (lax.broadcasted_iota(jnp.int32, (nblk, nq), 1) >> (blk.bit_length() - 1))
        valid = rowq < own
        sel_t = jnp.zeros((nblk, nq), F32)
        for j in range(nblk - 1):
            gj = gt[j:j + 1, :]
            beats = ((gt > gj) | ((gt == gj) & (rowq < j))) & valid
            rank = jnp.sum(beats.astype(F32), axis=0, keepdims=True)
            sel_t = jnp.where(rowq == j, (rank < MOBA_TOPK).astype(F32), sel_t)
        sel_t = jnp.concatenate([sel_t, jnp.zeros((LANES - nblk, nq), F32)], axis=0).astype(BF16)
        for i in range(i0, nblk):
            cs = (i - i0) * blk
            sel = _nt(eye, sel_t[:, cs:cs + blk])
            negm_ref[i - i0] = (sel - 1.0) * (-NEG)

    for i in range(nblk):
        q = q_ref[i * blk:(i + 1) * blk, :].astype(F32)
        negm = negm_ref[i - MOBA_TOPK - 1] if i > MOBA_TOPK else None
        w = (i + 1) * blk
        s = _nt((q * scale).astype(BF16), k_ref[0:w, :])
        pieces = []
        for j in range(i + 1):
            sj = s[:, j * blk:(j + 1) * blk]
            if j == i:
                sj = sj + bias_ref[:, blk:2 * blk]
            elif j == i - 1:
                sj = sj + bias_ref[:, 0:blk]
                if negm is not None:
                    sj = sj + negm[:, j:j + 1]
            else:
                sj = sj + (far_bias if negm is None else far_bias + negm[:, j:j + 1])
            pieces.append(sj)
        m = jnp.max(functools.reduce(jnp.maximum, pieces), axis=1, keepdims=True)
        ps = [jnp.exp(sj - m) for sj in pieces]
        l = jnp.sum(functools.reduce(jnp.add, ps), axis=1, keepdims=True)
        pcat = ps[0].astype(BF16) if i == 0 else jnp.concatenate([p.astype(BF16) for p in ps], axis=1)
        acc = jnp.dot(pcat, v_ref[0:w, :], preferred_element_type=F32)
        g = g_ref[i * blk:(i + 1) * blk, :].astype(F32)
        o_ref[i * blk:(i + 1) * blk, :] = (acc / l * _silu(g)).astype(o_ref.dtype)


def _moba_prompt(zm, zk, zv, bias_p, bsz, t):
    blk = MOBA_BLOCK
    nblk = t // blk
    assert nblk <= SUBLANES
    dh = A_HEAD_DIM
    kern = functools.partial(_moba_prompt_kernel, nblk=nblk, scale=dh ** -0.5)
    zspec = lambda g: pl.BlockSpec((None, None, t, dh), lambda b, h: (g, b, 0, h))
    return pl.pallas_call(
        kern,
        grid=(bsz, A_HEADS),
        in_specs=[zspec(0), zspec(0), zspec(0), zspec(1),
                  pl.BlockSpec((None, blk, 2 * blk), lambda b, h: (h, 0, 0))],
        out_specs=pl.BlockSpec((None, t, dh), lambda b, h: (b, 0, h)),
        out_shape=jax.ShapeDtypeStruct((bsz, t, A_WIDTH), BF16),
        scratch_shapes=[pltpu.VMEM((nblk, dh), F32),
                        pltpu.VMEM((max(nblk - MOBA_TOPK - 1, 1), blk, LANES), F32)],
        compiler_params=_params("parallel", "parallel"),
        name="moba_prompt",
    )(zm, zk, zv, zm, bias_p)


def _stack_heads(x):
    return jnp.concatenate([x[:, h * A_HEAD_DIM:(h + 1) * A_HEAD_DIM] for h in range(A_HEADS)], axis=0)


def _moba_sample_body(q_ref, kn_ref, vn_ref, ga_ref, bias_ref, k_refs, v_refs, o_ref,
                      qm_ref, gate_ref, m_ref, l_ref, oblk_ref, *, nb, bps, s_len, scale,
                      alongside=None):
    n = pl.program_id(1)
    rows = A_HEADS * s_len
    s_shift = s_len.bit_length() - 1
    far_w = PAGE_SIZE * A_HEADS

    @pl.when(n == 0)
    def _():
        qm_ref[...] = _stack_heads(q_ref[...].astype(F32))
        gate_ref[...] = jnp.full_like(gate_ref, NEG)
        m_ref[...] = jnp.full_like(m_ref, NEG)
        l_ref[...] = jnp.zeros_like(l_ref)


    qb = qm_ref[...].astype(BF16)
    row = lax.broadcasted_iota(jnp.int32, (rows, far_w), 0)
    lane = lax.broadcasted_iota(jnp.int32, (rows, far_w), 1)
    useful = (lane & (A_HEADS - 1)) == (row >> s_shift)
    lane_b = lax.broadcasted_iota(jnp.int32, (rows, LANES), 1)
    last = n == pl.num_programs(1) - 1
    far_bias = bias_ref[:, 0:1]

    def pages(ref):
        return ref[...].reshape(far_w, A_HEAD_DIM).astype(BF16)

    gate_new = gate_ref[...]
    m_new = m_ref[...]
    l_new = l_ref[...]
    for c in range(bps):
        blk = n * bps + c
        raw0 = _nt(qb, pages(k_refs[2 * c]))
        raw1 = _nt(qb, pages(k_refs[2 * c + 1]))
        if alongside is not None:
            alongside(c, bps)
        gsum = jnp.sum(jnp.where(useful, raw0 + raw1, 0.0), axis=1, keepdims=True)
        if c == bps - 1:
            last_v = (jnp.zeros((rows, far_w), jnp.int32) + blk) == nb - 1
            b0 = jnp.where(last_v, bias_ref[:, 0:far_w], far_bias)
            b1 = jnp.where(last_v, bias_ref[:, far_w:2 * far_w], far_bias)
        else:
            b0 = b1 = far_bias
        s0 = jnp.where(useful, raw0 * scale + b0, NEG)
        s1 = jnp.where(useful, raw1 * scale + b1, NEG)
        mloc = jnp.max(jnp.maximum(s0, s1), axis=1, keepdims=True)
        p0 = jnp.exp(s0 - mloc)
        p1 = jnp.exp(s1 - mloc)
        lloc = jnp.sum(p0 + p1, axis=1, keepdims=True)
        oblk_ref[blk] = (jnp.dot(p0.astype(BF16), pages(v_refs[2 * c]), preferred_element_type=F32)
                         + jnp.dot(p1.astype(BF16), pages(v_refs[2 * c + 1]), preferred_element_type=F32))
        here = lane_b == blk
        gate_new = jnp.where(here, gsum * (1.0 / MOBA_BLOCK), gate_new)
        m_new = jnp.where(here, mloc, m_new)
        l_new = jnp.where(here, lloc, l_new)
    gate_ref[...] = gate_new
    m_ref[...] = m_new
    l_ref[...] = l_new

    @pl.when(last)
    def _():
        row_b = lax.broadcasted_iota(jnp.int32, (rows, LANES), 0)
        lane_f = lane_b.astype(F32)
        g = jnp.where(lane_b < nb, gate_ref[...], -jnp.inf)
        sel = jnp.zeros((rows, LANES), jnp.bool_)
        for _r in range(min(MOBA_TOPK, nb)):
            mx = jnp.max(g, axis=1, keepdims=True)
            idx = jnp.min(jnp.where(g == mx, lane_f, float(LANES)), axis=1, keepdims=True)
            hit = lane_f == idx
            sel = sel | hit
            g = jnp.where(hit, -jnp.inf, g)
        pad = jnp.zeros((LANES - rows, A_HEAD_DIM), F32)
        kn = jnp.concatenate([_stack_heads(kn_ref[...].astype(F32)), pad], axis=0).astype(BF16)
        vn = jnp.concatenate([_stack_heads(vn_ref[...].astype(F32)), pad], axis=0).astype(BF16)
        so = _nt(qb, kn) * scale + bias_ref[:, 2 * far_w:2 * far_w + LANES]
        ok = ((lane_b < rows) & ((lane_b >> s_shift) == (row_b >> s_shift))
              & ((lane_b & (s_len - 1)) <= (row_b & (s_len - 1))))
        so = jnp.where(ok, so, NEG)
        mm = m_ref[...]
        mtot = jnp.maximum(jnp.max(jnp.where(sel, mm, NEG), axis=1, keepdims=True),
                           jnp.max(so, axis=1, keepdims=True))
        w = jnp.where(sel, jnp.exp(mm - mtot), 0.0)
        po = jnp.exp(so - mtot)
        lsum = jnp.sum(w * l_ref[...], axis=1, keepdims=True) + jnp.sum(po, axis=1, keepdims=True)
        acc = jnp.dot(po.astype(BF16), vn, preferred_element_type=F32)
        for nn in range(nb):
            acc = acc + w[:, nn:nn + 1] * oblk_ref[nn]
        out = acc / lsum
        res = jnp.concatenate([out[h * s_len:(h + 1) * s_len, :] for h in range(A_HEADS)], axis=1)
        o_ref[...] = (res * _silu(ga_ref[...].astype(F32))).astype(o_ref.dtype)


def _inproj_attn_kernel(pt_ref, x_ref, w_ref, q_ref, kn_ref, vn_ref, ga_ref, bias_ref, *rest,
                        heads_out, nb, bps, s_len, scale):
    del pt_ref
    npg = 2 * bps
    k_refs = rest[:npg]
    v_refs = rest[npg:2 * npg]
    rest = rest[2 * npg:]
    if heads_out:
        o_ref, oh_ref, ao_ref, wb_ref = rest[:4]
    else:
        (o_ref, ao_ref, wb_ref), oh_ref = rest[:3], None
    _inproj_prep(w_ref, wb_ref)
    _moba_sample_body(q_ref, kn_ref, vn_ref, ga_ref, bias_ref, k_refs, v_refs, ao_ref, *rest[-5:],
                      nb=nb, bps=bps, s_len=s_len, scale=scale,
                      alongside=functools.partial(_inproj_compute, x_ref, o_ref, oh_ref, wb_ref))


def _inproj_even_with_attn(x_bf, wt_all, e, groups, tm, heads_out, sample, b0):
    zm, zk, zv, bias_s, cache_k, cache_v, page_table, s_len = sample
    m, k = x_bf.shape
    gw = 1024
    first, skip, n_groups = groups
    nsteps = m // tm
    n_pages = page_table.shape[1]
    ppb = MOBA_BLOCK // PAGE_SIZE
    assert ppb == 2 and n_pages % ppb == 0 and s_len == SUBLANES and A_HEADS == SUBLANES
    nb = n_pages // ppb
    assert nb <= LANES and nb % nsteps == 0
    bps = nb // nsteps
    rows = A_HEADS * s_len
    dh = A_HEAD_DIM
    gmap = lambda j: first + j + skip * jnp.minimum(j, 1)
    tok = lambda g: pl.BlockSpec((None, None, s_len, A_WIDTH), lambda j, i, pt: (g, b0 + j, 0, 0))
    page = lambda o: pl.BlockSpec((None, None, PAGE_SIZE, A_HEADS, dh),
                                  lambda j, i, pt: (e, pt[b0 + j, ppb * bps * i + o], 0, 0, 0))
    pages = [page(o) for o in range(ppb * bps)]
    out_specs = [pl.BlockSpec((None, tm, gw), lambda j, i, pt: (j, i, 0))]
    out_shape = [jax.ShapeDtypeStruct((n_groups, m, gw), BF16)]
    if heads_out:
        assert n_groups == 1
        out_specs.append(pl.BlockSpec((tm, A_HEADS, dh), lambda j, i, pt: (i, 0, 0)))
        out_shape.append(jax.ShapeDtypeStruct((m, A_HEADS, dh), F32))
    out_specs.append(pl.BlockSpec((None, s_len, A_WIDTH), lambda j, i, pt: (j, 0, 0)))
    out_shape.append(jax.ShapeDtypeStruct((n_groups, s_len, A_WIDTH), BF16))
    grid_spec = pltpu.PrefetchScalarGridSpec(
        num_scalar_prefetch=1,
        grid=(n_groups, nsteps),
        in_specs=[pl.BlockSpec((tm, k), lambda j, i, pt: (i, 0)),
                  pl.BlockSpec((None, gw, k), lambda j, i, pt: (e, gmap(j), 0)),
                  tok(0), tok(0), tok(0), tok(1),
                  pl.BlockSpec(bias_s.shape, lambda j, i, pt: (0, 0))] + pages + pages,
        out_specs=out_specs,
        scratch_shapes=[pltpu.VMEM((gw, k), BF16),
                        pltpu.VMEM((rows, dh), F32),
                        pltpu.VMEM((rows, LANES), F32),
                        pltpu.VMEM((rows, LANES), F32),
                        pltpu.VMEM((rows, LANES), F32),
                        pltpu.VMEM((nb, rows, dh), F32)])
    kern = functools.partial(_inproj_attn_kernel, heads_out=heads_out, nb=nb, bps=bps, s_len=s_len,
                             scale=dh ** -0.5)
    return pl.pallas_call(
        kern,
        grid_spec=grid_spec,
        out_shape=out_shape,
        compiler_params=_params("arbitrary", "arbitrary"),
        name="inproj_even_sample_attn",
    )(page_table, x_bf, wt_all, zm, zk, zv, zm, bias_s,
      *([cache_k] * (ppb * bps)), *([cache_v] * (ppb * bps)))


def _mlstm_kernel(q_ref, k_ref, v_ref, og_ref, gm_ref, g_ref, gt_ref, ng_ref, c0_ref, n0_ref, m0_ref,
                  h_ref, c_out, n_out, m_out, c_scr, n_scr, m_scr, *, L, Lp):
    c = pl.program_id(1)
    dk = B_HEAD_DIM

    @pl.when(c == 0)
    def _():
        c_scr[...] = c0_ref[...]
        n_scr[...] = n0_ref[...]
        m_scr[...] = m0_ref[...]

    def padded(x):
        if L == Lp:
            return x
        return jnp.concatenate([x, jnp.zeros((Lp - L, x.shape[1]), x.dtype)], axis=0)

    g = padded(g_ref[...])
    gt = gt_ref[...]
    r = lax.broadcasted_iota(jnp.int32, (Lp, Lp), 0)
    cc = lax.broadcasted_iota(jnp.int32, (Lp, Lp), 1)
    tri = r >= cc
    tri_t = r <= cc
    rv = lax.broadcasted_iota(jnp.int32, (Lp, 1), 0) < L
    cv = lax.broadcasted_iota(jnp.int32, (1, Lp), 1) < L

    for hh in range(B_HEADS):
        cols = slice(hh * dk, (hh + 1) * dk)
        q = padded(q_ref[:, cols].astype(F32))
        k = padded(k_ref[:, cols].astype(F32)) * (dk ** -0.5)
        v = padded(v_ref[:, cols].astype(F32))
        li_col = g[:, hh:hh + 1]
        li_row = gt[hh:hh + 1, :]
        lf_col = jax.nn.log_sigmoid(g[:, hh + B_HEADS:hh + B_HEADS + 1])
        lf_row = jax.nn.log_sigmoid(gt[hh + B_HEADS:hh + B_HEADS + 1, :])
        if L != Lp:
            li_col = jnp.where(rv, li_col, NEG)
            lf_col = jnp.where(rv, lf_col, 0.0)
            li_row = jnp.where(cv, li_row, NEG)
            lf_row = jnp.where(cv, lf_row, 0.0)

        b_col = jnp.sum(jnp.where(tri, lf_row, 0.0), axis=1, keepdims=True)
        b_row = jnp.sum(jnp.where(tri_t, lf_col, 0.0), axis=0, keepdims=True)
        m_prev = m_scr[hh]
        log_d = jnp.where(tri, b_col - b_row + li_row, NEG)
        m_inter = b_col + m_prev
        m_t = jnp.maximum(m_inter, jnp.max(log_d, axis=1, keepdims=True))
        qb = q.astype(BF16)
        kb = k.astype(BF16)
        vb = v.astype(BF16)
        s = _nt(qb, kb) * jnp.exp(log_d - m_t)
        w_inter = jnp.exp(m_inter - m_t)
        c_old = c_scr[hh]
        n_old = n_scr[hh]
        num = (jnp.dot(s.astype(BF16), vb, preferred_element_type=F32)
               + w_inter * jnp.dot(qb, c_old.astype(BF16), preferred_element_type=F32))
        den = jnp.sum(s, axis=1, keepdims=True) + w_inter * jnp.sum(q * n_old, axis=1, keepdims=True)
        h = num / jnp.maximum(jnp.abs(den), jnp.exp(-m_t))

        b_last = b_row[:, L - 1:L]
        m_new = jnp.maximum(b_last + m_prev, jnp.max(b_last - b_row + li_row, axis=1, keepdims=True))
        decay = jnp.exp(b_last + m_prev - m_new)
        wk = jnp.exp(b_last - b_col + li_col - m_new)
        kw = k * wk
        c_scr[hh] = decay * c_old + lax.dot_general(kw.astype(BF16), vb, (((0,), (0,)), ((), ())),
                                                    preferred_element_type=F32)
        n_scr[hh] = decay * n_old + jnp.sum(kw, axis=0, keepdims=True)
        m_scr[hh] = m_new

        hv = h[:L]
        mu = jnp.mean(hv, axis=1, keepdims=True)
        var = jnp.mean(jnp.square(hv - mu), axis=1, keepdims=True)
        hn = (hv - mu) * lax.rsqrt(var + LN_EPS) * ng_ref[hh]
        h_ref[:, cols] = (hn * jax.nn.sigmoid(og_ref[:, cols].astype(F32))
                          * _silu(gm_ref[:, cols].astype(F32))).astype(h_ref.dtype)

    @pl.when(c == pl.num_programs(1) - 1)
    def _():
        c_out[...] = c_scr[...]
        n_out[...] = n_scr[...]
        m_out[...] = m_scr[...]


def _mlstm(zm, gates, norm_g, c0, n0, m0, bsz, t, L):
    Lp = max(L, LANES)
    nc = t // L
    dh = B_HEAD_DIM
    g3 = gates.reshape(bsz, t, LANES)
    gt = jnp.swapaxes(gates[:, :2 * B_HEADS].reshape(bsz, nc, L, 2 * B_HEADS), 2, 3)
    if Lp != L:
        gt = jnp.pad(gt, ((0, 0), (0, 0), (0, 0), (0, Lp - L)))
    ng = norm_g.reshape(B_HEADS, 1, dh).astype(F32)
    n0 = n0.reshape(bsz, B_HEADS, 1, dh)
    m0 = m0.reshape(bsz, B_HEADS, 1, 1)
    kern = functools.partial(_mlstm_kernel, L=L, Lp=Lp)
    zspec = lambda gidx: pl.BlockSpec((None, None, L, B_WIDTH), lambda b, c: (gidx, b, c, 0))
    st = lambda r, cdim: pl.BlockSpec((None, B_HEADS, r, cdim), lambda b, c: (b, 0, 0, 0))
    h, c1, n1, m1 = pl.pallas_call(
        kern,
        grid=(bsz, nc),
        in_specs=[zspec(2), zspec(3), zspec(4), zspec(5), zspec(6),
                  pl.BlockSpec((None, L, LANES), lambda b, c: (b, c, 0)),
                  pl.BlockSpec((None, None, 2 * B_HEADS, Lp), lambda b, c: (b, c, 0, 0)),
                  pl.BlockSpec((B_HEADS, 1, dh), lambda b, c: (0, 0, 0)),
                  st(dh, dh), st(1, dh), st(1, 1)],
        out_specs=[pl.BlockSpec((None, L, B_WIDTH), lambda b, c: (b, c, 0)),
                   st(dh, dh), st(1, dh), st(1, 1)],
        out_shape=[jax.ShapeDtypeStruct((bsz, t, B_WIDTH), BF16),
                   jax.ShapeDtypeStruct((bsz, B_HEADS, dh, dh), F32),
                   jax.ShapeDtypeStruct((bsz, B_HEADS, 1, dh), F32),
                   jax.ShapeDtypeStruct((bsz, B_HEADS, 1, 1), F32)],
        scratch_shapes=[pltpu.VMEM((B_HEADS, dh, dh), F32), pltpu.VMEM((B_HEADS, 1, dh), F32),
                        pltpu.VMEM((B_HEADS, 1, 1), F32)],
        compiler_params=_params("parallel", "arbitrary"),
        name="mlstm",
    )(zm, zm, zm, zm, zm, g3, gt, ng, c0, n0, m0)
    return h, c1, n1.reshape(bsz, B_HEADS, dh), m1.reshape(bsz, B_HEADS)


def _outln_kernel(*refs, n_parts):
    mix = refs[:n_parts]
    ws = refs[n_parts:2 * n_parts]
    x_ref, g_ref, b_ref, o_ref, ob_ref = refs[2 * n_parts:]
    tm = x_ref.shape[0]
    sub = min(tm, LANES)
    for r in range(tm // sub):
        rows = pl.ds(r * sub, sub)
        y = jnp.dot(mix[0][rows, :], ws[0][...], preferred_element_type=F32)
        for p in range(1, n_parts):
            y = y + jnp.dot(mix[p][rows, :], ws[p][...], preferred_element_type=F32)
        z = ALPHA * x_ref[rows, :] + y
        mu = jnp.mean(z, axis=1, keepdims=True)
        var = jnp.mean(jnp.square(z - mu), axis=1, keepdims=True)
        xn = (z - mu) * lax.rsqrt(var + LN_EPS) * g_ref[...] + b_ref[...]
        o_ref[rows, :] = xn
        ob_ref[rows, :] = xn.astype(BF16)


def _out_ln(mix_parts, w_parts, x, ln_g, ln_b, tm):
    m, d = x.shape
    n_parts = len(mix_parts)
    in_specs = ([pl.BlockSpec((tm, mp.shape[1]), lambda i: (i, 0)) for mp in mix_parts]
                + [pl.BlockSpec(wp.shape, lambda i: (0, 0)) for wp in w_parts]
                + [pl.BlockSpec((tm, d), lambda i: (i, 0)),
                   pl.BlockSpec((1, d), lambda i: (0, 0)),
                   pl.BlockSpec((1, d), lambda i: (0, 0))])
    return pl.pallas_call(
        functools.partial(_outln_kernel, n_parts=n_parts),
        grid=(m // tm,),
        in_specs=in_specs,
        out_specs=[pl.BlockSpec((tm, d), lambda i: (i, 0)), pl.BlockSpec((tm, d), lambda i: (i, 0))],
        out_shape=[jax.ShapeDtypeStruct((m, d), F32), jax.ShapeDtypeStruct((m, d), BF16)],
        compiler_params=_params("parallel"),
        name="outproj_ln",
    )(*mix_parts, *w_parts, x, ln_g.reshape(1, d).astype(F32), ln_b.reshape(1, d).astype(F32))


def _odd_kernel(*refs, tm, carry_mode, seq_tiles, seq_rows):
    if carry_mode:
        x_ref, wb, wc, wx, wg, cw_ref, mix_ref, tail_ref, wbf, carry = refs
    else:
        x_ref, wb, wc, wx, wg, cw_ref, plast_ref, pprev_ref, mix_ref, tail_ref, wbf = refs

    @pl.when(pl.program_id(1) == 0)
    def _():
        for gidx, wref in enumerate((wb, wc, wx, wg)):
            wbf[gidx] = wref[...].astype(BF16)

    x = x_ref[...]
    zb = jnp.dot(x, wbf[0], preferred_element_type=F32)
    zc = jnp.dot(x, wbf[1], preferred_element_type=F32)
    zx = jnp.dot(x, wbf[2], preferred_element_type=F32)
    zg = jnp.dot(x, wbf[3], preferred_element_type=F32)
    u = zc * zx
    rows = lax.broadcasted_iota(jnp.int32, u.shape, 0)
    r1 = pltpu.roll(u, 1, 0)
    r2 = pltpu.roll(u, 2, 0)
    if carry_mode:
        @pl.when(pl.program_id(1) % seq_tiles == 0)
        def _():
            carry[...] = jnp.zeros_like(carry)
        last = carry[SUBLANES - 1:SUBLANES, :]
        prev = carry[SUBLANES - 2:SUBLANES - 1, :]
        rm = rows
    else:
        last = plast_ref[...]
        prev = pprev_ref[...]
        rm = rows & (seq_rows - 1)
    u1 = jnp.where(rm == 0, last, r1)
    u2 = jnp.where(rm == 0, prev, jnp.where(rm == 1, last, r2))
    cw = cw_ref[...]
    y = cw[0:1, :] * u2 + cw[1:2, :] * u1 + cw[2:3, :] * u
    mix_ref[...] = (zb * y * _silu(zg)).astype(mix_ref.dtype)
    if carry_mode:
        tail = u[tm - SUBLANES:, :]
        carry[...] = tail
        tail_ref[...] = tail
    else:
        tail_ref[...] = u


def _odd_mix(x_bf, w_all, o, conv_w, tm, tn, seq_len, state=None):
    m, k = x_bf.shape
    nn = C_WIDTH // tn
    carry_mode = state is None
    wspec = lambda gidx: pl.BlockSpec((None, k, tn), lambda j, i: (o, 0, gidx * nn + j))
    in_specs = [pl.BlockSpec((tm, k), lambda j, i: (i, 0)),
                wspec(0), wspec(1), wspec(2), wspec(3),
                pl.BlockSpec((CONV_K, tn), lambda j, i: (0, j))]
    args = [x_bf, w_all, w_all, w_all, w_all, conv_w.astype(F32)]
    scratch = [pltpu.VMEM((4, k, tn), BF16)]
    if carry_mode:
        assert seq_len % tm == 0
        tail_shape = jax.ShapeDtypeStruct((m // tm, SUBLANES, C_WIDTH), F32)
        tail_spec = pl.BlockSpec((None, SUBLANES, tn), lambda j, i: (i, 0, j))
        scratch.append(pltpu.VMEM((SUBLANES, tn), F32))
    else:
        assert tm % seq_len == 0 and seq_len & (seq_len - 1) == 0
        in_specs += [pl.BlockSpec((tm, tn), lambda j, i: (i, j))] * 2
        args += list(state)
        tail_shape = jax.ShapeDtypeStruct((m, C_WIDTH), F32)
        tail_spec = pl.BlockSpec((tm, tn), lambda j, i: (i, j))
    kern = functools.partial(_odd_kernel, tm=tm, carry_mode=carry_mode,
                             seq_tiles=max(seq_len // tm, 1), seq_rows=seq_len)
    return pl.pallas_call(
        kern,
        grid=(nn, m // tm),
        in_specs=in_specs,
        out_specs=[pl.BlockSpec((tm, tn), lambda j, i: (i, j)), tail_spec],
        out_shape=[jax.ShapeDtypeStruct((m, C_WIDTH), BF16), tail_shape],
        scratch_shapes=scratch,
        compiler_params=_params("parallel", "arbitrary"),
        name="odd_mix",
    )(*args)


def _row_tiles(m):
    return min(m, 1024), min(m, 512)


def _project_even(x_bf, wt_all, e, bsz, t, sample=None):
    m = bsz * t
    tm, _ = _row_tiles(m)
    ng = N_GROUPS_EVEN - 2
    if sample is None:
        (zm,) = _inproj_even(x_bf, wt_all, e, (0, 2, ng), tm)
        zk, ka = _inproj_even(x_bf, wt_all, e, (1, 0, 1), tm, heads_out=True)
        attn_s = None
    else:
        n_seq = sample[0].shape[1]
        assert n_seq == ng + 1
        tm_a = tm // 2
        zm, attn_a = _inproj_even_with_attn(x_bf, wt_all, e, (0, 2, ng), tm_a, False, sample, 0)
        zk, ka, attn_b = _inproj_even_with_attn(x_bf, wt_all, e, (1, 0, 1), tm_a, True, sample, ng)
        attn_s = jnp.concatenate([attn_a, attn_b], axis=0)
    zv, va = _inproj_even(x_bf, wt_all, e, (2, 0, 1), tm, heads_out=True)
    return (zm.reshape(ng, bsz, t, 1024), zk.reshape(1, bsz, t, 1024), zv.reshape(1, bsz, t, 1024),
            ka.reshape(bsz, t, A_HEADS, A_HEAD_DIM), va.reshape(bsz, t, A_HEADS, A_HEAD_DIM), attn_s)


def _finish_even(x, x_bf, bsz, t, wt_all, e, zm, attn, gbias, w_out_a, w_out_b, norm_g, ln_g, ln_b,
                 state, L):
    tm, tm_out = _row_tiles(bsz * t)
    gates = _gates(x_bf, wt_all, e, gbias, tm)
    hm, c1, n1, m1 = _mlstm(zm, gates, norm_g, *state, bsz, t, L)
    xn, xn_bf = _out_ln([attn.reshape(bsz * t, A_WIDTH), hm.reshape(bsz * t, B_WIDTH)],
                        [w_out_a, w_out_b], x, ln_g, ln_b, tm_out)
    return xn, xn_bf, (c1, n1, m1)


def _even_layers(xp, xp_bf, bp, tp, xs, xs_bf, bs, ts, wt_all, e, gbias, w_out_a, w_out_b, norm_g,
                 ln_g, ln_b, bias_p, bias_s, cache_k, cache_v, page_table, state_s):
    zm_s, zk_s, zv_s, ka_s, va_s, _ = _project_even(xs_bf, wt_all, e, bs, ts)
    sample = (zm_s, zk_s, zv_s, bias_s, cache_k, cache_v, page_table, ts)
    zm_p, zk_p, zv_p, ka_p, va_p, attn_s = _project_even(xp_bf, wt_all, e, bp, tp, sample)
    attn_p = _moba_prompt(zm_p, zk_p, zv_p, bias_p, bp, tp)
    zeros = (jnp.zeros((bp, B_HEADS, B_HEAD_DIM, B_HEAD_DIM), F32),
             jnp.zeros((bp, B_HEADS, B_HEAD_DIM), F32), jnp.zeros((bp, B_HEADS), F32))
    xp, xp_bf, st_p = _finish_even(xp, xp_bf, bp, tp, wt_all, e, zm_p, attn_p, gbias, w_out_a, w_out_b,
                                   norm_g, ln_g, ln_b, zeros, math.gcd(tp, MLSTM_PROMPT_CHUNK))
    xs, xs_bf, st_s = _finish_even(xs, xs_bf, bs, ts, wt_all, e, zm_s, attn_s, gbias, w_out_a, w_out_b,
                                   norm_g, ln_g, ln_b, state_s, math.gcd(ts, MLSTM_CHUNK))
    return (xp, xp_bf, (ka_p, va_p) + st_p), (xs, xs_bf, (ka_s, va_s) + st_s)


def _odd_layer(x, x_bf, bsz, t, w_in_all, o, w_out, conv_w, ln_g, ln_b, buf):
    tm, tm_out = _row_tiles(bsz * t)
    if buf is None:
        tm = math.gcd(2 * tm, t)
        mix, tail = _odd_mix(x_bf, w_in_all, o, conv_w, tm, 256, t)
        seq_tiles = t // tm
        new_buf = tail.reshape(bsz, seq_tiles, SUBLANES, C_WIDTH)[:, -1, SUBLANES - (CONV_K - 1):, :]
    else:
        plast = jnp.repeat(buf[:, 1, :], t, axis=0)
        pprev = jnp.repeat(buf[:, 0, :], t, axis=0)
        mix, u = _odd_mix(x_bf, w_in_all, o, conv_w, tm, 256, t, state=(plast, pprev))
        new_buf = u.reshape(bsz, t, C_WIDTH)[:, t - (CONV_K - 1):, :]
    xn, xn_bf = _out_ln([mix], [w_out], x, ln_g, ln_b, tm_out)
    return xn, xn_bf, new_buf


def kernel(x_prompt, x_sample, cache_k, cache_v, page_table, state_C, state_n, state_m, state_conv,
           w_in_even, w_out_even, mlstm_gate_bias, mlstm_norm_g, rel_bias,
           w_in_odd, w_out_odd, conv_w, ln_g, ln_b):
    bp, tp, d = x_prompt.shape
    bs, ts, _ = x_sample.shape
    bias_p = _bias_tiles(rel_bias, _prompt_bias_idx())
    bias_s = _bias_tiles(rel_bias, _sample_bias_idx(ts)).reshape(A_HEADS * ts, -1)

    xp = x_prompt.reshape(bp * tp, d)
    xs = x_sample.reshape(bs * ts, d)
    xp_bf = xp.astype(BF16)
    xs_bf = xs.astype(BF16)
    wt_even = jnp.swapaxes(w_in_even, 1, 2)

    pk, pv, pc, pn, pm, pb = [], [], [], [], [], []
    sk, sv, sc, sn, sm, sb = [], [], [], [], [], []
    for layer in range(DEPTH):
        if layer % 2 == 0:
            e = layer // 2
            gbias = jnp.pad(mlstm_gate_bias[e], (0, LANES - 2 * B_HEADS)).reshape(1, LANES).astype(F32)
            w_out_a = w_out_even[e, :A_WIDTH].astype(BF16)
            w_out_b = w_out_even[e, A_WIDTH:].astype(BF16)
            (xp, xp_bf, (k1, v1, c1, n1, m1)), (xs, xs_bf, (k2, v2, c2, n2, m2)) = _even_layers(
                xp, xp_bf, bp, tp, xs, xs_bf, bs, ts, wt_even, e, gbias, w_out_a, w_out_b,
                mlstm_norm_g[e], ln_g[layer], ln_b[layer], bias_p, bias_s, cache_k, cache_v, page_table,
                (state_C[e], state_n[e], state_m[e]))
            pk.append(k1); pv.append(v1); pc.append(c1); pn.append(n1); pm.append(m1)
            sk.append(k2); sv.append(v2); sc.append(c2); sn.append(n2); sm.append(m2)
        else:
            o = layer // 2
            w_out = w_out_odd[o].astype(BF16)
            xp, xp_bf, b1 = _odd_layer(xp, xp_bf, bp, tp, w_in_odd, o, w_out, conv_w[o],
                                       ln_g[layer], ln_b[layer], None)
            xs, xs_bf, b2 = _odd_layer(xs, xs_bf, bs, ts, w_in_odd, o, w_out, conv_w[o],
                                       ln_g[layer], ln_b[layer], state_conv[o])
            pb.append(b1); sb.append(b2)
    return (xp.reshape(bp, tp, d), xs.reshape(bs, ts, d),
            jnp.stack(pk), jnp.stack(pv), jnp.stack(pc), jnp.stack(pn), jnp.stack(pm), jnp.stack(pb),
            jnp.stack(sk), jnp.stack(sv), jnp.stack(sc), jnp.stack(sn), jnp.stack(sm), jnp.stack(sb))
```

```python
import functools
import math

import numpy as np
import jax
import jax.numpy as jnp
from jax import lax
from jax.experimental import pallas as pl
from jax.experimental.pallas import tpu as pltpu

F32 = jnp.float32
BF16 = jnp.bfloat16

D_MODEL = 2048
DEPTH = 4
PAGE_SIZE = 128
A_HEADS = 8
A_HEAD_DIM = 128
A_WIDTH = A_HEADS * A_HEAD_DIM
MOBA_BLOCK = 256
MOBA_TOPK = 3
REL_BUCKETS = 32
REL_MAX_DIST = 128
B_HEADS = 4
B_HEAD_DIM = 256
B_WIDTH = B_HEADS * B_HEAD_DIM
MLSTM_CHUNK = 64
MLSTM_PROMPT_CHUNK = 256
C_WIDTH = D_MODEL
CONV_K = 3
N_GROUPS_EVEN = 9
EVEN_MAIN = N_GROUPS_EVEN * 1024
ALPHA = (2.0 * DEPTH) ** 0.25
LN_EPS = 1e-5

LANES = 128
SUBLANES = 8
VMEM_LIMIT = 56 * 1024 * 1024

NEG = -1e30


def _params(*sem):
    return pltpu.CompilerParams(dimension_semantics=sem, vmem_limit_bytes=VMEM_LIMIT)


def _nt(a, b):
    return lax.dot_general(a, b, (((1,), (1,)), ((), ())), preferred_element_type=F32)


def _silu(x):
    return x * jax.nn.sigmoid(x)


def _bucket_np(d):
    d = np.asarray(d, np.int64)
    n = np.maximum(d, 0)
    max_exact = REL_BUCKETS // 2
    nf = np.maximum(n, 1).astype(np.float64)
    large = max_exact + (np.log(nf / max_exact) / math.log(REL_MAX_DIST / max_exact)
                         * (REL_BUCKETS - max_exact)).astype(np.int64)
    large = np.minimum(large, REL_BUCKETS - 1)
    out = np.where(n < max_exact, n, large)
    return np.where(d < 0, -1, out).astype(np.int32)


def _bias_kernel(tab_ref, idx_ref, o_ref):
    h = pl.program_id(0)
    idx = idx_ref[...]
    out = jnp.full(idx.shape, NEG, F32)
    for b in range(REL_BUCKETS):
        out = jnp.where(idx == b, tab_ref[h, b], out)
    o_ref[...] = out


def _bias_tiles(rel_bias, idx):
    r, c = idx.shape
    return pl.pallas_call(
        _bias_kernel,
        grid=(A_HEADS,),
        in_specs=[pl.BlockSpec(memory_space=pltpu.SMEM),
                  pl.BlockSpec((r, c), lambda h: (0, 0))],
        out_specs=pl.BlockSpec((None, r, c), lambda h: (h, 0, 0)),
        out_shape=jax.ShapeDtypeStruct((A_HEADS, r, c), F32),
        compiler_params=_params("parallel"),
        name="bias_tiles",
    )(rel_bias.astype(F32), jnp.asarray(idx))


def _prompt_bias_idx():
    qi = np.arange(MOBA_BLOCK)[:, None]
    kj = np.arange(2 * MOBA_BLOCK)[None, :]
    return _bucket_np(qi + MOBA_BLOCK - kj)


def _sample_bias_idx(s_len):
    s = np.arange(s_len)[:, None]
    lane = np.arange(2 * PAGE_SIZE * A_HEADS)[None, :]
    off = lane // A_HEADS
    far = _bucket_np(MOBA_BLOCK + s - off)
    lo = np.arange(LANES)[None, :]
    own = _bucket_np(np.where(lo < A_HEADS * s_len, s - (lo % s_len), -1))
    return np.concatenate([far, own], axis=1)


def _inproj_prep(w_ref, wb_ref):
    @pl.when(pl.program_id(1) == 0)
    def _():
        wb_ref[...] = w_ref[...].astype(BF16)


def _inproj_compute(x_ref, o_ref, oh_ref, wb_ref, part=0, n_parts=1):
    tm = x_ref.shape[0]
    nf = wb_ref.shape[0] // n_parts
    cols = pl.ds(part * nf, nf)
    y = _nt(x_ref[...], wb_ref[cols, :])
    o_ref[:, cols] = y.astype(o_ref.dtype)
    if oh_ref is not None:
        nh = nf // A_HEAD_DIM
        oh_ref[:, pl.ds(part * nh, nh), :] = y.reshape(tm, nh, A_HEAD_DIM)


def _inproj_kernel(x_ref, w_ref, *rest, heads_out, aliased):
    if aliased:
        rest = rest[1:]
    if heads_out:
        o_ref, oh_ref, wb_ref = rest
    else:
        (o_ref, wb_ref), oh_ref = rest, None
    _inproj_prep(w_ref, wb_ref)
    _inproj_compute(x_ref, o_ref, oh_ref, wb_ref)


def _heads_out_spec(heads, e, m, tm, index_map):
    n_layers, stack = heads
    spec = pl.BlockSpec((None, tm, A_HEADS, A_HEAD_DIM), index_map)
    shape = jax.ShapeDtypeStruct((n_layers, m, A_HEADS, A_HEAD_DIM), F32)
    return spec, shape, stack


def _inproj_even(x_bf, wt_all, e, groups, tm, heads=None):
    m, k = x_bf.shape
    gw = 1024
    first, skip, n_groups = groups
    gmap = lambda j: first + j + skip * jnp.minimum(j, 1)
    out_specs = [pl.BlockSpec((None, tm, gw), lambda j, i: (j, i, 0))]
    out_shape = [jax.ShapeDtypeStruct((n_groups, m, gw), BF16)]
    in_specs = [pl.BlockSpec((tm, k), lambda j, i: (i, 0)),
                pl.BlockSpec((None, gw, k), lambda j, i: (e, gmap(j), 0))]
    args = [x_bf, wt_all]
    aliases = {}
    if heads is not None:
        assert n_groups == 1
        spec, shape, stack = _heads_out_spec(heads, e, m, tm, lambda j, i: (e, i, 0, 0))
        out_specs.append(spec)
        out_shape.append(shape)
        if stack is not None:
            in_specs.append(pl.BlockSpec(memory_space=pl.ANY))
            args.append(stack)
            aliases = {len(args) - 1: 1}
    return pl.pallas_call(
        functools.partial(_inproj_kernel, heads_out=heads is not None, aliased=bool(aliases)),
        grid=(n_groups, m // tm),
        in_specs=in_specs,
        out_specs=out_specs,
        out_shape=out_shape,
        scratch_shapes=[pltpu.VMEM((gw, k), BF16)],
        input_output_aliases=aliases,
        compiler_params=_params("parallel", "arbitrary"),
        name="inproj_even",
    )(*args)


def _gates_kernel(x_ref, w_ref, b_ref, o_ref, *xb_ref):
    w = w_ref[...]
    wp = jnp.concatenate([w, jnp.zeros((LANES - w.shape[0], w.shape[1]), F32)], axis=0)
    xb = x_ref[...].astype(BF16)
    o_ref[...] = _nt(xb, wp.astype(BF16)) + b_ref[...]
    if xb_ref:
        xb_ref[0][...] = xb


def _gates(x, wt_all, e, gbias, tm):
    m, k = x.shape
    ng = 2 * B_HEADS
    assert ng == SUBLANES and EVEN_MAIN % ng == 0
    out_specs = [pl.BlockSpec((tm, LANES), lambda i: (i, 0))]
    out_shape = [jax.ShapeDtypeStruct((m, LANES), F32)]
    if x.dtype != BF16:
        out_specs.append(pl.BlockSpec((tm, k), lambda i: (i, 0)))
        out_shape.append(jax.ShapeDtypeStruct((m, k), BF16))
    outs = pl.pallas_call(
        _gates_kernel,
        grid=(m // tm,),
        in_specs=[pl.BlockSpec((tm, k), lambda i: (i, 0)),
                  pl.BlockSpec((None, ng, k), lambda i: (e, EVEN_MAIN // ng, 0)),
                  pl.BlockSpec((1, LANES), lambda i: (0, 0))],
        out_specs=out_specs,
        out_shape=out_shape,
        compiler_params=_params("parallel"),
        name="mlstm_gates",
    )(x, wt_all, gbias)
    return outs[0], (outs[1] if len(outs) > 1 else x)


def _moba_prompt_kernel(q_ref, k_ref, v_ref, g_ref, bias_ref, o_ref, kmean_ref, negm_ref,
                        *, nblk, scale):
    blk = MOBA_BLOCK
    for j in range(nblk):
        kmean_ref[j:j + 1, :] = jnp.mean(k_ref[j * blk:(j + 1) * blk, :].astype(F32), axis=0, keepdims=True)
    far_bias = bias_ref[0:1, 0:1]
    eye = (lax.broadcasted_iota(jnp.int32, (blk, blk), 0)
           == lax.broadcasted_iota(jnp.int32, (blk, blk), 1)).astype(BF16)
    rowi = lax.broadcasted_iota(jnp.int32, (nblk, blk), 0)

    i0 = MOBA_TOPK + 1
    if i0 < nblk:
        nq = (nblk - i0) * blk
        gt = lax.dot_general(kmean_ref[...], q_ref[i0 * blk:, :].astype(F32), (((1,), (1,)), ((), ())),
                             precision=lax.Precision.HIGHEST, preferred_element_type=F32)
        rowq = lax.broadcasted_iota(jnp.int32, (nblk, nq), 0)
        own = i0 + (lax.broadcasted_iota(jnp.int32, (nblk, nq), 1) >> (blk.bit_length() - 1))
        valid = rowq < own
        sel_t = jnp.zeros((nblk, nq), F32)
        for j in range(nblk - 1):
            gj = gt[j:j + 1, :]
            beats = ((gt > gj) | ((gt == gj) & (rowq < j))) & valid
            rank = jnp.sum(beats.astype(F32), axis=0, keepdims=True)
            sel_t = jnp.where(rowq == j, (rank < MOBA_TOPK).astype(F32), sel_t)
        sel_t = jnp.concatenate([sel_t, jnp.zeros((LANES - nblk, nq), F32)], axis=0).astype(BF16)
        for i in range(i0, nblk):
            cs = (i - i0) * blk
            sel = _nt(eye, sel_t[:, cs:cs + blk])
            negm_ref[i - i0] = (sel - 1.0) * (-NEG)

    for i in range(nblk):
        q = q_ref[i * blk:(i + 1) * blk, :].astype(F32)
        negm = negm_ref[i - MOBA_TOPK - 1] if i > MOBA_TOPK else None
        w = (i + 1) * blk
        s = _nt((q * scale).astype(BF16), k_ref[0:w, :])
        pieces = []
        for j in range(i + 1):
            sj = s[:, j * blk:(j + 1) * blk]
            if j == i:
                sj = sj + bias_ref[:, blk:2 * blk]
            elif j == i - 1:
                sj = sj + bias_ref[:, 0:blk]
                if negm is not None:
                    sj = sj + negm[:, j:j + 1]
            else:
                sj = sj + (far_bias if negm is None else far_bias + negm[:, j:j + 1])
            pieces.append(sj)
        m = jnp.max(functools.reduce(jnp.maximum, pieces), axis=1, keepdims=True)
        ps = [jnp.exp(sj - m) for sj in pieces]
        l = jnp.sum(functools.reduce(jnp.add, ps), axis=1, keepdims=True)
        pcat = ps[0].astype(BF16) if i == 0 else jnp.concatenate([p.astype(BF16) for p in ps], axis=1)
        acc = jnp.dot(pcat, v_ref[0:w, :], preferred_element_type=F32)
        g = g_ref[i * blk:(i + 1) * blk, :].astype(F32)
        o_ref[i * blk:(i + 1) * blk, :] = (acc / l * _silu(g)).astype(o_ref.dtype)


def _moba_prompt(zm, zk, zv, bias_p, bsz, t):
    blk = MOBA_BLOCK
    nblk = t // blk
    assert nblk <= SUBLANES
    dh = A_HEAD_DIM
    kern = functools.partial(_moba_prompt_kernel, nblk=nblk, scale=dh ** -0.5)
    zspec = lambda g: pl.BlockSpec((None, None, t, dh), lambda b, h: (g, b, 0, h))
    return pl.pallas_call(
        kern,
        grid=(bsz, A_HEADS),
        in_specs=[zspec(0), zspec(0), zspec(0), zspec(1),
                  pl.BlockSpec((None, blk, 2 * blk), lambda b, h: (h, 0, 0))],
        out_specs=pl.BlockSpec((None, t, dh), lambda b, h: (b, 0, h)),
        out_shape=jax.ShapeDtypeStruct((bsz, t, A_WIDTH), BF16),
        scratch_shapes=[pltpu.VMEM((nblk, dh), F32),
                        pltpu.VMEM((max(nblk - MOBA_TOPK - 1, 1), blk, LANES), F32)],
        compiler_params=_params("parallel", "parallel"),
        name="moba_prompt",
    )(zm, zk, zv, zm, bias_p)


def _stack_heads(x):
    return jnp.concatenate([x[:, h * A_HEAD_DIM:(h + 1) * A_HEAD_DIM] for h in range(A_HEADS)], axis=0)


def _moba_sample_body(q_ref, kn_ref, vn_ref, ga_ref, bias_ref, k_refs, v_refs, o_ref,
                      qm_ref, gate_ref, m_ref, l_ref, oblk_ref, *, nb, bps, s_len, scale,
                      alongside=None):
    n = pl.program_id(1)
    rows = A_HEADS * s_len
    s_shift = s_len.bit_length() - 1
    far_w = PAGE_SIZE * A_HEADS

    @pl.when(n == 0)
    def _():
        qm_ref[...] = _stack_heads(q_ref[...].astype(F32))
        gate_ref[...] = jnp.full_like(gate_ref, NEG)
        m_ref[...] = jnp.full_like(m_ref, NEG)
        l_ref[...] = jnp.zeros_like(l_ref)

    qb = qm_ref[...].astype(BF16)
    row = lax.broadcasted_iota(jnp.int32, (rows, far_w), 0)
    lane = lax.broadcasted_iota(jnp.int32, (rows, far_w), 1)
    useful = (lane & (A_HEADS - 1)) == (row >> s_shift)
    lane_b = lax.broadcasted_iota(jnp.int32, (rows, LANES), 1)
    last = n == pl.num_programs(1) - 1
    far_bias = bias_ref[:, 0:1]

    def pages(ref):
        return ref[...].reshape(far_w, A_HEAD_DIM).astype(BF16)

    gate_new = gate_ref[...]
    m_new = m_ref[...]
    l_new = l_ref[...]
    for c in range(bps):
        blk = n * bps + c
        raw0 = _nt(qb, pages(k_refs[2 * c]))
        raw1 = _nt(qb, pages(k_refs[2 * c + 1]))
        if alongside is not None:
            alongside(c, bps)
        gsum = jnp.sum(jnp.where(useful, raw0 + raw1, 0.0), axis=1, keepdims=True)
        if c == bps - 1:
            last_v = (jnp.zeros((rows, far_w), jnp.int32) + blk) == nb - 1
            b0 = jnp.where(last_v, bias_ref[:, 0:far_w], far_bias)
            b1 = jnp.where(last_v, bias_ref[:, far_w:2 * far_w], far_bias)
        else:
            b0 = b1 = far_bias
        s0 = jnp.where(useful, raw0 * scale + b0, NEG)
        s1 = jnp.where(useful, raw1 * scale + b1, NEG)
        mloc = jnp.max(jnp.maximum(s0, s1), axis=1, keepdims=True)
        p0 = jnp.exp(s0 - mloc)
        p1 = jnp.exp(s1 - mloc)
        lloc = jnp.sum(p0 + p1, axis=1, keepdims=True)
        oblk_ref[blk] = (jnp.dot(p0.astype(BF16), pages(v_refs[2 * c]), preferred_element_type=F32)
                         + jnp.dot(p1.astype(BF16), pages(v_refs[2 * c + 1]), preferred_element_type=F32))
        here = lane_b == blk
        gate_new = jnp.where(here, gsum * (1.0 / MOBA_BLOCK), gate_new)
        m_new = jnp.where(here, mloc, m_new)
        l_new = jnp.where(here, lloc, l_new)
    gate_ref[...] = gate_new
    m_ref[...] = m_new
    l_ref[...] = l_new

    @pl.when(last)
    def _():
        row_b = lax.broadcasted_iota(jnp.int32, (rows, LANES), 0)
        lane_f = lane_b.astype(F32)
        g = jnp.where(lane_b < nb, gate_ref[...], -jnp.inf)
        sel = jnp.zeros((rows, LANES), jnp.bool_)
        for _r in range(min(MOBA_TOPK, nb)):
            mx = jnp.max(g, axis=1, keepdims=True)
            idx = jnp.min(jnp.where(g == mx, lane_f, float(LANES)), axis=1, keepdims=True)
            hit = lane_f == idx
            sel = sel | hit
            g = jnp.where(hit, -jnp.inf, g)
        pad = jnp.zeros((LANES - rows, A_HEAD_DIM), F32)
        kn = jnp.concatenate([_stack_heads(kn_ref[...].astype(F32)), pad], axis=0).astype(BF16)
        vn = jnp.concatenate([_stack_heads(vn_ref[...].astype(F32)), pad], axis=0).astype(BF16)
        so = _nt(qb, kn) * scale + bias_ref[:, 2 * far_w:2 * far_w + LANES]
        ok = ((lane_b < rows) & ((lane_b >> s_shift) == (row_b >> s_shift))
              & ((lane_b & (s_len - 1)) <= (row_b & (s_len - 1))))
        so = jnp.where(ok, so, NEG)
        mm = m_ref[...]
        mtot = jnp.maximum(jnp.max(jnp.where(sel, mm, NEG), axis=1, keepdims=True),
                           jnp.max(so, axis=1, keepdims=True))
        w = jnp.where(sel, jnp.exp(mm - mtot), 0.0)
        po = jnp.exp(so - mtot)
        lsum = jnp.sum(w * l_ref[...], axis=1, keepdims=True) + jnp.sum(po, axis=1, keepdims=True)
        acc = jnp.dot(po.astype(BF16), vn, preferred_element_type=F32)
        for nn in range(nb):
            acc = acc + w[:, nn:nn + 1] * oblk_ref[nn]
        out = acc / lsum
        res = jnp.concatenate([out[h * s_len:(h + 1) * s_len, :] for h in range(A_HEADS)], axis=1)
        o_ref[...] = (res * _silu(ga_ref[...].astype(F32))).astype(o_ref.dtype)


def _inproj_attn_kernel(pt_ref, x_ref, w_ref, q_ref, kn_ref, vn_ref, ga_ref, bias_ref, *rest,
                        heads_out, aliased, nb, bps, s_len, scale):
    del pt_ref
    npg = 2 * bps
    k_refs = rest[:npg]
    v_refs = rest[npg:2 * npg]
    rest = rest[2 * npg + (1 if aliased else 0):]
    if heads_out:
        o_ref, oh_ref, ao_ref, wb_ref = rest[:4]
    else:
        (o_ref, ao_ref, wb_ref), oh_ref = rest[:3], None
    _inproj_prep(w_ref, wb_ref)
    _moba_sample_body(q_ref, kn_ref, vn_ref, ga_ref, bias_ref, k_refs, v_refs, ao_ref, *rest[-5:],
                      nb=nb, bps=bps, s_len=s_len, scale=scale,
                      alongside=functools.partial(_inproj_compute, x_ref, o_ref, oh_ref, wb_ref))


def _inproj_even_with_attn(x_bf, wt_all, e, groups, tm, heads, sample, b0):
    zm, zk, zv, bias_s, cache_k, cache_v, page_table, s_len = sample
    m, k = x_bf.shape
    gw = 1024
    first, skip, n_groups = groups
    nsteps = m // tm
    n_pages = page_table.shape[1]
    ppb = MOBA_BLOCK // PAGE_SIZE
    assert ppb == 2 and n_pages % ppb == 0 and s_len == SUBLANES and A_HEADS == SUBLANES
    nb = n_pages // ppb
    assert nb <= LANES and nb % nsteps == 0
    bps = nb // nsteps
    rows = A_HEADS * s_len
    dh = A_HEAD_DIM
    gmap = lambda j: first + j + skip * jnp.minimum(j, 1)
    tok = lambda g: pl.BlockSpec((None, None, s_len, A_WIDTH), lambda j, i, pt: (g, b0 + j, 0, 0))
    page = lambda o: pl.BlockSpec((None, None, PAGE_SIZE, A_HEADS, dh),
                                  lambda j, i, pt: (e, pt[b0 + j, ppb * bps * i + o], 0, 0, 0))
    pages = [page(o) for o in range(ppb * bps)]
    out_specs = [pl.BlockSpec((None, tm, gw), lambda j, i, pt: (j, i, 0))]
    out_shape = [jax.ShapeDtypeStruct((n_groups, m, gw), BF16)]
    args = [page_table, x_bf, wt_all, zm, zk, zv, zm, bias_s,
            *([cache_k] * (ppb * bps)), *([cache_v] * (ppb * bps))]
    extra_specs = []
    aliases = {}
    if heads is not None:
        assert n_groups == 1
        spec, shape, stack = _heads_out_spec(heads, e, m, tm, lambda j, i, pt: (e, i, 0, 0))
        out_specs.append(spec)
        out_shape.append(shape)
        if stack is not None:
            extra_specs.append(pl.BlockSpec(memory_space=pl.ANY))
            args.append(stack)
            aliases = {len(args) - 1: 1}
    out_specs.append(pl.BlockSpec((None, s_len, A_WIDTH), lambda j, i, pt: (j, 0, 0)))
    out_shape.append(jax.ShapeDtypeStruct((n_groups, s_len, A_WIDTH), BF16))
    grid_spec = pltpu.PrefetchScalarGridSpec(
        num_scalar_prefetch=1,
        grid=(n_groups, nsteps),
        in_specs=[pl.BlockSpec((tm, k), lambda j, i, pt: (i, 0)),
                  pl.BlockSpec((None, gw, k), lambda j, i, pt: (e, gmap(j), 0)),
                  tok(0), tok(0), tok(0), tok(1),
                  pl.BlockSpec(bias_s.shape, lambda j, i, pt: (0, 0))] + pages + pages + extra_specs,
        out_specs=out_specs,
        scratch_shapes=[pltpu.VMEM((gw, k), BF16),
                        pltpu.VMEM((rows, dh), F32),
                        pltpu.VMEM((rows, LANES), F32),
                        pltpu.VMEM((rows, LANES), F32),
                        pltpu.VMEM((rows, LANES), F32),
                        pltpu.VMEM((nb, rows, dh), F32)])
    kern = functools.partial(_inproj_attn_kernel, heads_out=heads is not None, aliased=bool(aliases),
                             nb=nb, bps=bps, s_len=s_len, scale=dh ** -0.5)
    return pl.pallas_call(
        kern,
        grid_spec=grid_spec,
        out_shape=out_shape,
        input_output_aliases=aliases,
        compiler_params=_params("arbitrary", "arbitrary"),
        name="inproj_even_sample_attn",
    )(*args)


def _mlstm_kernel(q_ref, k_ref, v_ref, og_ref, gm_ref, g_ref, gt_ref, ng_ref, c0_ref, n0_ref, m0_ref,
                  h_ref, c_out, n_out, m_out, c_scr, n_scr, m_scr, *, L, Lp):
    c = pl.program_id(1)
    dk = B_HEAD_DIM

    @pl.when(c == 0)
    def _():
        c_scr[...] = c0_ref[...]
        n_scr[...] = n0_ref[...]
        m_scr[...] = m0_ref[...]

    def padded(x):
        if L == Lp:
            return x
        return jnp.concatenate([x, jnp.zeros((Lp - L, x.shape[1]), x.dtype)], axis=0)

    g = padded(g_ref[...])
    gt = gt_ref[...]
    r = lax.broadcasted_iota(jnp.int32, (Lp, Lp), 0)
    cc = lax.broadcasted_iota(jnp.int32, (Lp, Lp), 1)
    tri = r >= cc
    tri_t = r <= cc
    rv = lax.broadcasted_iota(jnp.int32, (Lp, 1), 0) < L
    cv = lax.broadcasted_iota(jnp.int32, (1, Lp), 1) < L

    for hh in range(B_HEADS):
        cols = slice(hh * dk, (hh + 1) * dk)
        q = padded(q_ref[:, cols].astype(F32))
        k = padded(k_ref[:, cols].astype(F32)) * (dk ** -0.5)
        v = padded(v_ref[:, cols].astype(F32))
        li_col = g[:, hh:hh + 1]
        li_row = gt[hh:hh + 1, :]
        lf_col = jax.nn.log_sigmoid(g[:, hh + B_HEADS:hh + B_HEADS + 1])
        lf_row = jax.nn.log_sigmoid(gt[hh + B_HEADS:hh + B_HEADS + 1, :])
        if L != Lp:
            li_col = jnp.where(rv, li_col, NEG)
            lf_col = jnp.where(rv, lf_col, 0.0)
            li_row = jnp.where(cv, li_row, NEG)
            lf_row = jnp.where(cv, lf_row, 0.0)

        b_col = jnp.sum(jnp.where(tri, lf_row, 0.0), axis=1, keepdims=True)
        b_row = jnp.sum(jnp.where(tri_t, lf_col, 0.0), axis=0, keepdims=True)
        m_prev = m_scr[hh]
        log_d = jnp.where(tri, b_col - b_row + li_row, NEG)
        m_inter = b_col + m_prev
        m_t = jnp.maximum(m_inter, jnp.max(log_d, axis=1, keepdims=True))
        qb = q.astype(BF16)
        kb = k.astype(BF16)
        vb = v.astype(BF16)
        s = _nt(qb, kb) * jnp.exp(log_d - m_t)
        w_inter = jnp.exp(m_inter - m_t)
        c_old = c_scr[hh]
        n_old = n_scr[hh]
        num = (jnp.dot(s.astype(BF16), vb, preferred_element_type=F32)
               + w_inter * jnp.dot(qb, c_old.astype(BF16), preferred_element_type=F32))
        den = jnp.sum(s, axis=1, keepdims=True) + w_inter * jnp.sum(q * n_old, axis=1, keepdims=True)
        h = num / jnp.maximum(jnp.abs(den), jnp.exp(-m_t))

        b_last = b_row[:, L - 1:L]
        m_new = jnp.maximum(b_last + m_prev, jnp.max(b_last - b_row + li_row, axis=1, keepdims=True))
        decay = jnp.exp(b_last + m_prev - m_new)
        wk = jnp.exp(b_last - b_col + li_col - m_new)
        kw = k * wk
        c_scr[hh] = decay * c_old + lax.dot_general(kw.astype(BF16), vb, (((0,), (0,)), ((), ())),
                                                    preferred_element_type=F32)
        n_scr[hh] = decay * n_old + jnp.sum(kw, axis=0, keepdims=True)
        m_scr[hh] = m_new

        hv = h[:L]
        mu = jnp.mean(hv, axis=1, keepdims=True)
        var = jnp.mean(jnp.square(hv - mu), axis=1, keepdims=True)
        hn = (hv - mu) * lax.rsqrt(var + LN_EPS) * ng_ref[hh]
        h_ref[:, cols] = (hn * jax.nn.sigmoid(og_ref[:, cols].astype(F32))
                          * _silu(gm_ref[:, cols].astype(F32))).astype(h_ref.dtype)

    @pl.when(c == pl.num_programs(1) - 1)
    def _():
        c_out[...] = c_scr[...]
        n_out[...] = n_scr[...]
        m_out[...] = m_scr[...]


def _mlstm(zm, gates, norm_g, c0, n0, m0, bsz, t, L):
    Lp = max(L, LANES)
    nc = t // L
    dh = B_HEAD_DIM
    g3 = gates.reshape(bsz, t, LANES)
    gt = jnp.swapaxes(gates[:, :2 * B_HEADS].reshape(bsz, nc, L, 2 * B_HEADS), 2, 3)
    if Lp != L:
        gt = jnp.pad(gt, ((0, 0), (0, 0), (0, 0), (0, Lp - L)))
    ng = norm_g.reshape(B_HEADS, 1, dh).astype(F32)
    n0 = n0.reshape(bsz, B_HEADS, 1, dh)
    m0 = m0.reshape(bsz, B_HEADS, 1, 1)
    kern = functools.partial(_mlstm_kernel, L=L, Lp=Lp)
    zspec = lambda gidx: pl.BlockSpec((None, None, L, B_WIDTH), lambda b, c: (gidx, b, c, 0))
    st = lambda r, cdim: pl.BlockSpec((None, B_HEADS, r, cdim), lambda b, c: (b, 0, 0, 0))
    h, c1, n1, m1 = pl.pallas_call(
        kern,
        grid=(bsz, nc),
        in_specs=[zspec(2), zspec(3), zspec(4), zspec(5), zspec(6),
                  pl.BlockSpec((None, L, LANES), lambda b, c: (b, c, 0)),
                  pl.BlockSpec((None, None, 2 * B_HEADS, Lp), lambda b, c: (b, c, 0, 0)),
                  pl.BlockSpec((B_HEADS, 1, dh), lambda b, c: (0, 0, 0)),
                  st(dh, dh), st(1, dh), st(1, 1)],
        out_specs=[pl.BlockSpec((None, L, B_WIDTH), lambda b, c: (b, c, 0)),
                   st(dh, dh), st(1, dh), st(1, 1)],
        out_shape=[jax.ShapeDtypeStruct((bsz, t, B_WIDTH), BF16),
                   jax.ShapeDtypeStruct((bsz, B_HEADS, dh, dh), F32),
                   jax.ShapeDtypeStruct((bsz, B_HEADS, 1, dh), F32),
                   jax.ShapeDtypeStruct((bsz, B_HEADS, 1, 1), F32)],
        scratch_shapes=[pltpu.VMEM((B_HEADS, dh, dh), F32), pltpu.VMEM((B_HEADS, 1, dh), F32),
                        pltpu.VMEM((B_HEADS, 1, 1), F32)],
        compiler_params=_params("parallel", "arbitrary"),
        name="mlstm",
    )(zm, zm, zm, zm, zm, g3, gt, ng, c0, n0, m0)
    return h, c1, n1.reshape(bsz, B_HEADS, dh), m1.reshape(bsz, B_HEADS)


def _outln_kernel(*refs, n_parts):
    mix = refs[:n_parts]
    ws = refs[n_parts:2 * n_parts]
    x_ref, g_ref, b_ref, o_ref, ob_ref = refs[2 * n_parts:]
    tm = x_ref.shape[0]
    sub = min(tm, LANES)
    for r in range(tm // sub):
        rows = pl.ds(r * sub, sub)
        y = jnp.dot(mix[0][rows, :], ws[0][...], preferred_element_type=F32)
        for p in range(1, n_parts):
            y = y + jnp.dot(mix[p][rows, :], ws[p][...], preferred_element_type=F32)
        z = ALPHA * x_ref[rows, :] + y
        mu = jnp.mean(z, axis=1, keepdims=True)
        var = jnp.mean(jnp.square(z - mu), axis=1, keepdims=True)
        xn = (z - mu) * lax.rsqrt(var + LN_EPS) * g_ref[...] + b_ref[...]
        o_ref[rows, :] = xn
        ob_ref[rows, :] = xn.astype(BF16)


def _out_ln(mix_parts, w_parts, x, ln_g, ln_b, tm):
    m, d = x.shape
    n_parts = len(mix_parts)
    in_specs = ([pl.BlockSpec((tm, mp.shape[1]), lambda i: (i, 0)) for mp in mix_parts]
                + [pl.BlockSpec(wp.shape, lambda i: (0, 0)) for wp in w_parts]
                + [pl.BlockSpec((tm, d), lambda i: (i, 0)),
                   pl.BlockSpec((1, d), lambda i: (0, 0)),
                   pl.BlockSpec((1, d), lambda i: (0, 0))])
    return pl.pallas_call(
        functools.partial(_outln_kernel, n_parts=n_parts),
        grid=(m // tm,),
        in_specs=in_specs,
        out_specs=[pl.BlockSpec((tm, d), lambda i: (i, 0)), pl.BlockSpec((tm, d), lambda i: (i, 0))],
        out_shape=[jax.ShapeDtypeStruct((m, d), F32), jax.ShapeDtypeStruct((m, d), BF16)],
        compiler_params=_params("parallel"),
        name="outproj_ln",
    )(*mix_parts, *w_parts, x, ln_g.reshape(1, d).astype(F32), ln_b.reshape(1, d).astype(F32))


def _odd_kernel(*refs, tm, carry_mode, seq_tiles, seq_rows):
    if carry_mode:
        x_ref, wb, wc, wx, wg, cw_ref, mix_ref, tail_ref, wbf, carry = refs
    else:
        x_ref, wb, wc, wx, wg, cw_ref, plast_ref, pprev_ref, mix_ref, tail_ref, wbf = refs

    @pl.when(pl.program_id(1) == 0)
    def _():
        for gidx, wref in enumerate((wb, wc, wx, wg)):
            wbf[gidx] = wref[...].astype(BF16)

    x = x_ref[...]
    zb = jnp.dot(x, wbf[0], preferred_element_type=F32)
    zc = jnp.dot(x, wbf[1], preferred_element_type=F32)
    zx = jnp.dot(x, wbf[2], preferred_element_type=F32)
    zg = jnp.dot(x, wbf[3], preferred_element_type=F32)
    u = zc * zx
    rows = lax.broadcasted_iota(jnp.int32, u.shape, 0)
    r1 = pltpu.roll(u, 1, 0)
    r2 = pltpu.roll(u, 2, 0)
    if carry_mode:
        @pl.when(pl.program_id(1) % seq_tiles == 0)
        def _():
            carry[...] = jnp.zeros_like(carry)
        last = carry[SUBLANES - 1:SUBLANES, :]
        prev = carry[SUBLANES - 2:SUBLANES - 1, :]
        rm = rows
    else:
        last = plast_ref[...]
        prev = pprev_ref[...]
        rm = rows & (seq_rows - 1)
    u1 = jnp.where(rm == 0, last, r1)
    u2 = jnp.where(rm == 0, prev, jnp.where(rm == 1, last, r2))
    cw = cw_ref[...]
    y = cw[0:1, :] * u2 + cw[1:2, :] * u1 + cw[2:3, :] * u
    mix_ref[...] = (zb * y * _silu(zg)).astype(mix_ref.dtype)
    if carry_mode:
        tail = u[tm - SUBLANES:, :]
        carry[...] = tail
        tail_ref[...] = tail
    else:
        tail_ref[...] = u


def _odd_mix(x_bf, w_all, o, conv_w, tm, tn, seq_len, state=None):
    m, k = x_bf.shape
    nn = C_WIDTH // tn
    carry_mode = state is None
    wspec = lambda gidx: pl.BlockSpec((None, k, tn), lambda j, i: (o, 0, gidx * nn + j))
    in_specs = [pl.BlockSpec((tm, k), lambda j, i: (i, 0)),
                wspec(0), wspec(1), wspec(2), wspec(3),
                pl.BlockSpec((CONV_K, tn), lambda j, i: (0, j))]
    args = [x_bf, w_all, w_all, w_all, w_all, conv_w.astype(F32)]
    scratch = [pltpu.VMEM((4, k, tn), BF16)]
    if carry_mode:
        assert seq_len % tm == 0
        tail_shape = jax.ShapeDtypeStruct((m // tm, SUBLANES, C_WIDTH), F32)
        tail_spec = pl.BlockSpec((None, SUBLANES, tn), lambda j, i: (i, 0, j))
        scratch.append(pltpu.VMEM((SUBLANES, tn), F32))
    else:
        assert tm % seq_len == 0 and seq_len & (seq_len - 1) == 0
        in_specs += [pl.BlockSpec((tm, tn), lambda j, i: (i, j))] * 2
        args += list(state)
        tail_shape = jax.ShapeDtypeStruct((m, C_WIDTH), F32)
        tail_spec = pl.BlockSpec((tm, tn), lambda j, i: (i, j))
    kern = functools.partial(_odd_kernel, tm=tm, carry_mode=carry_mode,
                             seq_tiles=max(seq_len // tm, 1), seq_rows=seq_len)
    return pl.pallas_call(
        kern,
        grid=(nn, m // tm),
        in_specs=in_specs,
        out_specs=[pl.BlockSpec((tm, tn), lambda j, i: (i, j)), tail_spec],
        out_shape=[jax.ShapeDtypeStruct((m, C_WIDTH), BF16), tail_shape],
        scratch_shapes=scratch,
        compiler_params=_params("parallel", "arbitrary"),
        name="odd_mix",
    )(*args)


def _row_tiles(m):
    return min(m, 1024), min(m, 512)


def _project_even(x_bf, wt_all, e, bsz, t, kv_stacks, sample=None):
    n_even = wt_all.shape[0]
    hk, hv = ((n_even, st) for st in kv_stacks)
    m = bsz * t
    tm, _ = _row_tiles(m)
    ng = N_GROUPS_EVEN - 2
    if sample is None:
        (zm,) = _inproj_even(x_bf, wt_all, e, (0, 2, ng), tm)
        zk, ka = _inproj_even(x_bf, wt_all, e, (1, 0, 1), tm, heads=hk)
        attn_s = None
    else:
        n_seq = sample[0].shape[1]
        assert n_seq == ng + 1
        tm_a = tm // 2
        zm, attn_a = _inproj_even_with_attn(x_bf, wt_all, e, (0, 2, ng), tm_a, None, sample, 0)
        zk, ka, attn_b = _inproj_even_with_attn(x_bf, wt_all, e, (1, 0, 1), tm_a, hk, sample, ng)
        attn_s = jnp.concatenate([attn_a, attn_b], axis=0)
    zv, va = _inproj_even(x_bf, wt_all, e, (2, 0, 1), tm, heads=hv)
    return (zm.reshape(ng, bsz, t, 1024), zk.reshape(1, bsz, t, 1024), zv.reshape(1, bsz, t, 1024),
            (ka, va), attn_s)


def _finish_even(x, bsz, t, zm, attn, gates, w_out_a, w_out_b, norm_g, ln_g, ln_b, state, L):
    _, tm_out = _row_tiles(bsz * t)
    hm, c1, n1, m1 = _mlstm(zm, gates, norm_g, *state, bsz, t, L)
    xn, xn_bf = _out_ln([attn.reshape(bsz * t, A_WIDTH), hm.reshape(bsz * t, B_WIDTH)],
                        [w_out_a, w_out_b], x, ln_g, ln_b, tm_out)
    return xn, xn_bf, (c1, n1, m1)


def _even_layers(xp, xp_bf, bp, tp, xs, xs_bf, bs, ts, wt_all, e, gbias, w_out_a, w_out_b, norm_g,
                 ln_g, ln_b, bias_p, bias_s, cache_k, cache_v, page_table, state_s, kv_p, kv_s):
    gates_p, xp_bf = _gates(xp if xp_bf is None else xp_bf, wt_all, e, gbias, _row_tiles(bp * tp)[0])
    gates_s, xs_bf = _gates(xs if xs_bf is None else xs_bf, wt_all, e, gbias, _row_tiles(bs * ts)[0])
    zm_s, zk_s, zv_s, kv_s, _ = _project_even(xs_bf, wt_all, e, bs, ts, kv_s)
    sample = (zm_s, zk_s, zv_s, bias_s, cache_k, cache_v, page_table, ts)
    zm_p, zk_p, zv_p, kv_p, attn_s = _project_even(xp_bf, wt_all, e, bp, tp, kv_p, sample)
    attn_p = _moba_prompt(zm_p, zk_p, zv_p, bias_p, bp, tp)
    zeros = (jnp.zeros((bp, B_HEADS, B_HEAD_DIM, B_HEAD_DIM), F32),
             jnp.zeros((bp, B_HEADS, B_HEAD_DIM), F32), jnp.zeros((bp, B_HEADS), F32))
    xp, xp_bf, st_p = _finish_even(xp, bp, tp, zm_p, attn_p, gates_p, w_out_a, w_out_b,
                                   norm_g, ln_g, ln_b, zeros, math.gcd(tp, MLSTM_PROMPT_CHUNK))
    xs, xs_bf, st_s = _finish_even(xs, bs, ts, zm_s, attn_s, gates_s, w_out_a, w_out_b,
                                   norm_g, ln_g, ln_b, state_s, math.gcd(ts, MLSTM_CHUNK))
    return (xp, xp_bf, kv_p, st_p), (xs, xs_bf, kv_s, st_s)


def _odd_layer(x, x_bf, bsz, t, w_in_all, o, w_out, conv_w, ln_g, ln_b, buf):
    tm, tm_out = _row_tiles(bsz * t)
    if buf is None:
        tm = math.gcd(2 * tm, t)
        mix, tail = _odd_mix(x_bf, w_in_all, o, conv_w, tm, 256, t)
        seq_tiles = t // tm
        new_buf = tail.reshape(bsz, seq_tiles, SUBLANES, C_WIDTH)[:, -1, SUBLANES - (CONV_K - 1):, :]
    else:
        plast = jnp.repeat(buf[:, 1, :], t, axis=0)
        pprev = jnp.repeat(buf[:, 0, :], t, axis=0)
        mix, u = _odd_mix(x_bf, w_in_all, o, conv_w, tm, 256, t, state=(plast, pprev))
        new_buf = u.reshape(bsz, t, C_WIDTH)[:, t - (CONV_K - 1):, :]
    xn, xn_bf = _out_ln([mix], [w_out], x, ln_g, ln_b, tm_out)
    return xn, xn_bf, new_buf


def kernel(x_prompt, x_sample, cache_k, cache_v, page_table, state_C, state_n, state_m, state_conv,
           w_in_even, w_out_even, mlstm_gate_bias, mlstm_norm_g, rel_bias,
           w_in_odd, w_out_odd, conv_w, ln_g, ln_b):
    bp, tp, d = x_prompt.shape
    bs, ts, _ = x_sample.shape
    bias_p = _bias_tiles(rel_bias, _prompt_bias_idx())
    bias_s = _bias_tiles(rel_bias, _sample_bias_idx(ts)).reshape(A_HEADS * ts, -1)

    xp = x_prompt.reshape(bp * tp, d)
    xs = x_sample.reshape(bs * ts, d)
    xp_bf = xs_bf = None
    wt_even = jnp.swapaxes(w_in_even, 1, 2)

    kv_p = kv_s = (None, None)
    pc, pn, pm, pb = [], [], [], []
    sc, sn, sm, sb = [], [], [], []
    for layer in range(DEPTH):
        if layer % 2 == 0:
            e = layer // 2
            gbias = jnp.pad(mlstm_gate_bias[e], (0, LANES - 2 * B_HEADS)).reshape(1, LANES).astype(F32)
            w_out_a = w_out_even[e, :A_WIDTH].astype(BF16)
            w_out_b = w_out_even[e, A_WIDTH:].astype(BF16)
            (xp, xp_bf, kv_p, (c1, n1, m1)), (xs, xs_bf, kv_s, (c2, n2, m2)) = _even_layers(
                xp, xp_bf, bp, tp, xs, xs_bf, bs, ts, wt_even, e, gbias, w_out_a, w_out_b,
                mlstm_norm_g[e], ln_g[layer], ln_b[layer], bias_p, bias_s, cache_k, cache_v, page_table,
                (state_C[e], state_n[e], state_m[e]), kv_p, kv_s)
            pc.append(c1); pn.append(n1); pm.append(m1)
            sc.append(c2); sn.append(n2); sm.append(m2)
        else:
            o = layer // 2
            w_out = w_out_odd[o].astype(BF16)
            xp, xp_bf, b1 = _odd_layer(xp, xp_bf, bp, tp, w_in_odd, o, w_out, conv_w[o],
                                       ln_g[layer], ln_b[layer], None)
            xs, xs_bf, b2 = _odd_layer(xs, xs_bf, bs, ts, w_in_odd, o, w_out, conv_w[o],
                                       ln_g[layer], ln_b[layer], state_conv[o])
            pb.append(b1); sb.append(b2)
    n_even = w_in_even.shape[0]
    heads = lambda st, b, t: st.reshape(n_even, b, t, A_HEADS, A_HEAD_DIM)
    return (xp.reshape(bp, tp, d), xs.reshape(bs, ts, d),
            heads(kv_p[0], bp, tp), heads(kv_p[1], bp, tp),
            jnp.stack(pc), jnp.stack(pn), jnp.stack(pm), jnp.stack(pb),
            heads(kv_s[0], bs, ts), heads(kv_s[1], bs, ts),
            jnp.stack(sc), jnp.stack(sn), jnp.stack(sm), jnp.stack(sb))
```

```python
import functools
import math

import numpy as np
import jax
import jax.numpy as jnp
from jax import lax
from jax.experimental import pallas as pl
from jax.experimental.pallas import tpu as pltpu

F32 = jnp.float32
BF16 = jnp.bfloat16

D_MODEL = 2048
DEPTH = 4
PAGE_SIZE = 128
A_HEADS = 8
A_HEAD_DIM = 128
A_WIDTH = A_HEADS * A_HEAD_DIM
MOBA_BLOCK = 256
MOBA_TOPK = 3
REL_BUCKETS = 32
REL_MAX_DIST = 128
B_HEADS = 4
B_HEAD_DIM = 256
B_WIDTH = B_HEADS * B_HEAD_DIM
MLSTM_CHUNK = 64
MLSTM_PROMPT_CHUNK = 256
C_WIDTH = D_MODEL
CONV_K = 3
N_GROUPS_EVEN = 9
EVEN_MAIN = N_GROUPS_EVEN * 1024
ALPHA = (2.0 * DEPTH) ** 0.25
LN_EPS = 1e-5

LANES = 128
SUBLANES = 8
VMEM_LIMIT = 56 * 1024 * 1024

NEG = -1e30


def _params(*sem):
    return pltpu.CompilerParams(dimension_semantics=sem, vmem_limit_bytes=VMEM_LIMIT)


def _nt(a, b):
    return lax.dot_general(a, b, (((1,), (1,)), ((), ())), preferred_element_type=F32)


def _silu(x):
    return x * jax.nn.sigmoid(x)


def _bucket_np(d):
    d = np.asarray(d, np.int64)
    n = np.maximum(d, 0)
    max_exact = REL_BUCKETS // 2
    nf = np.maximum(n, 1).astype(np.float64)
    large = max_exact + (np.log(nf / max_exact) / math.log(REL_MAX_DIST / max_exact)
                         * (REL_BUCKETS - max_exact)).astype(np.int64)
    large = np.minimum(large, REL_BUCKETS - 1)
    out = np.where(n < max_exact, n, large)
    return np.where(d < 0, -1, out).astype(np.int32)


def _bias_kernel(tab_ref, idx_ref, o_ref):
    h = pl.program_id(0)
    idx = idx_ref[...]
    out = jnp.full(idx.shape, NEG, F32)
    for b in range(REL_BUCKETS):
        out = jnp.where(idx == b, tab_ref[h, b], out)
    o_ref[...] = out


def _bias_tiles(rel_bias, idx):
    r, c = idx.shape
    return pl.pallas_call(
        _bias_kernel,
        grid=(A_HEADS,),
        in_specs=[pl.BlockSpec(memory_space=pltpu.SMEM),
                  pl.BlockSpec((r, c), lambda h: (0, 0))],
        out_specs=pl.BlockSpec((None, r, c), lambda h: (h, 0, 0)),
        out_shape=jax.ShapeDtypeStruct((A_HEADS, r, c), F32),
        compiler_params=_params("parallel"),
        name="bias_tiles",
    )(rel_bias.astype(F32), jnp.asarray(idx))


def _prompt_bias_idx():
    qi = np.arange(MOBA_BLOCK)[:, None]
    kj = np.arange(2 * MOBA_BLOCK)[None, :]
    return _bucket_np(qi + MOBA_BLOCK - kj)


def _sample_bias_idx(s_len):
    s = np.arange(s_len)[:, None]
    lane = np.arange(2 * PAGE_SIZE * A_HEADS)[None, :]
    off = lane // A_HEADS
    far = _bucket_np(MOBA_BLOCK + s - off)
    lo = np.arange(LANES)[None, :]
    own = _bucket_np(np.where(lo < A_HEADS * s_len, s - (lo % s_len), -1))
    return np.concatenate([far, own], axis=1)


def _inproj_prep(w_ref, wb_ref):
    @pl.when(pl.program_id(1) == 0)
    def _():
        wb_ref[...] = w_ref[...].astype(BF16)


def _inproj_compute(x_ref, o_ref, oh_ref, wb_ref, part=0, n_parts=1):
    tm = x_ref.shape[0]
    nf = wb_ref.shape[0] // n_parts
    cols = pl.ds(part * nf, nf)
    y = _nt(x_ref[...], wb_ref[cols, :])
    o_ref[:, cols] = y.astype(o_ref.dtype)
    if oh_ref is not None:
        nh = nf // A_HEAD_DIM
        oh_ref[:, pl.ds(part * nh, nh), :] = y.reshape(tm, nh, A_HEAD_DIM)


def _inproj_kernel(x_ref, w_ref, *rest, heads_out, aliased):
    if aliased:
        rest = rest[1:]
    if heads_out:
        o_ref, oh_ref, wb_ref = rest
    else:
        (o_ref, wb_ref), oh_ref = rest, None
    _inproj_prep(w_ref, wb_ref)
    _inproj_compute(x_ref, o_ref, oh_ref, wb_ref)


def _heads_out_spec(heads, e, m, tm, index_map):
    n_layers, stack = heads
    spec = pl.BlockSpec((None, tm, A_HEADS, A_HEAD_DIM), index_map)
    shape = jax.ShapeDtypeStruct((n_layers, m, A_HEADS, A_HEAD_DIM), F32)
    return spec, shape, stack


def _inproj_even(x_bf, wt_all, e, groups, tm, heads=None):
    m, k = x_bf.shape
    gw = 1024
    first, skip, n_groups = groups
    gmap = lambda j: first + j + skip * jnp.minimum(j, 1)
    out_specs = [pl.BlockSpec((None, tm, gw), lambda j, i: (j, i, 0))]
    out_shape = [jax.ShapeDtypeStruct((n_groups, m, gw), BF16)]
    in_specs = [pl.BlockSpec((tm, k), lambda j, i: (i, 0)),
                pl.BlockSpec((None, gw, k), lambda j, i: (e, gmap(j), 0))]
    args = [x_bf, wt_all]
    aliases = {}
    if heads is not None:
        assert n_groups == 1
        spec, shape, stack = _heads_out_spec(heads, e, m, tm, lambda j, i: (e, i, 0, 0))
        out_specs.append(spec)
        out_shape.append(shape)
        if stack is not None:
            in_specs.append(pl.BlockSpec(memory_space=pl.ANY))
            args.append(stack)
            aliases = {len(args) - 1: 1}
    return pl.pallas_call(
        functools.partial(_inproj_kernel, heads_out=heads is not None, aliased=bool(aliases)),
        grid=(n_groups, m // tm),
        in_specs=in_specs,
        out_specs=out_specs,
        out_shape=out_shape,
        scratch_shapes=[pltpu.VMEM((gw, k), BF16)],
        input_output_aliases=aliases,
        compiler_params=_params("parallel", "arbitrary"),
        name="inproj_even",
    )(*args)


def _gates_kernel(x_ref, w_ref, b_ref, o_ref, *xb_ref):
    w = w_ref[...]
    wp = jnp.concatenate([w, jnp.zeros((LANES - w.shape[0], w.shape[1]), F32)], axis=0)
    xb = x_ref[...].astype(BF16)
    o_ref[...] = _nt(xb, wp.astype(BF16)) + b_ref[...]
    if xb_ref:
        xb_ref[0][...] = xb


def _gates(x, wt_all, e, gbias, tm):
    m, k = x.shape
    ng = 2 * B_HEADS
    assert ng == SUBLANES and EVEN_MAIN % ng == 0
    out_specs = [pl.BlockSpec((tm, LANES), lambda i: (i, 0))]
    out_shape = [jax.ShapeDtypeStruct((m, LANES), F32)]
    if x.dtype != BF16:
        out_specs.append(pl.BlockSpec((tm, k), lambda i: (i, 0)))
        out_shape.append(jax.ShapeDtypeStruct((m, k), BF16))
    outs = pl.pallas_call(
        _gates_kernel,
        grid=(m // tm,),
        in_specs=[pl.BlockSpec((tm, k), lambda i: (i, 0)),
                  pl.BlockSpec((None, ng, k), lambda i: (e, EVEN_MAIN // ng, 0)),
                  pl.BlockSpec((1, LANES), lambda i: (0, 0))],
        out_specs=out_specs,
        out_shape=out_shape,
        compiler_params=_params("parallel"),
        name="mlstm_gates",
    )(x, wt_all, gbias)
    return outs[0], (outs[1] if len(outs) > 1 else x)


def _moba_prompt_kernel(q_ref, k_ref, v_ref, g_ref, bias_ref, o_ref, kmean_ref, negm_ref,
                        *, nblk, scale):
    blk = MOBA_BLOCK
    for j in range(nblk):
        kmean_ref[j:j + 1, :] = jnp.mean(k_ref[j * blk:(j + 1) * blk, :].astype(F32), axis=0, keepdims=True)
    far_bias = bias_ref[0:1, 0:1]
    eye = (lax.broadcasted_iota(jnp.int32, (blk, blk), 0)
           == lax.broadcasted_iota(jnp.int32, (blk, blk), 1)).astype(BF16)
    rowi = lax.broadcasted_iota(jnp.int32, (nblk, blk), 0)

    i0 = MOBA_TOPK + 1
    if i0 < nblk:
        nq = (nblk - i0) * blk
        gt = lax.dot_general(kmean_ref[...], q_ref[i0 * blk:, :].astype(F32), (((1,), (1,)), ((), ())),
                             precision=lax.Precision.HIGHEST, preferred_element_type=F32)
        rowq = lax.broadcasted_iota(jnp.int32, (nblk, nq), 0)
        own = i0 + (lax.broadcasted_iota(jnp.int32, (nblk, nq), 1) >> (blk.bit_length() - 1))
        valid = rowq < own
        sel_t = jnp.zeros((nblk, nq), F32)
        for j in range(nblk - 1):
            gj = gt[j:j + 1, :]
            beats = ((gt > gj) | ((gt == gj) & (rowq < j))) & valid
            rank = jnp.sum(beats.astype(F32), axis=0, keepdims=True)
            sel_t = jnp.where(rowq == j, (rank < MOBA_TOPK).astype(F32), sel_t)
        sel_t = jnp.concatenate([sel_t, jnp.zeros((LANES - nblk, nq), F32)], axis=0).astype(BF16)
        for i in range(i0, nblk):
            cs = (i - i0) * blk
            sel = _nt(eye, sel_t[:, cs:cs + blk])
            negm_ref[i - i0] = (sel - 1.0) * (-NEG)

    for i in range(nblk):
        q = q_ref[i * blk:(i + 1) * blk, :].astype(F32)
        negm = negm_ref[i - MOBA_TOPK - 1] if i > MOBA_TOPK else None
        w = (i + 1) * blk
        s = _nt((q * scale).astype(BF16), k_ref[0:w, :])
        pieces = []
        for j in range(i + 1):
            sj = s[:, j * blk:(j + 1) * blk]
            if j == i:
                sj = sj + bias_ref[:, blk:2 * blk]
            elif j == i - 1:
                sj = sj + bias_ref[:, 0:blk]
                if negm is not None:
                    sj = sj + negm[:, j:j + 1]
            else:
                sj = sj + (far_bias if negm is None else far_bias + negm[:, j:j + 1])
            pieces.append(sj)
        m = jnp.max(functools.reduce(jnp.maximum, pieces), axis=1, keepdims=True)
        ps = [jnp.exp(sj - m) for sj in pieces]
        l = jnp.sum(functools.reduce(jnp.add, ps), axis=1, keepdims=True)
        pcat = ps[0].astype(BF16) if i == 0 else jnp.concatenate([p.astype(BF16) for p in ps], axis=1)
        acc = jnp.dot(pcat, v_ref[0:w, :], preferred_element_type=F32)
        g = g_ref[i * blk:(i + 1) * blk, :].astype(F32)
        o_ref[i * blk:(i + 1) * blk, :] = (acc / l * _silu(g)).astype(o_ref.dtype)


def _moba_prompt(zm, zk, zv, bias_p, bsz, t):
    blk = MOBA_BLOCK
    nblk = t // blk
    assert nblk <= SUBLANES
    dh = A_HEAD_DIM
    kern = functools.partial(_moba_prompt_kernel, nblk=nblk, scale=dh ** -0.5)
    zspec = lambda g: pl.BlockSpec((None, None, t, dh), lambda b, h: (g, b, 0, h))
    return pl.pallas_call(
        kern,
        grid=(bsz, A_HEADS),
        in_specs=[zspec(0), zspec(0), zspec(0), zspec(1),
                  pl.BlockSpec((None, blk, 2 * blk), lambda b, h: (h, 0, 0))],
        out_specs=pl.BlockSpec((None, t, dh), lambda b, h: (b, 0, h)),
        out_shape=jax.ShapeDtypeStruct((bsz, t, A_WIDTH), BF16),
        scratch_shapes=[pltpu.VMEM((nblk, dh), F32),
                        pltpu.VMEM((max(nblk - MOBA_TOPK - 1, 1), blk, LANES), F32)],
        compiler_params=_params("parallel", "parallel"),
        name="moba_prompt",
    )(zm, zk, zv, zm, bias_p)


def _stack_heads(x):
    return jnp.concatenate([x[:, h * A_HEAD_DIM:(h + 1) * A_HEAD_DIM] for h in range(A_HEADS)], axis=0)


def _moba_sample_body(q_ref, kn_ref, vn_ref, ga_ref, bias_ref, k_refs, v_refs, o_ref,
                      qm_ref, gate_ref, m_ref, l_ref, oblk_ref, *, nb, bps, s_len, scale,
                      alongside=None):
    n = pl.program_id(1)
    rows = A_HEADS * s_len
    s_shift = s_len.bit_length() - 1
    far_w = PAGE_SIZE * A_HEADS

    @pl.when(n == 0)
    def _():
        qm_ref[...] = _stack_heads(q_ref[...].astype(F32))
        gate_ref[...] = jnp.full_like(gate_ref, NEG)
        m_ref[...] = jnp.full_like(m_ref, NEG)
        l_ref[...] = jnp.zeros_like(l_ref)

    qb = qm_ref[...].astype(BF16)
    row = lax.broadcasted_iota(jnp.int32, (rows, far_w), 0)
    lane = lax.broadcasted_iota(jnp.int32, (rows, far_w), 1)
    useful = (lane & (A_HEADS - 1)) == (row >> s_shift)
    lane_b = lax.broadcasted_iota(jnp.int32, (rows, LANES), 1)
    last = n == pl.num_programs(1) - 1
    far_bias = bias_ref[:, 0:1]

    def pages(ref):
        return ref[...].reshape(far_w, A_HEAD_DIM).astype(BF16)

    gate_new = gate_ref[...]
    m_new = m_ref[...]
    l_new = l_ref[...]
    for c in range(bps):
        blk = n * bps + c
        raw0 = _nt(qb, pages(k_refs[2 * c]))
        raw1 = _nt(qb, pages(k_refs[2 * c + 1]))
        if alongside is not None:
            alongside(c, bps)
        gsum = jnp.sum(jnp.where(useful, raw0 + raw1, 0.0), axis=1, keepdims=True)
        if c == bps - 1:
            last_v = (jnp.zeros((rows, far_w), jnp.int32) + blk) == nb - 1
            b0 = jnp.where(last_v, bias_ref[:, 0:far_w], far_bias)
            b1 = jnp.where(last_v, bias_ref[:, far_w:2 * far_w], far_bias)
        else:
            b0 = b1 = far_bias
        s0 = jnp.where(useful, raw0 * scale + b0, NEG)
        s1 = jnp.where(useful, raw1 * scale + b1, NEG)
        mloc = jnp.max(jnp.maximum(s0, s1), axis=1, keepdims=True)
        p0 = jnp.exp(s0 - mloc)
        p1 = jnp.exp(s1 - mloc)
        lloc = jnp.sum(p0 + p1, axis=1, keepdims=True)
        oblk_ref[blk] = (jnp.dot(p0.astype(BF16), pages(v_refs[2 * c]), preferred_element_type=F32)
                         + jnp.dot(p1.astype(BF16), pages(v_refs[2 * c + 1]), preferred_element_type=F32))
        here = lane_b == blk
        gate_new = jnp.where(here, gsum * (1.0 / MOBA_BLOCK), gate_new)
        m_new = jnp.where(here, mloc, m_new)
        l_new = jnp.where(here, lloc, l_new)
    gate_ref[...] = gate_new
    m_ref[...] = m_new
    l_ref[...] = l_new

    @pl.when(last)
    def _():
        row_b = lax.broadcasted_iota(jnp.int32, (rows, LANES), 0)
        lane_f = lane_b.astype(F32)
        g = jnp.where(lane_b < nb, gate_ref[...], -jnp.inf)
        sel = jnp.zeros((rows, LANES), jnp.bool_)
        for _r in range(min(MOBA_TOPK, nb)):
            mx = jnp.max(g, axis=1, keepdims=True)
            idx = jnp.min(jnp.where(g == mx, lane_f, float(LANES)), axis=1, keepdims=True)
            hit = lane_f == idx
            sel = sel | hit
            g = jnp.where(hit, -jnp.inf, g)
        pad = jnp.zeros((LANES - rows, A_HEAD_DIM), F32)
        kn = jnp.concatenate([_stack_heads(kn_ref[...].astype(F32)), pad], axis=0).astype(BF16)
        vn = jnp.concatenate([_stack_heads(vn_ref[...].astype(F32)), pad], axis=0).astype(BF16)
        so = _nt(qb, kn) * scale + bias_ref[:, 2 * far_w:2 * far_w + LANES]
        ok = ((lane_b < rows) & ((lane_b >> s_shift) == (row_b >> s_shift))
              & ((lane_b & (s_len - 1)) <= (row_b & (s_len - 1))))
        so = jnp.where(ok, so, NEG)
        mm = m_ref[...]
        mtot = jnp.maximum(jnp.max(jnp.where(sel, mm, NEG), axis=1, keepdims=True),
                           jnp.max(so, axis=1, keepdims=True))
        w = jnp.where(sel, jnp.exp(mm - mtot), 0.0)
        po = jnp.exp(so - mtot)
        lsum = jnp.sum(w * l_ref[...], axis=1, keepdims=True) + jnp.sum(po, axis=1, keepdims=True)
        acc = jnp.dot(po.astype(BF16), vn, preferred_element_type=F32)
        for nn in range(nb):
            acc = acc + w[:, nn:nn + 1] * oblk_ref[nn]
        out = acc / lsum
        res = jnp.concatenate([out[h * s_len:(h + 1) * s_len, :] for h in range(A_HEADS)], axis=1)
        o_ref[...] = (res * _silu(ga_ref[...].astype(F32))).astype(o_ref.dtype)


def _inproj_attn_kernel(pt_ref, x_ref, w_ref, q_ref, kn_ref, vn_ref, ga_ref, bias_ref, *rest,
                        heads_out, aliased, nb, bps, s_len, scale):
    del pt_ref
    npg = 2 * bps
    k_refs = rest[:npg]
    v_refs = rest[npg:2 * npg]
    rest = rest[2 * npg + (1 if aliased else 0):]
    if heads_out:
        o_ref, oh_ref, ao_ref, wb_ref = rest[:4]
    else:
        (o_ref, ao_ref, wb_ref), oh_ref = rest[:3], None
    _inproj_prep(w_ref, wb_ref)
    _moba_sample_body(q_ref, kn_ref, vn_ref, ga_ref, bias_ref, k_refs, v_refs, ao_ref, *rest[-5:],
                      nb=nb, bps=bps, s_len=s_len, scale=scale,
                      alongside=functools.partial(_inproj_compute, x_ref, o_ref, oh_ref, wb_ref))


def _inproj_even_with_attn(x_bf, wt_all, e, groups, tm, heads, sample, b0):
    zm, zk, zv, bias_s, cache_k, cache_v, page_table, s_len = sample
    m, k = x_bf.shape
    gw = 1024
    first, skip, n_groups = groups
    nsteps = m // tm
    n_pages = page_table.shape[1]
    ppb = MOBA_BLOCK // PAGE_SIZE
    assert ppb == 2 and n_pages % ppb == 0 and s_len == SUBLANES and A_HEADS == SUBLANES
    nb = n_pages // ppb
    assert nb <= LANES and nb % nsteps == 0
    bps = nb // nsteps
    rows = A_HEADS * s_len
    dh = A_HEAD_DIM
    gmap = lambda j: first + j + skip * jnp.minimum(j, 1)
    tok = lambda g: pl.BlockSpec((None, None, s_len, A_WIDTH), lambda j, i, pt: (g, b0 + j, 0, 0))
    page = lambda o: pl.BlockSpec((None, None, PAGE_SIZE, A_HEADS, dh),
                                  lambda j, i, pt: (e, pt[b0 + j, ppb * bps * i + o], 0, 0, 0))
    pages = [page(o) for o in range(ppb * bps)]
    out_specs = [pl.BlockSpec((None, tm, gw), lambda j, i, pt: (j, i, 0))]
    out_shape = [jax.ShapeDtypeStruct((n_groups, m, gw), BF16)]
    args = [page_table, x_bf, wt_all, zm, zk, zv, zm, bias_s,
            *([cache_k] * (ppb * bps)), *([cache_v] * (ppb * bps))]
    extra_specs = []
    aliases = {}
    if heads is not None:
        assert n_groups == 1
        spec, shape, stack = _heads_out_spec(heads, e, m, tm, lambda j, i, pt: (e, i, 0, 0))
        out_specs.append(spec)
        out_shape.append(shape)
        if stack is not None:
            extra_specs.append(pl.BlockSpec(memory_space=pl.ANY))
            args.append(stack)
            aliases = {len(args) - 1: 1}
    out_specs.append(pl.BlockSpec((None, s_len, A_WIDTH), lambda j, i, pt: (j, 0, 0)))
    out_shape.append(jax.ShapeDtypeStruct((n_groups, s_len, A_WIDTH), BF16))
    grid_spec = pltpu.PrefetchScalarGridSpec(
        num_scalar_prefetch=1,
        grid=(n_groups, nsteps),
        in_specs=[pl.BlockSpec((tm, k), lambda j, i, pt: (i, 0)),
                  pl.BlockSpec((None, gw, k), lambda j, i, pt: (e, gmap(j), 0)),
                  tok(0), tok(0), tok(0), tok(1),
                  pl.BlockSpec(bias_s.shape, lambda j, i, pt: (0, 0))] + pages + pages + extra_specs,
        out_specs=out_specs,
        scratch_shapes=[pltpu.VMEM((gw, k), BF16),
                        pltpu.VMEM((rows, dh), F32),
                        pltpu.VMEM((rows, LANES), F32),
                        pltpu.VMEM((rows, LANES), F32),
                        pltpu.VMEM((rows, LANES), F32),
                        pltpu.VMEM((nb, rows, dh), F32)])
    kern = functools.partial(_inproj_attn_kernel, heads_out=heads is not None, aliased=bool(aliases),
                             nb=nb, bps=bps, s_len=s_len, scale=dh ** -0.5)
    return pl.pallas_call(
        kern,
        grid_spec=grid_spec,
        out_shape=out_shape,
        input_output_aliases=aliases,
        compiler_params=_params("arbitrary", "arbitrary"),
        name="inproj_even_sample_attn",
    )(*args)


def _mlstm_kernel(q_ref, k_ref, v_ref, og_ref, gm_ref, g_ref, gt_ref, ng_ref, c0_ref, n0_ref, m0_ref,
                  h_ref, c_out, n_out, m_out, c_scr, n_scr, m_scr, *, L, Lp):
    c = pl.program_id(1)
    dk = B_HEAD_DIM

    @pl.when(c == 0)
    def _():
        c_scr[...] = c0_ref[...]
        n_scr[...] = n0_ref[...]
        m_scr[...] = m0_ref[...]

    def padded(x):
        if L == Lp:
            return x
        return jnp.concatenate([x, jnp.zeros((Lp - L, x.shape[1]), x.dtype)], axis=0)

    g = padded(g_ref[...])
    gt = gt_ref[...]
    r = lax.broadcasted_iota(jnp.int32, (Lp, Lp), 0)
    cc = lax.broadcasted_iota(jnp.int32, (Lp, Lp), 1)
    tri = r >= cc
    tri_t = r <= cc
    rv = lax.broadcasted_iota(jnp.int32, (Lp, 1), 0) < L
    cv = lax.broadcasted_iota(jnp.int32, (1, Lp), 1) < L

    heads = range(B_HEADS)
    cols = [slice(hh * dk, (hh + 1) * dk) for hh in heads]
    ph = []
    for hh in heads:
        q = padded(q_ref[:, cols[hh]].astype(F32))
        k = padded(k_ref[:, cols[hh]].astype(F32)) * (dk ** -0.5)
        v = padded(v_ref[:, cols[hh]].astype(F32))
        li_col = g[:, hh:hh + 1]
        li_row = gt[hh:hh + 1, :]
        lf_col = jax.nn.log_sigmoid(g[:, hh + B_HEADS:hh + B_HEADS + 1])
        lf_row = jax.nn.log_sigmoid(gt[hh + B_HEADS:hh + B_HEADS + 1, :])
        if L != Lp:
            li_col = jnp.where(rv, li_col, NEG)
            lf_col = jnp.where(rv, lf_col, 0.0)
            li_row = jnp.where(cv, li_row, NEG)
            lf_row = jnp.where(cv, lf_row, 0.0)
        b_col = jnp.sum(jnp.where(tri, lf_row, 0.0), axis=1, keepdims=True)
        b_row = jnp.sum(jnp.where(tri_t, lf_col, 0.0), axis=0, keepdims=True)
        m_prev = m_scr[hh]
        log_d = jnp.where(tri, b_col - b_row + li_row, NEG)
        m_inter = b_col + m_prev
        m_t = jnp.maximum(m_inter, jnp.max(log_d, axis=1, keepdims=True))
        ph.append(dict(q=q, k=k, vb=v.astype(BF16), li_col=li_col, li_row=li_row, b_col=b_col,
                       b_row=b_row, m_prev=m_prev, log_d=log_d, m_inter=m_inter, m_t=m_t))
    for hh in heads:
        p = ph[hh]
        qb = p['q'].astype(BF16)
        s = _nt(qb, p['k'].astype(BF16)) * jnp.exp(p['log_d'] - p['m_t'])
        w_inter = jnp.exp(p['m_inter'] - p['m_t'])
        c_old = c_scr[hh]
        n_old = n_scr[hh]
        num = (jnp.dot(s.astype(BF16), p['vb'], preferred_element_type=F32)
               + w_inter * jnp.dot(qb, c_old.astype(BF16), preferred_element_type=F32))
        den = jnp.sum(s, axis=1, keepdims=True) + w_inter * jnp.sum(p['q'] * n_old, axis=1, keepdims=True)
        p.update(h=num / jnp.maximum(jnp.abs(den), jnp.exp(-p['m_t'])), c_old=c_old, n_old=n_old)
    for hh in heads:
        p = ph[hh]
        b_last = p['b_row'][:, L - 1:L]
        m_new = jnp.maximum(b_last + p['m_prev'],
                            jnp.max(b_last - p['b_row'] + p['li_row'], axis=1, keepdims=True))
        decay = jnp.exp(b_last + p['m_prev'] - m_new)
        wk = jnp.exp(b_last - p['b_col'] + p['li_col'] - m_new)
        kw = p['k'] * wk
        c_scr[hh] = decay * p['c_old'] + lax.dot_general(kw.astype(BF16), p['vb'], (((0,), (0,)), ((), ())),
                                                         preferred_element_type=F32)
        n_scr[hh] = decay * p['n_old'] + jnp.sum(kw, axis=0, keepdims=True)
        m_scr[hh] = m_new
    for hh in heads:
        hv = ph[hh]['h'][:L]
        mu = jnp.mean(hv, axis=1, keepdims=True)
        var = jnp.mean(jnp.square(hv - mu), axis=1, keepdims=True)
        hn = (hv - mu) * lax.rsqrt(var + LN_EPS) * ng_ref[hh]
        h_ref[:, cols[hh]] = (hn * jax.nn.sigmoid(og_ref[:, cols[hh]].astype(F32))
                              * _silu(gm_ref[:, cols[hh]].astype(F32))).astype(h_ref.dtype)

    @pl.when(c == pl.num_programs(1) - 1)
    def _():
        c_out[...] = c_scr[...]
        n_out[...] = n_scr[...]
        m_out[...] = m_scr[...]


def _mlstm(zm, gates, norm_g, c0, n0, m0, bsz, t, L):
    Lp = max(L, LANES)
    nc = t // L
    dh = B_HEAD_DIM
    g3 = gates.reshape(bsz, t, LANES)
    gt = jnp.swapaxes(gates[:, :2 * B_HEADS].reshape(bsz, nc, L, 2 * B_HEADS), 2, 3)
    if Lp != L:
        gt = jnp.pad(gt, ((0, 0), (0, 0), (0, 0), (0, Lp - L)))
    ng = norm_g.reshape(B_HEADS, 1, dh).astype(F32)
    n0 = n0.reshape(bsz, B_HEADS, 1, dh)
    m0 = m0.reshape(bsz, B_HEADS, 1, 1)
    kern = functools.partial(_mlstm_kernel, L=L, Lp=Lp)
    zspec = lambda gidx: pl.BlockSpec((None, None, L, B_WIDTH), lambda b, c: (gidx, b, c, 0))
    st = lambda r, cdim: pl.BlockSpec((None, B_HEADS, r, cdim), lambda b, c: (b, 0, 0, 0))
    h, c1, n1, m1 = pl.pallas_call(
        kern,
        grid=(bsz, nc),
        in_specs=[zspec(2), zspec(3), zspec(4), zspec(5), zspec(6),
                  pl.BlockSpec((None, L, LANES), lambda b, c: (b, c, 0)),
                  pl.BlockSpec((None, None, 2 * B_HEADS, Lp), lambda b, c: (b, c, 0, 0)),
                  pl.BlockSpec((B_HEADS, 1, dh), lambda b, c: (0, 0, 0)),
                  st(dh, dh), st(1, dh), st(1, 1)],
        out_specs=[pl.BlockSpec((None, L, B_WIDTH), lambda b, c: (b, c, 0)),
                   st(dh, dh), st(1, dh), st(1, 1)],
        out_shape=[jax.ShapeDtypeStruct((bsz, t, B_WIDTH), BF16),
                   jax.ShapeDtypeStruct((bsz, B_HEADS, dh, dh), F32),
                   jax.ShapeDtypeStruct((bsz, B_HEADS, 1, dh), F32),
                   jax.ShapeDtypeStruct((bsz, B_HEADS, 1, 1), F32)],
        scratch_shapes=[pltpu.VMEM((B_HEADS, dh, dh), F32), pltpu.VMEM((B_HEADS, 1, dh), F32),
                        pltpu.VMEM((B_HEADS, 1, 1), F32)],
        compiler_params=_params("parallel", "arbitrary"),
        name="mlstm",
    )(zm, zm, zm, zm, zm, g3, gt, ng, c0, n0, m0)
    return h, c1, n1.reshape(bsz, B_HEADS, dh), m1.reshape(bsz, B_HEADS)


def _outln_kernel(*refs, n_parts):
    mix = refs[:n_parts]
    ws = refs[n_parts:2 * n_parts]
    x_ref, g_ref, b_ref, o_ref, ob_ref = refs[2 * n_parts:]
    tm = x_ref.shape[0]
    sub = min(tm, LANES)
    for r in range(tm // sub):
        rows = pl.ds(r * sub, sub)
        y = jnp.dot(mix[0][rows, :], ws[0][...], preferred_element_type=F32)
        for p in range(1, n_parts):
            y = y + jnp.dot(mix[p][rows, :], ws[p][...], preferred_element_type=F32)
        z = ALPHA * x_ref[rows, :] + y
        mu = jnp.mean(z, axis=1, keepdims=True)
        var = jnp.mean(jnp.square(z - mu), axis=1, keepdims=True)
        xn = (z - mu) * lax.rsqrt(var + LN_EPS) * g_ref[...] + b_ref[...]
        o_ref[rows, :] = xn
        ob_ref[rows, :] = xn.astype(BF16)


def _out_ln(mix_parts, w_parts, x, ln_g, ln_b, tm):
    m, d = x.shape
    n_parts = len(mix_parts)
    in_specs = ([pl.BlockSpec((tm, mp.shape[1]), lambda i: (i, 0)) for mp in mix_parts]
                + [pl.BlockSpec(wp.shape, lambda i: (0, 0)) for wp in w_parts]
                + [pl.BlockSpec((tm, d), lambda i: (i, 0)),
                   pl.BlockSpec((1, d), lambda i: (0, 0)),
                   pl.BlockSpec((1, d), lambda i: (0, 0))])
    return pl.pallas_call(
        functools.partial(_outln_kernel, n_parts=n_parts),
        grid=(m // tm,),
        in_specs=in_specs,
        out_specs=[pl.BlockSpec((tm, d), lambda i: (i, 0)), pl.BlockSpec((tm, d), lambda i: (i, 0))],
        out_shape=[jax.ShapeDtypeStruct((m, d), F32), jax.ShapeDtypeStruct((m, d), BF16)],
        compiler_params=_params("parallel"),
        name="outproj_ln",
    )(*mix_parts, *w_parts, x, ln_g.reshape(1, d).astype(F32), ln_b.reshape(1, d).astype(F32))


def _odd_kernel(*refs, tm, carry_mode, seq_tiles, seq_rows):
    if carry_mode:
        x_ref, wb, wc, wx, wg, cw_ref, mix_ref, tail_ref, wbf, carry = refs
    else:
        x_ref, wb, wc, wx, wg, cw_ref, plast_ref, pprev_ref, mix_ref, tail_ref, wbf = refs

    @pl.when(pl.program_id(1) == 0)
    def _():
        for gidx, wref in enumerate((wb, wc, wx, wg)):
            wbf[gidx] = wref[...].astype(BF16)

    if carry_mode:
        @pl.when(pl.program_id(1) % seq_tiles == 0)
        def _():
            carry[...] = jnp.zeros_like(carry)
        last = carry[SUBLANES - 1:SUBLANES, :]
        prev = carry[SUBLANES - 2:SUBLANES - 1, :]

    cw = cw_ref[...]
    sub = min(tm, 4 * LANES)
    for r in range(tm // sub):
        rs = pl.ds(r * sub, sub)
        x = x_ref[rs, :]
        zb = jnp.dot(x, wbf[0], preferred_element_type=F32)
        zc = jnp.dot(x, wbf[1], preferred_element_type=F32)
        zx = jnp.dot(x, wbf[2], preferred_element_type=F32)
        zg = jnp.dot(x, wbf[3], preferred_element_type=F32)
        u = zc * zx
        rows = lax.broadcasted_iota(jnp.int32, u.shape, 0)
        r1 = pltpu.roll(u, 1, 0)
        r2 = pltpu.roll(u, 2, 0)
        if carry_mode:
            rm = rows
        else:
            last = plast_ref[rs, :]
            prev = pprev_ref[rs, :]
            rm = rows & (seq_rows - 1)
        u1 = jnp.where(rm == 0, last, r1)
        u2 = jnp.where(rm == 0, prev, jnp.where(rm == 1, last, r2))
        y = cw[0:1, :] * u2 + cw[1:2, :] * u1 + cw[2:3, :] * u
        mix_ref[rs, :] = (zb * y * _silu(zg)).astype(mix_ref.dtype)
        if carry_mode:
            last = u[sub - 1:sub, :]
            prev = u[sub - 2:sub - 1, :]
        else:
            tail_ref[rs, :] = u
    if carry_mode:
        tail = u[sub - SUBLANES:, :]
        carry[...] = tail
        tail_ref[...] = tail


def _odd_mix(x_bf, w_all, o, conv_w, tm, tn, seq_len, state=None):
    m, k = x_bf.shape
    nn = C_WIDTH // tn
    carry_mode = state is None
    wspec = lambda gidx: pl.BlockSpec((None, k, tn), lambda j, i: (o, 0, gidx * nn + j))
    in_specs = [pl.BlockSpec((tm, k), lambda j, i: (i, 0)),
                wspec(0), wspec(1), wspec(2), wspec(3),
                pl.BlockSpec((CONV_K, tn), lambda j, i: (0, j))]
    args = [x_bf, w_all, w_all, w_all, w_all, conv_w.astype(F32)]
    scratch = [pltpu.VMEM((4, k, tn), BF16)]
    if carry_mode:
        assert seq_len % tm == 0
        tail_shape = jax.ShapeDtypeStruct((m // tm, SUBLANES, C_WIDTH), F32)
        tail_spec = pl.BlockSpec((None, SUBLANES, tn), lambda j, i: (i, 0, j))
        scratch.append(pltpu.VMEM((SUBLANES, tn), F32))
    else:
        assert tm % seq_len == 0 and seq_len & (seq_len - 1) == 0
        in_specs += [pl.BlockSpec((tm, tn), lambda j, i: (i, j))] * 2
        args += list(state)
        tail_shape = jax.ShapeDtypeStruct((m, C_WIDTH), F32)
        tail_spec = pl.BlockSpec((tm, tn), lambda j, i: (i, j))
    kern = functools.partial(_odd_kernel, tm=tm, carry_mode=carry_mode,
                             seq_tiles=max(seq_len // tm, 1), seq_rows=seq_len)
    return pl.pallas_call(
        kern,
        grid=(nn, m // tm),
        in_specs=in_specs,
        out_specs=[pl.BlockSpec((tm, tn), lambda j, i: (i, j)), tail_spec],
        out_shape=[jax.ShapeDtypeStruct((m, C_WIDTH), BF16), tail_shape],
        scratch_shapes=scratch,
        compiler_params=_params("parallel", "arbitrary"),
        name="odd_mix",
    )(*args)


def _row_tiles(m):
    return min(m, 1024), min(m, 512)


def _project_even(x_bf, wt_all, e, bsz, t, kv_stacks, sample=None):
    n_even = wt_all.shape[0]
    hk, hv = ((n_even, st) for st in kv_stacks)
    m = bsz * t
    tm, _ = _row_tiles(m)
    ng = N_GROUPS_EVEN - 2
    if sample is None:
        (zm,) = _inproj_even(x_bf, wt_all, e, (0, 2, ng), tm)
        zk, ka = _inproj_even(x_bf, wt_all, e, (1, 0, 1), tm, heads=hk)
        attn_s = None
    else:
        n_seq = sample[0].shape[1]
        assert n_seq == ng + 1
        tm_a = tm // 2
        zm, attn_a = _inproj_even_with_attn(x_bf, wt_all, e, (0, 2, ng), tm_a, None, sample, 0)
        zk, ka, attn_b = _inproj_even_with_attn(x_bf, wt_all, e, (1, 0, 1), tm_a, hk, sample, ng)
        attn_s = jnp.concatenate([attn_a, attn_b], axis=0)
    zv, va = _inproj_even(x_bf, wt_all, e, (2, 0, 1), tm, heads=hv)
    return (zm.reshape(ng, bsz, t, 1024), zk.reshape(1, bsz, t, 1024), zv.reshape(1, bsz, t, 1024),
            (ka, va), attn_s)


def _finish_even(x, bsz, t, zm, attn, gates, w_out_a, w_out_b, norm_g, ln_g, ln_b, state, L):
    _, tm_out = _row_tiles(bsz * t)
    hm, c1, n1, m1 = _mlstm(zm, gates, norm_g, *state, bsz, t, L)
    xn, xn_bf = _out_ln([attn.reshape(bsz * t, A_WIDTH), hm.reshape(bsz * t, B_WIDTH)],
                        [w_out_a, w_out_b], x, ln_g, ln_b, tm_out)
    return xn, xn_bf, (c1, n1, m1)


def _even_layers(xp, xp_bf, bp, tp, xs, xs_bf, bs, ts, wt_all, e, gbias, w_out_a, w_out_b, norm_g,
                 ln_g, ln_b, bias_p, bias_s, cache_k, cache_v, page_table, state_s, kv_p, kv_s):
    gates_p, xp_bf = _gates(xp if xp_bf is None else xp_bf, wt_all, e, gbias, _row_tiles(bp * tp)[0])
    gates_s, xs_bf = _gates(xs if xs_bf is None else xs_bf, wt_all, e, gbias, _row_tiles(bs * ts)[0])
    zm_s, zk_s, zv_s, kv_s, _ = _project_even(xs_bf, wt_all, e, bs, ts, kv_s)
    sample = (zm_s, zk_s, zv_s, bias_s, cache_k, cache_v, page_table, ts)
    zm_p, zk_p, zv_p, kv_p, attn_s = _project_even(xp_bf, wt_all, e, bp, tp, kv_p, sample)
    attn_p = _moba_prompt(zm_p, zk_p, zv_p, bias_p, bp, tp)
    zeros = (jnp.zeros((bp, B_HEADS, B_HEAD_DIM, B_HEAD_DIM), F32),
             jnp.zeros((bp, B_HEADS, B_HEAD_DIM), F32), jnp.zeros((bp, B_HEADS), F32))
    xp, xp_bf, st_p = _finish_even(xp, bp, tp, zm_p, attn_p, gates_p, w_out_a, w_out_b,
                                   norm_g, ln_g, ln_b, zeros, math.gcd(tp, MLSTM_PROMPT_CHUNK))
    xs, xs_bf, st_s = _finish_even(xs, bs, ts, zm_s, attn_s, gates_s, w_out_a, w_out_b,
                                   norm_g, ln_g, ln_b, state_s, math.gcd(ts, MLSTM_CHUNK))
    return (xp, xp_bf, kv_p, st_p), (xs, xs_bf, kv_s, st_s)


def _odd_layer(x, x_bf, bsz, t, w_in_all, o, w_out, conv_w, ln_g, ln_b, buf):
    tm, tm_out = _row_tiles(bsz * t)
    if buf is None:
        tm = math.gcd(2 * tm, t)
        mix, tail = _odd_mix(x_bf, w_in_all, o, conv_w, tm, 256, t)
        seq_tiles = t // tm
        new_buf = tail.reshape(bsz, seq_tiles, SUBLANES, C_WIDTH)[:, -1, SUBLANES - (CONV_K - 1):, :]
    else:
        plast = jnp.repeat(buf[:, 1, :], t, axis=0)
        pprev = jnp.repeat(buf[:, 0, :], t, axis=0)
        mix, u = _odd_mix(x_bf, w_in_all, o, conv_w, tm, 256, t, state=(plast, pprev))
        new_buf = u.reshape(bsz, t, C_WIDTH)[:, t - (CONV_K - 1):, :]
    xn, xn_bf = _out_ln([mix], [w_out], x, ln_g, ln_b, tm_out)
    return xn, xn_bf, new_buf


def kernel(x_prompt, x_sample, cache_k, cache_v, page_table, state_C, state_n, state_m, state_conv,
           w_in_even, w_out_even, mlstm_gate_bias, mlstm_norm_g, rel_bias,
           w_in_odd, w_out_odd, conv_w, ln_g, ln_b):
    bp, tp, d = x_prompt.shape
    bs, ts, _ = x_sample.shape
    bias_p = _bias_tiles(rel_bias, _prompt_bias_idx())
    bias_s = _bias_tiles(rel_bias, _sample_bias_idx(ts)).reshape(A_HEADS * ts, -1)

    xp = x_prompt.reshape(bp * tp, d)
    xs = x_sample.reshape(bs * ts, d)
    xp_bf = xs_bf = None
    wt_even = jnp.swapaxes(w_in_even, 1, 2)

    kv_p = kv_s = (None, None)
    pc, pn, pm, pb = [], [], [], []
    sc, sn, sm, sb = [], [], [], []
    for layer in range(DEPTH):
        if layer % 2 == 0:
            e = layer // 2
            gbias = jnp.pad(mlstm_gate_bias[e], (0, LANES - 2 * B_HEADS)).reshape(1, LANES).astype(F32)
            w_out_a = w_out_even[e, :A_WIDTH].astype(BF16)
            w_out_b = w_out_even[e, A_WIDTH:].astype(BF16)
            (xp, xp_bf, kv_p, (c1, n1, m1)), (xs, xs_bf, kv_s, (c2, n2, m2)) = _even_layers(
                xp, xp_bf, bp, tp, xs, xs_bf, bs, ts, wt_even, e, gbias, w_out_a, w_out_b,
                mlstm_norm_g[e], ln_g[layer], ln_b[layer], bias_p, bias_s, cache_k, cache_v, page_table,
                (state_C[e], state_n[e], state_m[e]), kv_p, kv_s)
            pc.append(c1); pn.append(n1); pm.append(m1)
            sc.append(c2); sn.append(n2); sm.append(m2)
        else:
            o = layer // 2
            w_out = w_out_odd[o].astype(BF16)
            xp, xp_bf, b1 = _odd_layer(xp, xp_bf, bp, tp, w_in_odd, o, w_out, conv_w[o],
                                       ln_g[layer], ln_b[layer], None)
            xs, xs_bf, b2 = _odd_layer(xs, xs_bf, bs, ts, w_in_odd, o, w_out, conv_w[o],
                                       ln_g[layer], ln_b[layer], state_conv[o])
            pb.append(b1); sb.append(b2)
    n_even = w_in_even.shape[0]
    heads = lambda st, b, t: st.reshape(n_even, b, t, A_HEADS, A_HEAD_DIM)
    return (xp.reshape(bp, tp, d), xs.reshape(bs, ts, d),
            heads(kv_p[0], bp, tp), heads(kv_p[1], bp, tp),
            jnp.stack(pc), jnp.stack(pn), jnp.stack(pm), jnp.stack(pb),
            heads(kv_s[0], bs, ts), heads(kv_s[1], bs, ts),
            jnp.stack(sc), jnp.stack(sn), jnp.stack(sm), jnp.stack(sb))
```

```python
import functools
import math

import numpy as np
import jax
import jax.numpy as jnp
from jax import lax
from jax.experimental import pallas as pl
from jax.experimental.pallas import tpu as pltpu

F32 = jnp.float32
BF16 = jnp.bfloat16

D_MODEL = 2048
DEPTH = 4
PAGE_SIZE = 128
A_HEADS = 8
A_HEAD_DIM = 128
A_WIDTH = A_HEADS * A_HEAD_DIM
MOBA_BLOCK = 256
MOBA_TOPK = 3
REL_BUCKETS = 32
REL_MAX_DIST = 128
B_HEADS = 4
B_HEAD_DIM = 256
B_WIDTH = B_HEADS * B_HEAD_DIM
MLSTM_CHUNK = 64
MLSTM_PROMPT_CHUNK = 256
C_WIDTH = D_MODEL
CONV_K = 3
N_GROUPS_EVEN = 9
EVEN_MAIN = N_GROUPS_EVEN * 1024
ALPHA = (2.0 * DEPTH) ** 0.25
LN_EPS = 1e-5

LANES = 128
SUBLANES = 8
VMEM_LIMIT = 56 * 1024 * 1024

NEG = -1e30


def _params(*sem):
    return pltpu.CompilerParams(dimension_semantics=sem, vmem_limit_bytes=VMEM_LIMIT)


def _nt(a, b):
    return lax.dot_general(a, b, (((1,), (1,)), ((), ())), preferred_element_type=F32)


def _silu(x):
    return x * jax.nn.sigmoid(x)


def _bucket_np(d):
    d = np.asarray(d, np.int64)
    n = np.maximum(d, 0)
    max_exact = REL_BUCKETS // 2
    nf = np.maximum(n, 1).astype(np.float64)
    large = max_exact + (np.log(nf / max_exact) / math.log(REL_MAX_DIST / max_exact)
                         * (REL_BUCKETS - max_exact)).astype(np.int64)
    large = np.minimum(large, REL_BUCKETS - 1)
    out = np.where(n < max_exact, n, large)
    return np.where(d < 0, -1, out).astype(np.int32)


def _bias_kernel(tab_ref, idx_ref, o_ref):
    h = pl.program_id(0)
    idx = idx_ref[...]
    out = jnp.full(idx.shape, NEG, F32)
    for b in range(REL_BUCKETS):
        out = jnp.where(idx == b, tab_ref[h, b], out)
    o_ref[...] = out


def _bias_tiles(rel_bias, idx):
    r, c = idx.shape
    return pl.pallas_call(
        _bias_kernel,
        grid=(A_HEADS,),
        in_specs=[pl.BlockSpec(memory_space=pltpu.SMEM),
                  pl.BlockSpec((r, c), lambda h: (0, 0))],
        out_specs=pl.BlockSpec((None, r, c), lambda h: (h, 0, 0)),
        out_shape=jax.ShapeDtypeStruct((A_HEADS, r, c), F32),
        compiler_params=_params("parallel"),
        name="bias_tiles",
    )(rel_bias.astype(F32), jnp.asarray(idx))


def _prompt_bias_idx():
    qi = np.arange(MOBA_BLOCK)[:, None]
    kj = np.arange(2 * MOBA_BLOCK)[None, :]
    return _bucket_np(qi + MOBA_BLOCK - kj)


def _sample_bias_idx(s_len):
    s = np.arange(s_len)[:, None]
    lane = np.arange(2 * PAGE_SIZE * A_HEADS)[None, :]
    off = lane // A_HEADS
    far = _bucket_np(MOBA_BLOCK + s - off)
    lo = np.arange(LANES)[None, :]
    own = _bucket_np(np.where(lo < A_HEADS * s_len, s - (lo % s_len), -1))
    return np.concatenate([far, own], axis=1)


def _inproj_prep(w_ref, wb_ref):
    @pl.when(pl.program_id(1) == 0)
    def _():
        wb_ref[...] = w_ref[...].astype(BF16)


def _inproj_compute(x_ref, o_ref, oh_ref, wb_ref, slot, part=0, n_parts=1):
    tm = x_ref.shape[0]
    nf = wb_ref.shape[0] // n_parts
    cols = pl.ds(part * nf, nf)
    y = _nt(x_ref[...], wb_ref[cols, :])
    o_ref[:, cols] = y.astype(o_ref.dtype)
    if oh_ref is not None:
        nh = nf // A_HEAD_DIM
        yh = y.reshape(tm, nh, A_HEAD_DIM)
        if slot is None:
            oh_ref[:, pl.ds(part * nh, nh), :] = yh
        else:
            oh_ref[slot, :, pl.ds(part * nh, nh), :] = yh
            if part == 0:
                for other in range(oh_ref.shape[0]):
                    if other != slot:
                        oh_ref[other] = jnp.zeros(oh_ref.shape[1:], oh_ref.dtype)


def _inproj_kernel(x_ref, w_ref, *rest, heads_out, aliased, slot):
    if aliased:
        rest = rest[1:]
    if heads_out:
        o_ref, oh_ref, wb_ref = rest
    else:
        (o_ref, wb_ref), oh_ref = rest, None
    _inproj_prep(w_ref, wb_ref)
    _inproj_compute(x_ref, o_ref, oh_ref, wb_ref, slot)


def _heads_out_spec(heads, e, m, tm, index_map):
    n_layers, stack = heads
    shape = jax.ShapeDtypeStruct((n_layers, m, A_HEADS, A_HEAD_DIM), F32)
    if stack is None:
        return pl.BlockSpec((n_layers, tm, A_HEADS, A_HEAD_DIM), index_map(0)), shape, None, e
    return pl.BlockSpec((None, tm, A_HEADS, A_HEAD_DIM), index_map(e)), shape, stack, None


def _inproj_even(x_bf, wt_all, e, groups, tm, heads=None):
    m, k = x_bf.shape
    gw = 1024
    first, skip, n_groups = groups
    gmap = lambda j: first + j + skip * jnp.minimum(j, 1)
    out_specs = [pl.BlockSpec((None, tm, gw), lambda j, i: (j, i, 0))]
    out_shape = [jax.ShapeDtypeStruct((n_groups, m, gw), BF16)]
    in_specs = [pl.BlockSpec((tm, k), lambda j, i: (i, 0)),
                pl.BlockSpec((None, gw, k), lambda j, i: (e, gmap(j), 0))]
    args = [x_bf, wt_all]
    aliases = {}
    slot = None
    if heads is not None:
        assert n_groups == 1
        spec, shape, stack, slot = _heads_out_spec(heads, e, m, tm, lambda l: (lambda j, i: (l, i, 0, 0)))
        out_specs.append(spec)
        out_shape.append(shape)
        if stack is not None:
            in_specs.append(pl.BlockSpec(memory_space=pl.ANY))
            args.append(stack)
            aliases = {len(args) - 1: 1}
    return pl.pallas_call(
        functools.partial(_inproj_kernel, heads_out=heads is not None, aliased=bool(aliases), slot=slot),
        grid=(n_groups, m // tm),
        in_specs=in_specs,
        out_specs=out_specs,
        out_shape=out_shape,
        scratch_shapes=[pltpu.VMEM((gw, k), BF16)],
        input_output_aliases=aliases,
        compiler_params=_params("parallel", "arbitrary"),
        name="inproj_even",
    )(*args)


def _gates_kernel(x_ref, w_ref, b_ref, o_ref, *xb_ref):
    w = w_ref[...]
    wp = jnp.concatenate([w, jnp.zeros((LANES - w.shape[0], w.shape[1]), F32)], axis=0)
    xb = x_ref[...].astype(BF16)
    o_ref[...] = _nt(xb, wp.astype(BF16)) + b_ref[...]
    if xb_ref:
        xb_ref[0][...] = xb


def _gates(x, wt_all, e, gbias, tm):
    m, k = x.shape
    ng = 2 * B_HEADS
    assert ng == SUBLANES and EVEN_MAIN % ng == 0
    out_specs = [pl.BlockSpec((tm, LANES), lambda i: (i, 0))]
    out_shape = [jax.ShapeDtypeStruct((m, LANES), F32)]
    if x.dtype != BF16:
        out_specs.append(pl.BlockSpec((tm, k), lambda i: (i, 0)))
        out_shape.append(jax.ShapeDtypeStruct((m, k), BF16))
    outs = pl.pallas_call(
        _gates_kernel,
        grid=(m // tm,),
        in_specs=[pl.BlockSpec((tm, k), lambda i: (i, 0)),
                  pl.BlockSpec((None, ng, k), lambda i: (e, EVEN_MAIN // ng, 0)),
                  pl.BlockSpec((1, LANES), lambda i: (0, 0))],
        out_specs=out_specs,
        out_shape=out_shape,
        compiler_params=_params("parallel"),
        name="mlstm_gates",
    )(x, wt_all, gbias)
    return outs[0], (outs[1] if len(outs) > 1 else x)


def _moba_prompt_kernel(q_ref, k_ref, v_ref, g_ref, bias_ref, o_ref, kmean_ref, negm_ref,
                        *, nblk, scale):
    blk = MOBA_BLOCK
    for j in range(nblk):
        kmean_ref[j:j + 1, :] = jnp.mean(k_ref[j * blk:(j + 1) * blk, :].astype(F32), axis=0, keepdims=True)
    far_bias = bias_ref[0:1, 0:1]
    eye = (lax.broadcasted_iota(jnp.int32, (blk, blk), 0)
           == lax.broadcasted_iota(jnp.int32, (blk, blk), 1)).astype(BF16)
    rowi = lax.broadcasted_iota(jnp.int32, (nblk, blk), 0)

    i0 = MOBA_TOPK + 1
    if i0 < nblk:
        nq = (nblk - i0) * blk
        gt = lax.dot_general(kmean_ref[...], q_ref[i0 * blk:, :].astype(F32), (((1,), (1,)), ((), ())),
                             precision=lax.Precision.HIGHEST, preferred_element_type=F32)
        rowq = lax.broadcasted_iota(jnp.int32, (nblk, nq), 0)
        own = i0 + (lax.broadcasted_iota(jnp.int32, (nblk, nq), 1) >> (blk.bit_length() - 1))
        valid = rowq < own
        sel_t = jnp.zeros((nblk, nq), F32)
        for j in range(nblk - 1):
            gj = gt[j:j + 1, :]
            beats = ((gt > gj) | ((gt == gj) & (rowq < j))) & valid
            rank = jnp.sum(beats.astype(F32), axis=0, keepdims=True)
            sel_t = jnp.where(rowq == j, (rank < MOBA_TOPK).astype(F32), sel_t)
        sel_t = jnp.concatenate([sel_t, jnp.zeros((LANES - nblk, nq), F32)], axis=0).astype(BF16)
        for i in range(i0, nblk):
            cs = (i - i0) * blk
            sel = _nt(eye, sel_t[:, cs:cs + blk])
            negm_ref[i - i0] = (sel - 1.0) * (-NEG)

    for i in range(nblk):
        q = q_ref[i * blk:(i + 1) * blk, :].astype(F32)
        negm = negm_ref[i - MOBA_TOPK - 1] if i > MOBA_TOPK else None
        w = (i + 1) * blk
        s = _nt((q * scale).astype(BF16), k_ref[0:w, :])
        pieces = []
        for j in range(i + 1):
            sj = s[:, j * blk:(j + 1) * blk]
            if j == i:
                sj = sj + bias_ref[:, blk:2 * blk]
            elif j == i - 1:
                sj = sj + bias_ref[:, 0:blk]
                if negm is not None:
                    sj = sj + negm[:, j:j + 1]
            else:
                sj = sj + (far_bias if negm is None else far_bias + negm[:, j:j + 1])
            pieces.append(sj)
        m = jnp.max(functools.reduce(jnp.maximum, pieces), axis=1, keepdims=True)
        ps = [jnp.exp(sj - m) for sj in pieces]
        l = jnp.sum(functools.reduce(jnp.add, ps), axis=1, keepdims=True)
        pcat = ps[0].astype(BF16) if i == 0 else jnp.concatenate([p.astype(BF16) for p in ps], axis=1)
        acc = jnp.dot(pcat, v_ref[0:w, :], preferred_element_type=F32)
        g = g_ref[i * blk:(i + 1) * blk, :].astype(F32)
        o_ref[i * blk:(i + 1) * blk, :] = (acc / l * _silu(g)).astype(o_ref.dtype)


def _moba_prompt(zm, zk, zv, bias_p, bsz, t):
    blk = MOBA_BLOCK
    nblk = t // blk
    assert nblk <= SUBLANES
    dh = A_HEAD_DIM
    kern = functools.partial(_moba_prompt_kernel, nblk=nblk, scale=dh ** -0.5)
    zspec = lambda g: pl.BlockSpec((None, None, t, dh), lambda b, h: (g, b, 0, h))
    return pl.pallas_call(
        kern,
        grid=(bsz, A_HEADS),
        in_specs=[zspec(0), zspec(0), zspec(0), zspec(1),
                  pl.BlockSpec((None, blk, 2 * blk), lambda b, h: (h, 0, 0))],
        out_specs=pl.BlockSpec((None, t, dh), lambda b, h: (b, 0, h)),
        out_shape=jax.ShapeDtypeStruct((bsz, t, A_WIDTH), BF16),
        scratch_shapes=[pltpu.VMEM((nblk, dh), F32),
                        pltpu.VMEM((max(nblk - MOBA_TOPK - 1, 1), blk, LANES), F32)],
        compiler_params=_params("parallel", "parallel"),
        name="moba_prompt",
    )(zm, zk, zv, zm, bias_p)


def _stack_heads(x):
    return jnp.concatenate([x[:, h * A_HEAD_DIM:(h + 1) * A_HEAD_DIM] for h in range(A_HEADS)], axis=0)


def _moba_sample_body(q_ref, kn_ref, vn_ref, ga_ref, bias_ref, k_refs, v_refs, o_ref,
                      qm_ref, gate_ref, m_ref, l_ref, oblk_ref, *, nb, bps, s_len, scale,
                      alongside=None):
    n = pl.program_id(1)
    rows = A_HEADS * s_len
    s_shift = s_len.bit_length() - 1
    far_w = PAGE_SIZE * A_HEADS

    @pl.when(n == 0)
    def _():
        qm_ref[...] = _stack_heads(q_ref[...].astype(F32))
        gate_ref[...] = jnp.full_like(gate_ref, NEG)
        m_ref[...] = jnp.full_like(m_ref, NEG)
        l_ref[...] = jnp.zeros_like(l_ref)

    qb = qm_ref[...].astype(BF16)
    row = lax.broadcasted_iota(jnp.int32, (rows, far_w), 0)
    lane = lax.broadcasted_iota(jnp.int32, (rows, far_w), 1)
    useful = (lane & (A_HEADS - 1)) == (row >> s_shift)
    lane_b = lax.broadcasted_iota(jnp.int32, (rows, LANES), 1)
    last = n == pl.num_programs(1) - 1
    far_bias = bias_ref[:, 0:1]

    def pages(ref):
        return ref[...].reshape(far_w, A_HEAD_DIM).astype(BF16)

    gate_new = gate_ref[...]
    m_new = m_ref[...]
    l_new = l_ref[...]
    for c in range(bps):
        blk = n * bps + c
        raw0 = _nt(qb, pages(k_refs[2 * c]))
        raw1 = _nt(qb, pages(k_refs[2 * c + 1]))
        if alongside is not None:
            alongside(c, bps)
        gsum = jnp.sum(jnp.where(useful, raw0 + raw1, 0.0), axis=1, keepdims=True)
        if c == bps - 1:
            last_v = (jnp.zeros((rows, far_w), jnp.int32) + blk) == nb - 1
            b0 = jnp.where(last_v, bias_ref[:, 0:far_w], far_bias)
            b1 = jnp.where(last_v, bias_ref[:, far_w:2 * far_w], far_bias)
        else:
            b0 = b1 = far_bias
        s0 = jnp.where(useful, raw0 * scale + b0, NEG)
        s1 = jnp.where(useful, raw1 * scale + b1, NEG)
        mloc = jnp.max(jnp.maximum(s0, s1), axis=1, keepdims=True)
        p0 = jnp.exp(s0 - mloc)
        p1 = jnp.exp(s1 - mloc)
        lloc = jnp.sum(p0 + p1, axis=1, keepdims=True)
        oblk_ref[blk] = (jnp.dot(p0.astype(BF16), pages(v_refs[2 * c]), preferred_element_type=F32)
                         + jnp.dot(p1.astype(BF16), pages(v_refs[2 * c + 1]), preferred_element_type=F32))
        here = lane_b == blk
        gate_new = jnp.where(here, gsum * (1.0 / MOBA_BLOCK), gate_new)
        m_new = jnp.where(here, mloc, m_new)
        l_new = jnp.where(here, lloc, l_new)
    gate_ref[...] = gate_new
    m_ref[...] = m_new
    l_ref[...] = l_new

    @pl.when(last)
    def _():
        row_b = lax.broadcasted_iota(jnp.int32, (rows, LANES), 0)
        lane_f = lane_b.astype(F32)
        g = jnp.where(lane_b < nb, gate_ref[...], -jnp.inf)
        sel = jnp.zeros((rows, LANES), jnp.bool_)
        for _r in range(min(MOBA_TOPK, nb)):
            mx = jnp.max(g, axis=1, keepdims=True)
            idx = jnp.min(jnp.where(g == mx, lane_f, float(LANES)), axis=1, keepdims=True)
            hit = lane_f == idx
            sel = sel | hit
            g = jnp.where(hit, -jnp.inf, g)
        pad = jnp.zeros((LANES - rows, A_HEAD_DIM), F32)
        kn = jnp.concatenate([_stack_heads(kn_ref[...].astype(F32)), pad], axis=0).astype(BF16)
        vn = jnp.concatenate([_stack_heads(vn_ref[...].astype(F32)), pad], axis=0).astype(BF16)
        so = _nt(qb, kn) * scale + bias_ref[:, 2 * far_w:2 * far_w + LANES]
        ok = ((lane_b < rows) & ((lane_b >> s_shift) == (row_b >> s_shift))
              & ((lane_b & (s_len - 1)) <= (row_b & (s_len - 1))))
        so = jnp.where(ok, so, NEG)
        mm = m_ref[...]
        mtot = jnp.maximum(jnp.max(jnp.where(sel, mm, NEG), axis=1, keepdims=True),
                           jnp.max(so, axis=1, keepdims=True))
        w = jnp.where(sel, jnp.exp(mm - mtot), 0.0)
        po = jnp.exp(so - mtot)
        lsum = jnp.sum(w * l_ref[...], axis=1, keepdims=True) + jnp.sum(po, axis=1, keepdims=True)
        acc = jnp.dot(po.astype(BF16), vn, preferred_element_type=F32)
        for nn in range(nb):
            acc = acc + w[:, nn:nn + 1] * oblk_ref[nn]
        out = acc / lsum
        res = jnp.concatenate([out[h * s_len:(h + 1) * s_len, :] for h in range(A_HEADS)], axis=1)
        o_ref[...] = (res * _silu(ga_ref[...].astype(F32))).astype(o_ref.dtype)


def _inproj_attn_kernel(pt_ref, x_ref, w_ref, q_ref, kn_ref, vn_ref, ga_ref, bias_ref, *rest,
                        heads_out, aliased, slot, nb, bps, s_len, scale):
    del pt_ref
    npg = 2 * bps
    k_refs = rest[:npg]
    v_refs = rest[npg:2 * npg]
    rest = rest[2 * npg + (1 if aliased else 0):]
    if heads_out:
        o_ref, oh_ref, ao_ref, wb_ref = rest[:4]
    else:
        (o_ref, ao_ref, wb_ref), oh_ref = rest[:3], None
    _inproj_prep(w_ref, wb_ref)
    _moba_sample_body(q_ref, kn_ref, vn_ref, ga_ref, bias_ref, k_refs, v_refs, ao_ref, *rest[-5:],
                      nb=nb, bps=bps, s_len=s_len, scale=scale,
                      alongside=functools.partial(_inproj_compute, x_ref, o_ref, oh_ref, wb_ref, slot))


def _inproj_even_with_attn(x_bf, wt_all, e, groups, tm, heads, sample, b0):
    zm, zk, zv, bias_s, cache_k, cache_v, page_table, s_len = sample
    m, k = x_bf.shape
    gw = 1024
    first, skip, n_groups = groups
    nsteps = m // tm
    n_pages = page_table.shape[1]
    ppb = MOBA_BLOCK // PAGE_SIZE
    assert ppb == 2 and n_pages % ppb == 0 and s_len == SUBLANES and A_HEADS == SUBLANES
    nb = n_pages // ppb
    assert nb <= LANES and nb % nsteps == 0
    bps = nb // nsteps
    rows = A_HEADS * s_len
    dh = A_HEAD_DIM
    gmap = lambda j: first + j + skip * jnp.minimum(j, 1)
    tok = lambda g: pl.BlockSpec((None, None, s_len, A_WIDTH), lambda j, i, pt: (g, b0 + j, 0, 0))
    page = lambda o: pl.BlockSpec((None, None, PAGE_SIZE, A_HEADS, dh),
                                  lambda j, i, pt: (e, pt[b0 + j, ppb * bps * i + o], 0, 0, 0))
    pages = [page(o) for o in range(ppb * bps)]
    out_specs = [pl.BlockSpec((None, tm, gw), lambda j, i, pt: (j, i, 0))]
    out_shape = [jax.ShapeDtypeStruct((n_groups, m, gw), BF16)]
    args = [page_table, x_bf, wt_all, zm, zk, zv, zm, bias_s,
            *([cache_k] * (ppb * bps)), *([cache_v] * (ppb * bps))]
    extra_specs = []
    aliases = {}
    slot = None
    if heads is not None:
        assert n_groups == 1
        spec, shape, stack, slot = _heads_out_spec(heads, e, m, tm,
                                                   lambda l: (lambda j, i, pt: (l, i, 0, 0)))
        out_specs.append(spec)
        out_shape.append(shape)
        if stack is not None:
            extra_specs.append(pl.BlockSpec(memory_space=pl.ANY))
            args.append(stack)
            aliases = {len(args) - 1: 1}
    out_specs.append(pl.BlockSpec((None, s_len, A_WIDTH), lambda j, i, pt: (j, 0, 0)))
    out_shape.append(jax.ShapeDtypeStruct((n_groups, s_len, A_WIDTH), BF16))
    grid_spec = pltpu.PrefetchScalarGridSpec(
        num_scalar_prefetch=1,
        grid=(n_groups, nsteps),
        in_specs=[pl.BlockSpec((tm, k), lambda j, i, pt: (i, 0)),
                  pl.BlockSpec((None, gw, k), lambda j, i, pt: (e, gmap(j), 0)),
                  tok(0), tok(0), tok(0), tok(1),
                  pl.BlockSpec(bias_s.shape, lambda j, i, pt: (0, 0))] + pages + pages + extra_specs,
        out_specs=out_specs,
        scratch_shapes=[pltpu.VMEM((gw, k), BF16),
                        pltpu.VMEM((rows, dh), F32),
                        pltpu.VMEM((rows, LANES), F32),
                        pltpu.VMEM((rows, LANES), F32),
                        pltpu.VMEM((rows, LANES), F32),
                        pltpu.VMEM((nb, rows, dh), F32)])
    kern = functools.partial(_inproj_attn_kernel, heads_out=heads is not None, aliased=bool(aliases),
                             slot=slot, nb=nb, bps=bps, s_len=s_len, scale=dh ** -0.5)
    return pl.pallas_call(
        kern,
        grid_spec=grid_spec,
        out_shape=out_shape,
        input_output_aliases=aliases,
        compiler_params=_params("arbitrary", "arbitrary"),
        name="inproj_even_sample_attn",
    )(*args)


def _mlstm_kernel(q_ref, k_ref, v_ref, og_ref, gm_ref, g_ref, gt_ref, ng_ref, c0_ref, n0_ref, m0_ref,
                  h_ref, c_out, n_out, m_out, c_scr, n_scr, m_scr, *, L, Lp):
    c = pl.program_id(1)
    dk = B_HEAD_DIM

    @pl.when(c == 0)
    def _():
        c_scr[...] = c0_ref[...]
        n_scr[...] = n0_ref[...]
        m_scr[...] = m0_ref[...]

    def padded(x):
        if L == Lp:
            return x
        return jnp.concatenate([x, jnp.zeros((Lp - L, x.shape[1]), x.dtype)], axis=0)

    g = padded(g_ref[...])
    gt = gt_ref[...]
    r = lax.broadcasted_iota(jnp.int32, (Lp, Lp), 0)
    cc = lax.broadcasted_iota(jnp.int32, (Lp, Lp), 1)
    tri = r >= cc
    tri_t = r <= cc
    rv = lax.broadcasted_iota(jnp.int32, (Lp, 1), 0) < L
    cv = lax.broadcasted_iota(jnp.int32, (1, Lp), 1) < L

    heads = range(B_HEADS)
    cols = [slice(hh * dk, (hh + 1) * dk) for hh in heads]
    ph = []
    for hh in heads:
        q = padded(q_ref[:, cols[hh]].astype(F32))
        k = padded(k_ref[:, cols[hh]].astype(F32)) * (dk ** -0.5)
        v = padded(v_ref[:, cols[hh]].astype(F32))
        li_col = g[:, hh:hh + 1]
        li_row = gt[hh:hh + 1, :]
        lf_col = jax.nn.log_sigmoid(g[:, hh + B_HEADS:hh + B_HEADS + 1])
        lf_row = jax.nn.log_sigmoid(gt[hh + B_HEADS:hh + B_HEADS + 1, :])
        if L != Lp:
            li_col = jnp.where(rv, li_col, NEG)
            lf_col = jnp.where(rv, lf_col, 0.0)
            li_row = jnp.where(cv, li_row, NEG)
            lf_row = jnp.where(cv, lf_row, 0.0)
        b_col = jnp.sum(jnp.where(tri, lf_row, 0.0), axis=1, keepdims=True)
        b_row = jnp.sum(jnp.where(tri_t, lf_col, 0.0), axis=0, keepdims=True)
        m_prev = m_scr[hh]
        log_d = jnp.where(tri, b_col - b_row + li_row, NEG)
        m_inter = b_col + m_prev
        m_t = jnp.maximum(m_inter, jnp.max(log_d, axis=1, keepdims=True))
        ph.append(dict(q=q, k=k, vb=v.astype(BF16), li_col=li_col, li_row=li_row, b_col=b_col,
                       b_row=b_row, m_prev=m_prev, log_d=log_d, m_inter=m_inter, m_t=m_t))
    for hh in heads:
        p = ph[hh]
        qb = p['q'].astype(BF16)
        s = _nt(qb, p['k'].astype(BF16)) * jnp.exp(p['log_d'] - p['m_t'])
        w_inter = jnp.exp(p['m_inter'] - p['m_t'])
        c_old = c_scr[hh]
        n_old = n_scr[hh]
        num = (jnp.dot(s.astype(BF16), p['vb'], preferred_element_type=F32)
               + w_inter * jnp.dot(qb, c_old.astype(BF16), preferred_element_type=F32))
        den = jnp.sum(s, axis=1, keepdims=True) + w_inter * jnp.sum(p['q'] * n_old, axis=1, keepdims=True)
        p.update(h=num / jnp.maximum(jnp.abs(den), jnp.exp(-p['m_t'])), c_old=c_old, n_old=n_old)
    for hh in heads:
        p = ph[hh]
        b_last = p['b_row'][:, L - 1:L]
        m_new = jnp.maximum(b_last + p['m_prev'],
                            jnp.max(b_last - p['b_row'] + p['li_row'], axis=1, keepdims=True))
        decay = jnp.exp(b_last + p['m_prev'] - m_new)
        wk = jnp.exp(b_last - p['b_col'] + p['li_col'] - m_new)
        kw = p['k'] * wk
        c_scr[hh] = decay * p['c_old'] + lax.dot_general(kw.astype(BF16), p['vb'], (((0,), (0,)), ((), ())),
                                                         preferred_element_type=F32)
        n_scr[hh] = decay * p['n_old'] + jnp.sum(kw, axis=0, keepdims=True)
        m_scr[hh] = m_new
    for hh in heads:
        hv = ph[hh]['h'][:L]
        mu = jnp.mean(hv, axis=1, keepdims=True)
        var = jnp.mean(jnp.square(hv - mu), axis=1, keepdims=True)
        hn = (hv - mu) * lax.rsqrt(var + LN_EPS) * ng_ref[hh]
        h_ref[:, cols[hh]] = (hn * jax.nn.sigmoid(og_ref[:, cols[hh]].astype(F32))
                              * _silu(gm_ref[:, cols[hh]].astype(F32))).astype(h_ref.dtype)

    @pl.when(c == pl.num_programs(1) - 1)
    def _():
        c_out[...] = c_scr[...]
        n_out[...] = n_scr[...]
        m_out[...] = m_scr[...]


def _mlstm(zm, gates, norm_g, c0, n0, m0, bsz, t, L):
    Lp = max(L, LANES)
    nc = t // L
    dh = B_HEAD_DIM
    g3 = gates.reshape(bsz, t, LANES)
    gt = jnp.swapaxes(gates[:, :2 * B_HEADS].reshape(bsz, nc, L, 2 * B_HEADS), 2, 3)
    if Lp != L:
        gt = jnp.pad(gt, ((0, 0), (0, 0), (0, 0), (0, Lp - L)))
    ng = norm_g.reshape(B_HEADS, 1, dh).astype(F32)
    n0 = n0.reshape(bsz, B_HEADS, 1, dh)
    m0 = m0.reshape(bsz, B_HEADS, 1, 1)
    kern = functools.partial(_mlstm_kernel, L=L, Lp=Lp)
    zspec = lambda gidx: pl.BlockSpec((None, None, L, B_WIDTH), lambda b, c: (gidx, b, c, 0))
    st = lambda r, cdim: pl.BlockSpec((None, B_HEADS, r, cdim), lambda b, c: (b, 0, 0, 0))
    h, c1, n1, m1 = pl.pallas_call(
        kern,
        grid=(bsz, nc),
        in_specs=[zspec(2), zspec(3), zspec(4), zspec(5), zspec(6),
                  pl.BlockSpec((None, L, LANES), lambda b, c: (b, c, 0)),
                  pl.BlockSpec((None, None, 2 * B_HEADS, Lp), lambda b, c: (b, c, 0, 0)),
                  pl.BlockSpec((B_HEADS, 1, dh), lambda b, c: (0, 0, 0)),
                  st(dh, dh), st(1, dh), st(1, 1)],
        out_specs=[pl.BlockSpec((None, L, B_WIDTH), lambda b, c: (b, c, 0)),
                   st(dh, dh), st(1, dh), st(1, 1)],
        out_shape=[jax.ShapeDtypeStruct((bsz, t, B_WIDTH), BF16),
                   jax.ShapeDtypeStruct((bsz, B_HEADS, dh, dh), F32),
                   jax.ShapeDtypeStruct((bsz, B_HEADS, 1, dh), F32),
                   jax.ShapeDtypeStruct((bsz, B_HEADS, 1, 1), F32)],
        scratch_shapes=[pltpu.VMEM((B_HEADS, dh, dh), F32), pltpu.VMEM((B_HEADS, 1, dh), F32),
                        pltpu.VMEM((B_HEADS, 1, 1), F32)],
        compiler_params=_params("parallel", "arbitrary"),
        name="mlstm",
    )(zm, zm, zm, zm, zm, g3, gt, ng, c0, n0, m0)
    return h, c1, n1.reshape(bsz, B_HEADS, dh), m1.reshape(bsz, B_HEADS)


def _outln_kernel(*refs, n_parts):
    mix = refs[:n_parts]
    ws = refs[n_parts:2 * n_parts]
    x_ref, g_ref, b_ref, o_ref, ob_ref = refs[2 * n_parts:]
    tm = x_ref.shape[0]
    sub = min(tm, LANES)
    for r in range(tm // sub):
        rows = pl.ds(r * sub, sub)
        y = jnp.dot(mix[0][rows, :], ws[0][...], preferred_element_type=F32)
        for p in range(1, n_parts):
            y = y + jnp.dot(mix[p][rows, :], ws[p][...], preferred_element_type=F32)
        z = ALPHA * x_ref[rows, :] + y
        mu = jnp.mean(z, axis=1, keepdims=True)
        var = jnp.mean(jnp.square(z - mu), axis=1, keepdims=True)
        xn = (z - mu) * lax.rsqrt(var + LN_EPS) * g_ref[...] + b_ref[...]
        o_ref[rows, :] = xn
        ob_ref[rows, :] = xn.astype(BF16)


def _out_ln(mix_parts, w_parts, x, ln_g, ln_b, tm):
    m, d = x.shape
    n_parts = len(mix_parts)
    in_specs = ([pl.BlockSpec((tm, mp.shape[1]), lambda i: (i, 0)) for mp in mix_parts]
                + [pl.BlockSpec(wp.shape, lambda i: (0, 0)) for wp in w_parts]
                + [pl.BlockSpec((tm, d), lambda i: (i, 0)),
                   pl.BlockSpec((1, d), lambda i: (0, 0)),
                   pl.BlockSpec((1, d), lambda i: (0, 0))])
    return pl.pallas_call(
        functools.partial(_outln_kernel, n_parts=n_parts),
        grid=(m // tm,),
        in_specs=in_specs,
        out_specs=[pl.BlockSpec((tm, d), lambda i: (i, 0)), pl.BlockSpec((tm, d), lambda i: (i, 0))],
        out_shape=[jax.ShapeDtypeStruct((m, d), F32), jax.ShapeDtypeStruct((m, d), BF16)],
        compiler_params=_params("parallel"),
        name="outproj_ln",
    )(*mix_parts, *w_parts, x, ln_g.reshape(1, d).astype(F32), ln_b.reshape(1, d).astype(F32))


def _odd_kernel(*refs, tm, carry_mode, seq_tiles, seq_rows):
    if carry_mode:
        x_ref, wb, wc, wx, wg, cw_ref, mix_ref, tail_ref, wbf, carry = refs
    else:
        x_ref, wb, wc, wx, wg, cw_ref, plast_ref, pprev_ref, mix_ref, tail_ref, wbf = refs

    @pl.when(pl.program_id(1) == 0)
    def _():
        for gidx, wref in enumerate((wb, wc, wx, wg)):
            wbf[gidx] = wref[...].astype(BF16)

    if carry_mode:
        @pl.when(pl.program_id(1) % seq_tiles == 0)
        def _():
            carry[...] = jnp.zeros_like(carry)
        last = carry[SUBLANES - 1:SUBLANES, :]
        prev = carry[SUBLANES - 2:SUBLANES - 1, :]

    cw = cw_ref[...]
    sub = min(tm, 4 * LANES)
    for r in range(tm // sub):
        rs = pl.ds(r * sub, sub)
        x = x_ref[rs, :]
        zb = jnp.dot(x, wbf[0], preferred_element_type=F32)
        zc = jnp.dot(x, wbf[1], preferred_element_type=F32)
        zx = jnp.dot(x, wbf[2], preferred_element_type=F32)
        zg = jnp.dot(x, wbf[3], preferred_element_type=F32)
        u = zc * zx
        rows = lax.broadcasted_iota(jnp.int32, u.shape, 0)
        r1 = pltpu.roll(u, 1, 0)
        r2 = pltpu.roll(u, 2, 0)
        if carry_mode:
            rm = rows
        else:
            last = plast_ref[rs, :]
            prev = pprev_ref[rs, :]
            rm = rows & (seq_rows - 1)
        u1 = jnp.where(rm == 0, last, r1)
        u2 = jnp.where(rm == 0, prev, jnp.where(rm == 1, last, r2))
        y = cw[0:1, :] * u2 + cw[1:2, :] * u1 + cw[2:3, :] * u
        mix_ref[rs, :] = (zb * y * _silu(zg)).astype(mix_ref.dtype)
        if carry_mode:
            last = u[sub - 1:sub, :]
            prev = u[sub - 2:sub - 1, :]
        else:
            tail_ref[rs, :] = u
    if carry_mode:
        tail = u[sub - SUBLANES:, :]
        carry[...] = tail
        tail_ref[...] = tail


def _odd_mix(x_bf, w_all, o, conv_w, tm, tn, seq_len, state=None):
    m, k = x_bf.shape
    nn = C_WIDTH // tn
    carry_mode = state is None
    wspec = lambda gidx: pl.BlockSpec((None, k, tn), lambda j, i: (o, 0, gidx * nn + j))
    in_specs = [pl.BlockSpec((tm, k), lambda j, i: (i, 0)),
                wspec(0), wspec(1), wspec(2), wspec(3),
                pl.BlockSpec((CONV_K, tn), lambda j, i: (0, j))]
    args = [x_bf, w_all, w_all, w_all, w_all, conv_w.astype(F32)]
    scratch = [pltpu.VMEM((4, k, tn), BF16)]
    if carry_mode:
        assert seq_len % tm == 0
        tail_shape = jax.ShapeDtypeStruct((m // tm, SUBLANES, C_WIDTH), F32)
        tail_spec = pl.BlockSpec((None, SUBLANES, tn), lambda j, i: (i, 0, j))
        scratch.append(pltpu.VMEM((SUBLANES, tn), F32))
    else:
        assert tm % seq_len == 0 and seq_len & (seq_len - 1) == 0
        in_specs += [pl.BlockSpec((tm, tn), lambda j, i: (i, j))] * 2
        args += list(state)
        tail_shape = jax.ShapeDtypeStruct((m, C_WIDTH), F32)
        tail_spec = pl.BlockSpec((tm, tn), lambda j, i: (i, j))
    kern = functools.partial(_odd_kernel, tm=tm, carry_mode=carry_mode,
                             seq_tiles=max(seq_len // tm, 1), seq_rows=seq_len)
    return pl.pallas_call(
        kern,
        grid=(nn, m // tm),
        in_specs=in_specs,
        out_specs=[pl.BlockSpec((tm, tn), lambda j, i: (i, j)), tail_spec],
        out_shape=[jax.ShapeDtypeStruct((m, C_WIDTH), BF16), tail_shape],
        scratch_shapes=scratch,
        compiler_params=_params("parallel", "arbitrary"),
        name="odd_mix",
    )(*args)


def _row_tiles(m):
    return min(m, 1024), min(m, 512)


def _project_even(x_bf, wt_all, e, bsz, t, kv_stacks, sample=None):
    n_even = wt_all.shape[0]
    hk, hv = ((n_even, st) for st in kv_stacks)
    m = bsz * t
    tm, _ = _row_tiles(m)
    ng = N_GROUPS_EVEN - 2
    if sample is None:
        (zm,) = _inproj_even(x_bf, wt_all, e, (0, 2, ng), tm)
        zk, ka = _inproj_even(x_bf, wt_all, e, (1, 0, 1), tm, heads=hk)
        attn_s = None
    else:
        n_seq = sample[0].shape[1]
        assert n_seq == ng + 1
        tm_a = tm // 2
        zm, attn_a = _inproj_even_with_attn(x_bf, wt_all, e, (0, 2, ng), tm_a, None, sample, 0)
        zk, ka, attn_b = _inproj_even_with_attn(x_bf, wt_all, e, (1, 0, 1), tm_a, hk, sample, ng)
        attn_s = jnp.concatenate([attn_a, attn_b], axis=0)
    zv, va = _inproj_even(x_bf, wt_all, e, (2, 0, 1), tm, heads=hv)
    return (zm.reshape(ng, bsz, t, 1024), zk.reshape(1, bsz, t, 1024), zv.reshape(1, bsz, t, 1024),
            (ka, va), attn_s)


def _finish_even(x, bsz, t, zm, attn, gates, w_out_a, w_out_b, norm_g, ln_g, ln_b, state, L):
    _, tm_out = _row_tiles(bsz * t)
    hm, c1, n1, m1 = _mlstm(zm, gates, norm_g, *state, bsz, t, L)
    xn, xn_bf = _out_ln([attn.reshape(bsz * t, A_WIDTH), hm.reshape(bsz * t, B_WIDTH)],
                        [w_out_a, w_out_b], x, ln_g, ln_b, tm_out)
    return xn, xn_bf, (c1, n1, m1)


def _even_layers(xp, xp_bf, bp, tp, xs, xs_bf, bs, ts, wt_all, e, gbias, w_out_a, w_out_b, norm_g,
                 ln_g, ln_b, bias_p, bias_s, cache_k, cache_v, page_table, state_s, kv_p, kv_s):
    gates_p, xp_bf = _gates(xp if xp_bf is None else xp_bf, wt_all, e, gbias, _row_tiles(bp * tp)[0])
    gates_s, xs_bf = _gates(xs if xs_bf is None else xs_bf, wt_all, e, gbias, _row_tiles(bs * ts)[0])
    zm_s, zk_s, zv_s, kv_s, _ = _project_even(xs_bf, wt_all, e, bs, ts, kv_s)
    sample = (zm_s, zk_s, zv_s, bias_s, cache_k, cache_v, page_table, ts)
    zm_p, zk_p, zv_p, kv_p, attn_s = _project_even(xp_bf, wt_all, e, bp, tp, kv_p, sample)
    attn_p = _moba_prompt(zm_p, zk_p, zv_p, bias_p, bp, tp)
    zeros = (jnp.zeros((bp, B_HEADS, B_HEAD_DIM, B_HEAD_DIM), F32),
             jnp.zeros((bp, B_HEADS, B_HEAD_DIM), F32), jnp.zeros((bp, B_HEADS), F32))
    xp, xp_bf, st_p = _finish_even(xp, bp, tp, zm_p, attn_p, gates_p, w_out_a, w_out_b,
                                   norm_g, ln_g, ln_b, zeros, math.gcd(tp, MLSTM_PROMPT_CHUNK))
    xs, xs_bf, st_s = _finish_even(xs, bs, ts, zm_s, attn_s, gates_s, w_out_a, w_out_b,
                                   norm_g, ln_g, ln_b, state_s, math.gcd(ts, MLSTM_CHUNK))
    return (xp, xp_bf, kv_p, st_p), (xs, xs_bf, kv_s, st_s)


def _odd_layer(x, x_bf, bsz, t, w_in_all, o, w_out, conv_w, ln_g, ln_b, buf):
    tm, tm_out = _row_tiles(bsz * t)
    if buf is None:
        tm = math.gcd(2 * tm, t)
        mix, tail = _odd_mix(x_bf, w_in_all, o, conv_w, tm, 256, t)
        seq_tiles = t // tm
        new_buf = tail.reshape(bsz, seq_tiles, SUBLANES, C_WIDTH)[:, -1, SUBLANES - (CONV_K - 1):, :]
    else:
        plast = jnp.repeat(buf[:, 1, :], t, axis=0)
        pprev = jnp.repeat(buf[:, 0, :], t, axis=0)
        mix, u = _odd_mix(x_bf, w_in_all, o, conv_w, tm, 256, t, state=(plast, pprev))
        new_buf = u.reshape(bsz, t, C_WIDTH)[:, t - (CONV_K - 1):, :]
    xn, xn_bf = _out_ln([mix], [w_out], x, ln_g, ln_b, tm_out)
    return xn, xn_bf, new_buf


def kernel(x_prompt, x_sample, cache_k, cache_v, page_table, state_C, state_n, state_m, state_conv,
           w_in_even, w_out_even, mlstm_gate_bias, mlstm_norm_g, rel_bias,
           w_in_odd, w_out_odd, conv_w, ln_g, ln_b):
    bp, tp, d = x_prompt.shape
    bs, ts, _ = x_sample.shape
    bias_p = _bias_tiles(rel_bias, _prompt_bias_idx())
    bias_s = _bias_tiles(rel_bias, _sample_bias_idx(ts)).reshape(A_HEADS * ts, -1)

    xp = x_prompt.reshape(bp * tp, d)
    xs = x_sample.reshape(bs * ts, d)
    xp_bf = xs_bf = None
    wt_even = jnp.swapaxes(w_in_even, 1, 2)

    kv_p = kv_s = (None, None)
    pc, pn, pm, pb = [], [], [], []
    sc, sn, sm, sb = [], [], [], []
    for layer in range(DEPTH):
        if layer % 2 == 0:
            e = layer // 2
            gbias = jnp.pad(mlstm_gate_bias[e], (0, LANES - 2 * B_HEADS)).reshape(1, LANES).astype(F32)
            w_out_a = w_out_even[e, :A_WIDTH].astype(BF16)
            w_out_b = w_out_even[e, A_WIDTH:].astype(BF16)
            (xp, xp_bf, kv_p, (c1, n1, m1)), (xs, xs_bf, kv_s, (c2, n2, m2)) = _even_layers(
                xp, xp_bf, bp, tp, xs, xs_bf, bs, ts, wt_even, e, gbias, w_out_a, w_out_b,
                mlstm_norm_g[e], ln_g[layer], ln_b[layer], bias_p, bias_s, cache_k, cache_v, page_table,
                (state_C[e], state_n[e], state_m[e]), kv_p, kv_s)
            pc.append(c1); pn.append(n1); pm.append(m1)
            sc.append(c2); sn.append(n2); sm.append(m2)
        else:
            o = layer // 2
            w_out = w_out_odd[o].astype(BF16)
            xp, xp_bf, b1 = _odd_layer(xp, xp_bf, bp, tp, w_in_odd, o, w_out, conv_w[o],
                                       ln_g[layer], ln_b[layer], None)
            xs, xs_bf, b2 = _odd_layer(xs, xs_bf, bs, ts, w_in_odd, o, w_out, conv_w[o],
                                       ln_g[layer], ln_b[layer], state_conv[o])
            pb.append(b1); sb.append(b2)
    n_even = w_in_even.shape[0]
    heads = lambda st, b, t: st.reshape(n_even, b, t, A_HEADS, A_HEAD_DIM)
    return (xp.reshape(bp, tp, d), xs.reshape(bs, ts, d),
            heads(kv_p[0], bp, tp), heads(kv_p[1], bp, tp),
            jnp.stack(pc), jnp.stack(pn), jnp.stack(pm), jnp.stack(pb),
            heads(kv_s[0], bs, ts), heads(kv_s[1], bs, ts),
            jnp.stack(sc), jnp.stack(sn), jnp.stack(sm), jnp.stack(sb))
```

```python
import functools
import math

import numpy as np
import jax
import jax.numpy as jnp
from jax import lax
from jax.experimental import pallas as pl
from jax.experimental.pallas import tpu as pltpu

F32 = jnp.float32
BF16 = jnp.bfloat16

D_MODEL = 2048
DEPTH = 4
PAGE_SIZE = 128
A_HEADS = 8
A_HEAD_DIM = 128
A_WIDTH = A_HEADS * A_HEAD_DIM
MOBA_BLOCK = 256
MOBA_TOPK = 3
REL_BUCKETS = 32
REL_MAX_DIST = 128
B_HEADS = 4
B_HEAD_DIM = 256
B_WIDTH = B_HEADS * B_HEAD_DIM
MLSTM_CHUNK = 64
MLSTM_PROMPT_CHUNK = 256
C_WIDTH = D_MODEL
CONV_K = 3
N_GROUPS_EVEN = 9
EVEN_MAIN = N_GROUPS_EVEN * 1024
ALPHA = (2.0 * DEPTH) ** 0.25
LN_EPS = 1e-5

LANES = 128
SUBLANES = 8
VMEM_LIMIT = 56 * 1024 * 1024

NEG = -1e30


def _params(*sem):
    return pltpu.CompilerParams(dimension_semantics=sem, vmem_limit_bytes=VMEM_LIMIT)


def _nt(a, b):
    return lax.dot_general(a, b, (((1,), (1,)), ((), ())), preferred_element_type=F32)


def _silu(x):
    return x * jax.nn.sigmoid(x)


def _bucket_np(d):
    d = np.asarray(d, np.int64)
    n = np.maximum(d, 0)
    max_exact = REL_BUCKETS // 2
    nf = np.maximum(n, 1).astype(np.float64)
    large = max_exact + (np.log(nf / max_exact) / math.log(REL_MAX_DIST / max_exact)
                         * (REL_BUCKETS - max_exact)).astype(np.int64)
    large = np.minimum(large, REL_BUCKETS - 1)
    out = np.where(n < max_exact, n, large)
    return np.where(d < 0, -1, out).astype(np.int32)


def _bias_kernel(tab_ref, idx_ref, o_ref):
    h = pl.program_id(0)
    idx = idx_ref[...]
    out = jnp.full(idx.shape, NEG, F32)
    for b in range(REL_BUCKETS):
        out = jnp.where(idx == b, tab_ref[h, b], out)
    o_ref[...] = out


def _bias_tiles(rel_bias, idx):
    r, c = idx.shape
    return pl.pallas_call(
        _bias_kernel,
        grid=(A_HEADS,),
        in_specs=[pl.BlockSpec(memory_space=pltpu.SMEM),
                  pl.BlockSpec((r, c), lambda h: (0, 0))],
        out_specs=pl.BlockSpec((None, r, c), lambda h: (h, 0, 0)),
        out_shape=jax.ShapeDtypeStruct((A_HEADS, r, c), F32),
        compiler_params=_params("parallel"),
        name="bias_tiles",
    )(rel_bias.astype(F32), jnp.asarray(idx))


def _prompt_bias_idx():
    qi = np.arange(MOBA_BLOCK)[:, None]
    kj = np.arange(2 * MOBA_BLOCK)[None, :]
    return _bucket_np(qi + MOBA_BLOCK - kj)


def _sample_bias_idx(s_len):
    s = np.arange(s_len)[:, None]
    lane = np.arange(2 * PAGE_SIZE * A_HEADS)[None, :]
    off = lane // A_HEADS
    far = _bucket_np(MOBA_BLOCK + s - off)
    lo = np.arange(LANES)[None, :]
    own = _bucket_np(np.where(lo < A_HEADS * s_len, s - (lo % s_len), -1))
    return np.concatenate([far, own], axis=1)


def _inproj_prep(w_ref, wb_ref):
    @pl.when(pl.program_id(1) == 0)
    def _():
        wb_ref[...] = w_ref[...].astype(BF16)


def _inproj_compute(x_ref, o_ref, oh_ref, wb_ref, slot, part=0, n_parts=1):
    tm = x_ref.shape[0]
    nf = wb_ref.shape[0] // n_parts
    cols = pl.ds(part * nf, nf)
    y = _nt(x_ref[...], wb_ref[cols, :])
    o_ref[:, cols] = y.astype(o_ref.dtype)
    if oh_ref is not None:
        nh = nf // A_HEAD_DIM
        yh = y.reshape(tm, nh, A_HEAD_DIM)
        if slot is None:
            oh_ref[:, pl.ds(part * nh, nh), :] = yh
        else:
            oh_ref[slot, :, pl.ds(part * nh, nh), :] = yh
            if part == 0:
                for other in range(oh_ref.shape[0]):
                    if other != slot:
                        oh_ref[other] = jnp.zeros(oh_ref.shape[1:], oh_ref.dtype)


def _inproj_kernel(x_ref, w_ref, *rest, heads_out, aliased, slot):
    if aliased:
        rest = rest[1:]
    if heads_out:
        o_ref, oh_ref, wb_ref = rest
    else:
        (o_ref, wb_ref), oh_ref = rest, None
    _inproj_prep(w_ref, wb_ref)
    _inproj_compute(x_ref, o_ref, oh_ref, wb_ref, slot)


def _heads_out_spec(heads, e, m, tm, index_map):
    n_layers, stack = heads
    shape = jax.ShapeDtypeStruct((n_layers, m, A_HEADS, A_HEAD_DIM), F32)
    if stack is None:
        return pl.BlockSpec((n_layers, tm, A_HEADS, A_HEAD_DIM), index_map(0)), shape, None, e
    return pl.BlockSpec((None, tm, A_HEADS, A_HEAD_DIM), index_map(e)), shape, stack, None


def _inproj_even(x_bf, wt_all, e, groups, tm, heads=None):
    m, k = x_bf.shape
    gw = 1024
    first, skip, n_groups = groups
    gmap = lambda j: first + j + skip * jnp.minimum(j, 1)
    out_specs = [pl.BlockSpec((None, tm, gw), lambda j, i: (j, i, 0))]
    out_shape = [jax.ShapeDtypeStruct((n_groups, m, gw), BF16)]
    in_specs = [pl.BlockSpec((tm, k), lambda j, i: (i, 0)),
                pl.BlockSpec((None, gw, k), lambda j, i: (e, gmap(j), 0))]
    args = [x_bf, wt_all]
    aliases = {}
    slot = None
    if heads is not None:
        assert n_groups == 1
        spec, shape, stack, slot = _heads_out_spec(heads, e, m, tm, lambda l: (lambda j, i: (l, i, 0, 0)))
        out_specs.append(spec)
        out_shape.append(shape)
        if stack is not None:
            in_specs.append(pl.BlockSpec(memory_space=pl.ANY))
            args.append(stack)
            aliases = {len(args) - 1: 1}
    return pl.pallas_call(
        functools.partial(_inproj_kernel, heads_out=heads is not None, aliased=bool(aliases), slot=slot),
        grid=(n_groups, m // tm),
        in_specs=in_specs,
        out_specs=out_specs,
        out_shape=out_shape,
        scratch_shapes=[pltpu.VMEM((gw, k), BF16)],
        input_output_aliases=aliases,
        compiler_params=_params("parallel", "arbitrary"),
        name="inproj_even",
    )(*args)


def _gates_kernel(x_ref, w_ref, b_ref, o_ref, *xb_ref):
    w = w_ref[...]
    wp = jnp.concatenate([w, jnp.zeros((LANES - w.shape[0], w.shape[1]), F32)], axis=0)
    xb = x_ref[...].astype(BF16)
    o_ref[...] = _nt(xb, wp.astype(BF16)) + b_ref[...]
    if xb_ref:
        xb_ref[0][...] = xb


def _gates(x, wt_all, e, gbias, tm):
    m, k = x.shape
    ng = 2 * B_HEADS
    assert ng == SUBLANES and EVEN_MAIN % ng == 0
    out_specs = [pl.BlockSpec((tm, LANES), lambda i: (i, 0))]
    out_shape = [jax.ShapeDtypeStruct((m, LANES), F32)]
    if x.dtype != BF16:
        out_specs.append(pl.BlockSpec((tm, k), lambda i: (i, 0)))
        out_shape.append(jax.ShapeDtypeStruct((m, k), BF16))
    outs = pl.pallas_call(
        _gates_kernel,
        grid=(m // tm,),
        in_specs=[pl.BlockSpec((tm, k), lambda i: (i, 0)),
                  pl.BlockSpec((None, ng, k), lambda i: (e, EVEN_MAIN // ng, 0)),
                  pl.BlockSpec((1, LANES), lambda i: (0, 0))],
        out_specs=out_specs,
        out_shape=out_shape,
        compiler_params=_params("parallel"),
        name="mlstm_gates",
    )(x, wt_all, gbias)
    return outs[0], (outs[1] if len(outs) > 1 else x)


def _moba_prompt_kernel(q_ref, k_ref, v_ref, g_ref, bias_ref, o_ref, kmean_ref, negm_ref,
                        *, nblk, scale):
    blk = MOBA_BLOCK
    for j in range(nblk):
        kmean_ref[j:j + 1, :] = jnp.mean(k_ref[j * blk:(j + 1) * blk, :].astype(F32), axis=0, keepdims=True)
    far_bias = bias_ref[0:1, 0:1]
    eye = (lax.broadcasted_iota(jnp.int32, (blk, blk), 0)
           == lax.broadcasted_iota(jnp.int32, (blk, blk), 1)).astype(BF16)
    rowi = lax.broadcasted_iota(jnp.int32, (nblk, blk), 0)

    i0 = MOBA_TOPK + 1
    if i0 < nblk:
        nq = (nblk - i0) * blk
        gt = lax.dot_general(kmean_ref[...], q_ref[i0 * blk:, :].astype(F32), (((1,), (1,)), ((), ())),
                             precision=lax.Precision.HIGHEST, preferred_element_type=F32)
        rowq = lax.broadcasted_iota(jnp.int32, (nblk, nq), 0)
        own = i0 + (lax.broadcasted_iota(jnp.int32, (nblk, nq), 1) >> (blk.bit_length() - 1))
        valid = rowq < own
        sel_t = jnp.zeros((nblk, nq), F32)
        for j in range(nblk - 1):
            gj = gt[j:j + 1, :]
            beats = ((gt > gj) | ((gt == gj) & (rowq < j))) & valid
            rank = jnp.sum(beats.astype(F32), axis=0, keepdims=True)
            sel_t = jnp.where(rowq == j, (rank < MOBA_TOPK).astype(F32), sel_t)
        sel_t = jnp.concatenate([sel_t, jnp.zeros((LANES - nblk, nq), F32)], axis=0).astype(BF16)
        for i in range(i0, nblk):
            cs = (i - i0) * blk
            sel = _nt(eye, sel_t[:, cs:cs + blk])
            negm_ref[i - i0] = (sel - 1.0) * (-NEG)

    for i in range(nblk):
        q = q_ref[i * blk:(i + 1) * blk, :].astype(F32)
        negm = negm_ref[i - MOBA_TOPK - 1] if i > MOBA_TOPK else None
        w = (i + 1) * blk
        s = _nt((q * scale).astype(BF16), k_ref[0:w, :])
        pieces = []
        for j in range(i + 1):
            sj = s[:, j * blk:(j + 1) * blk]
            if j == i:
                sj = sj + bias_ref[:, blk:2 * blk]
            elif j == i - 1:
                sj = sj + bias_ref[:, 0:blk]
                if negm is not None:
                    sj = sj + negm[:, j:j + 1]
            else:
                sj = sj + (far_bias if negm is None else far_bias + negm[:, j:j + 1])
            pieces.append(sj)
        m = jnp.max(functools.reduce(jnp.maximum, pieces), axis=1, keepdims=True)
        ps = [jnp.exp(sj - m) for sj in pieces]
        l = jnp.sum(functools.reduce(jnp.add, ps), axis=1, keepdims=True)
        pcat = ps[0].astype(BF16) if i == 0 else jnp.concatenate([p.astype(BF16) for p in ps], axis=1)
        acc = jnp.dot(pcat, v_ref[0:w, :], preferred_element_type=F32)
        g = g_ref[i * blk:(i + 1) * blk, :].astype(F32)
        o_ref[i * blk:(i + 1) * blk, :] = (acc / l * _silu(g)).astype(o_ref.dtype)


def _moba_prompt(zm, zk, zv, bias_p, bsz, t):
    blk = MOBA_BLOCK
    nblk = t // blk
    assert nblk <= SUBLANES
    dh = A_HEAD_DIM
    kern = functools.partial(_moba_prompt_kernel, nblk=nblk, scale=dh ** -0.5)
    zspec = lambda g: pl.BlockSpec((None, None, t, dh), lambda b, h: (g, b, 0, h))
    return pl.pallas_call(
        kern,
        grid=(bsz, A_HEADS),
        in_specs=[zspec(0), zspec(0), zspec(0), zspec(1),
                  pl.BlockSpec((None, blk, 2 * blk), lambda b, h: (h, 0, 0))],
        out_specs=pl.BlockSpec((None, t, dh), lambda b, h: (b, 0, h)),
        out_shape=jax.ShapeDtypeStruct((bsz, t, A_WIDTH), BF16),
        scratch_shapes=[pltpu.VMEM((nblk, dh), F32),
                        pltpu.VMEM((max(nblk - MOBA_TOPK - 1, 1), blk, LANES), F32)],
        compiler_params=_params("parallel", "parallel"),
        name="moba_prompt",
    )(zm, zk, zv, zm, bias_p)


def _stack_heads(x):
    return jnp.concatenate([x[:, h * A_HEAD_DIM:(h + 1) * A_HEAD_DIM] for h in range(A_HEADS)], axis=0)


def _moba_sample_body(q_ref, kn_ref, vn_ref, ga_ref, bias_ref, k_refs, v_refs, o_ref,
                      qm_ref, gate_ref, m_ref, l_ref, oblk_ref, *, nb, bps, s_len, scale,
                      alongside=None):
    n = pl.program_id(1)
    rows = A_HEADS * s_len
    s_shift = s_len.bit_length() - 1
    far_w = PAGE_SIZE * A_HEADS

    @pl.when(n == 0)
    def _():
        qm_ref[...] = _stack_heads(q_ref[...].astype(F32))
        gate_ref[...] = jnp.full_like(gate_ref, NEG)
        m_ref[...] = jnp.full_like(m_ref, NEG)
        l_ref[...] = jnp.zeros_like(l_ref)

    qb = qm_ref[...].astype(BF16)
    row = lax.broadcasted_iota(jnp.int32, (rows, far_w), 0)
    lane = lax.broadcasted_iota(jnp.int32, (rows, far_w), 1)
    useful = (lane & (A_HEADS - 1)) == (row >> s_shift)
    lane_b = lax.broadcasted_iota(jnp.int32, (rows, LANES), 1)
    last = n == pl.num_programs(1) - 1
    far_bias = bias_ref[:, 0:1]

    def pages(ref):
        return ref[...].reshape(far_w, A_HEAD_DIM).astype(BF16)

    gate_new = gate_ref[...]
    m_new = m_ref[...]
    l_new = l_ref[...]
    for c in range(bps):
        blk = n * bps + c
        raw0 = _nt(qb, pages(k_refs[2 * c]))
        raw1 = _nt(qb, pages(k_refs[2 * c + 1]))
        if alongside is not None:
            alongside(c, bps)
        gsum = jnp.sum(jnp.where(useful, raw0 + raw1, 0.0), axis=1, keepdims=True)
        if c == bps - 1:
            last_v = (jnp.zeros((rows, far_w), jnp.int32) + blk) == nb - 1
            b0 = jnp.where(last_v, bias_ref[:, 0:far_w], far_bias)
            b1 = jnp.where(last_v, bias_ref[:, far_w:2 * far_w], far_bias)
        else:
            b0 = b1 = far_bias
        s0 = jnp.where(useful, raw0 * scale + b0, NEG)
        s1 = jnp.where(useful, raw1 * scale + b1, NEG)
        mloc = jnp.max(jnp.maximum(s0, s1), axis=1, keepdims=True)
        p0 = jnp.exp(s0 - mloc)
        p1 = jnp.exp(s1 - mloc)
        lloc = jnp.sum(p0 + p1, axis=1, keepdims=True)
        oblk_ref[blk] = (jnp.dot(p0.astype(BF16), pages(v_refs[2 * c]), preferred_element_type=F32)
                         + jnp.dot(p1.astype(BF16), pages(v_refs[2 * c + 1]), preferred_element_type=F32))
        here = lane_b == blk
        gate_new = jnp.where(here, gsum * (1.0 / MOBA_BLOCK), gate_new)
        m_new = jnp.where(here, mloc, m_new)
        l_new = jnp.where(here, lloc, l_new)
    gate_ref[...] = gate_new
    m_ref[...] = m_new
    l_ref[...] = l_new

    @pl.when(last)
    def _():
        row_b = lax.broadcasted_iota(jnp.int32, (rows, LANES), 0)
        lane_f = lane_b.astype(F32)
        g = jnp.where(lane_b < nb, gate_ref[...], -jnp.inf)
        sel = jnp.zeros((rows, LANES), jnp.bool_)
        for _r in range(min(MOBA_TOPK, nb)):
            mx = jnp.max(g, axis=1, keepdims=True)
            idx = jnp.min(jnp.where(g == mx, lane_f, float(LANES)), axis=1, keepdims=True)
            hit = lane_f == idx
            sel = sel | hit
            g = jnp.where(hit, -jnp.inf, g)
        pad = jnp.zeros((LANES - rows, A_HEAD_DIM), F32)
        kn = jnp.concatenate([_stack_heads(kn_ref[...].astype(F32)), pad], axis=0).astype(BF16)
        vn = jnp.concatenate([_stack_heads(vn_ref[...].astype(F32)), pad], axis=0).astype(BF16)
        so = _nt(qb, kn) * scale + bias_ref[:, 2 * far_w:2 * far_w + LANES]
        ok = ((lane_b < rows) & ((lane_b >> s_shift) == (row_b >> s_shift))
              & ((lane_b & (s_len - 1)) <= (row_b & (s_len - 1))))
        so = jnp.where(ok, so, NEG)
        mm = m_ref[...]
        mtot = jnp.maximum(jnp.max(jnp.where(sel, mm, NEG), axis=1, keepdims=True),
                           jnp.max(so, axis=1, keepdims=True))
        w = jnp.where(sel, jnp.exp(mm - mtot), 0.0)
        po = jnp.exp(so - mtot)
        lsum = jnp.sum(w * l_ref[...], axis=1, keepdims=True) + jnp.sum(po, axis=1, keepdims=True)
        acc = jnp.dot(po.astype(BF16), vn, preferred_element_type=F32)
        for nn in range(nb):
            acc = acc + w[:, nn:nn + 1] * oblk_ref[nn]
        out = acc / lsum
        res = jnp.concatenate([out[h * s_len:(h + 1) * s_len, :] for h in range(A_HEADS)], axis=1)
        o_ref[...] = (res * _silu(ga_ref[...].astype(F32))).astype(o_ref.dtype)


def _inproj_attn_kernel(pt_ref, x_ref, w_ref, q_ref, kn_ref, vn_ref, ga_ref, bias_ref, *rest,
                        heads_out, aliased, slot, nb, bps, s_len, scale):
    del pt_ref
    npg = 2 * bps
    k_refs = rest[:npg]
    v_refs = rest[npg:2 * npg]
    rest = rest[2 * npg + (1 if aliased else 0):]
    if heads_out:
        o_ref, oh_ref, ao_ref, wb_ref = rest[:4]
    else:
        (o_ref, ao_ref, wb_ref), oh_ref = rest[:3], None
    _inproj_prep(w_ref, wb_ref)
    _moba_sample_body(q_ref, kn_ref, vn_ref, ga_ref, bias_ref, k_refs, v_refs, ao_ref, *rest[-5:],
                      nb=nb, bps=bps, s_len=s_len, scale=scale,
                      alongside=functools.partial(_inproj_compute, x_ref, o_ref, oh_ref, wb_ref, slot))


def _inproj_even_with_attn(x_bf, wt_all, e, groups, tm, heads, sample, b0):
    zm, zk, zv, bias_s, cache_k, cache_v, page_table, s_len = sample
    m, k = x_bf.shape
    gw = 1024
    first, skip, n_groups = groups
    nsteps = m // tm
    n_pages = page_table.shape[1]
    ppb = MOBA_BLOCK // PAGE_SIZE
    assert ppb == 2 and n_pages % ppb == 0 and s_len == SUBLANES and A_HEADS == SUBLANES
    nb = n_pages // ppb
    assert nb <= LANES and nb % nsteps == 0
    bps = nb // nsteps
    rows = A_HEADS * s_len
    dh = A_HEAD_DIM
    gmap = lambda j: first + j + skip * jnp.minimum(j, 1)
    tok = lambda g: pl.BlockSpec((None, None, s_len, A_WIDTH), lambda j, i, pt: (g, b0 + j, 0, 0))
    page = lambda o: pl.BlockSpec((None, None, PAGE_SIZE, A_HEADS, dh),
                                  lambda j, i, pt: (e, pt[b0 + j, ppb * bps * i + o], 0, 0, 0))
    pages = [page(o) for o in range(ppb * bps)]
    out_specs = [pl.BlockSpec((None, tm, gw), lambda j, i, pt: (j, i, 0))]
    out_shape = [jax.ShapeDtypeStruct((n_groups, m, gw), BF16)]
    args = [page_table, x_bf, wt_all, zm, zk, zv, zm, bias_s,
            *([cache_k] * (ppb * bps)), *([cache_v] * (ppb * bps))]
    extra_specs = []
    aliases = {}
    slot = None
    if heads is not None:
        assert n_groups == 1
        spec, shape, stack, slot = _heads_out_spec(heads, e, m, tm,
                                                   lambda l: (lambda j, i, pt: (l, i, 0, 0)))
        out_specs.append(spec)
        out_shape.append(shape)
        if stack is not None:
            extra_specs.append(pl.BlockSpec(memory_space=pl.ANY))
            args.append(stack)
            aliases = {len(args) - 1: 1}
    out_specs.append(pl.BlockSpec((None, s_len, A_WIDTH), lambda j, i, pt: (j, 0, 0)))
    out_shape.append(jax.ShapeDtypeStruct((n_groups, s_len, A_WIDTH), BF16))
    grid_spec = pltpu.PrefetchScalarGridSpec(
        num_scalar_prefetch=1,
        grid=(n_groups, nsteps),
        in_specs=[pl.BlockSpec((tm, k), lambda j, i, pt: (i, 0)),
                  pl.BlockSpec((None, gw, k), lambda j, i, pt: (e, gmap(j), 0)),
                  tok(0), tok(0), tok(0), tok(1),
                  pl.BlockSpec(bias_s.shape, lambda j, i, pt: (0, 0))] + pages + pages + extra_specs,
        out_specs=out_specs,
        scratch_shapes=[pltpu.VMEM((gw, k), BF16),
                        pltpu.VMEM((rows, dh), F32),
                        pltpu.VMEM((rows, LANES), F32),
                        pltpu.VMEM((rows, LANES), F32),
                        pltpu.VMEM((rows, LANES), F32),
                        pltpu.VMEM((nb, rows, dh), F32)])
    kern = functools.partial(_inproj_attn_kernel, heads_out=heads is not None, aliased=bool(aliases),
                             slot=slot, nb=nb, bps=bps, s_len=s_len, scale=dh ** -0.5)
    return pl.pallas_call(
        kern,
        grid_spec=grid_spec,
        out_shape=out_shape,
        input_output_aliases=aliases,
        compiler_params=_params("arbitrary", "arbitrary"),
        name="inproj_even_sample_attn",
    )(*args)


def _mlstm_kernel(q_ref, k_ref, v_ref, og_ref, gm_ref, g_ref, gt_ref, ng_ref, c0_ref, n0_ref, m0_ref,
                  h_ref, c_out, n_out, m_out, c_scr, n_scr, m_scr, *, L, Lp):
    c = pl.program_id(1)
    dk = B_HEAD_DIM

    @pl.when(c == 0)
    def _():
        c_scr[...] = c0_ref[...]
        n_scr[...] = n0_ref[...]
        m_scr[...] = m0_ref[...]

    def padded(x):
        if L == Lp:
            return x
        return jnp.concatenate([x, jnp.zeros((Lp - L, x.shape[1]), x.dtype)], axis=0)

    g = padded(g_ref[...])
    gt = gt_ref[...]
    r = lax.broadcasted_iota(jnp.int32, (Lp, Lp), 0)
    cc = lax.broadcasted_iota(jnp.int32, (Lp, Lp), 1)
    tri = r >= cc
    tri_t = r <= cc
    rv = lax.broadcasted_iota(jnp.int32, (Lp, 1), 0) < L
    cv = lax.broadcasted_iota(jnp.int32, (1, Lp), 1) < L

    heads = range(B_HEADS)
    cols = [slice(hh * dk, (hh + 1) * dk) for hh in heads]
    ph = []
    for hh in heads:
        q = padded(q_ref[:, cols[hh]].astype(F32))
        k = padded(k_ref[:, cols[hh]].astype(F32)) * (dk ** -0.5)
        v = padded(v_ref[:, cols[hh]].astype(F32))
        li_col = g[:, hh:hh + 1]
        li_row = gt[hh:hh + 1, :]
        lf_col = jax.nn.log_sigmoid(g[:, hh + B_HEADS:hh + B_HEADS + 1])
        lf_row = jax.nn.log_sigmoid(gt[hh + B_HEADS:hh + B_HEADS + 1, :])
        if L != Lp:
            li_col = jnp.where(rv, li_col, NEG)
            lf_col = jnp.where(rv, lf_col, 0.0)
            li_row = jnp.where(cv, li_row, NEG)
            lf_row = jnp.where(cv, lf_row, 0.0)
        b_col = jnp.sum(jnp.where(tri, lf_row, 0.0), axis=1, keepdims=True)
        b_row = jnp.sum(jnp.where(tri_t, lf_col, 0.0), axis=0, keepdims=True)
        m_prev = m_scr[hh]
        log_d = jnp.where(tri, b_col - b_row + li_row, NEG)
        m_inter = b_col + m_prev
        m_t = jnp.maximum(m_inter, jnp.max(log_d, axis=1, keepdims=True))
        ph.append(dict(q=q, k=k, vb=v.astype(BF16), li_col=li_col, li_row=li_row, b_col=b_col,
                       b_row=b_row, m_prev=m_prev, log_d=log_d, m_inter=m_inter, m_t=m_t))
    for hh in heads:
        p = ph[hh]
        qb = p['q'].astype(BF16)
        s = _nt(qb, p['k'].astype(BF16)) * jnp.exp(p['log_d'] - p['m_t'])
        w_inter = jnp.exp(p['m_inter'] - p['m_t'])
        c_old = c_scr[hh]
        n_old = n_scr[hh]
        num = (jnp.dot(s.astype(BF16), p['vb'], preferred_element_type=F32)
               + w_inter * jnp.dot(qb, c_old.astype(BF16), preferred_element_type=F32))
        den = jnp.sum(s, axis=1, keepdims=True) + w_inter * jnp.sum(p['q'] * n_old, axis=1, keepdims=True)
        p.update(h=num / jnp.maximum(jnp.abs(den), jnp.exp(-p['m_t'])), c_old=c_old, n_old=n_old)
    for hh in heads:
        p = ph[hh]
        b_last = p['b_row'][:, L - 1:L]
        m_new = jnp.maximum(b_last + p['m_prev'],
                            jnp.max(b_last - p['b_row'] + p['li_row'], axis=1, keepdims=True))
        decay = jnp.exp(b_last + p['m_prev'] - m_new)
        wk = jnp.exp(b_last - p['b_col'] + p['li_col'] - m_new)
        kw = p['k'] * wk
        c_scr[hh] = decay * p['c_old'] + lax.dot_general(kw.astype(BF16), p['vb'], (((0,), (0,)), ((), ())),
                                                         preferred_element_type=F32)
        n_scr[hh] = decay * p['n_old'] + jnp.sum(kw, axis=0, keepdims=True)
        m_scr[hh] = m_new
    for hh in heads:
        hv = ph[hh]['h'][:L]
        mu = jnp.mean(hv, axis=1, keepdims=True)
        var = jnp.mean(jnp.square(hv - mu), axis=1, keepdims=True)
        hn = (hv - mu) * lax.rsqrt(var + LN_EPS) * ng_ref[hh]
        h_ref[:, cols[hh]] = (hn * jax.nn.sigmoid(og_ref[:, cols[hh]].astype(F32))
                              * _silu(gm_ref[:, cols[hh]].astype(F32))).astype(h_ref.dtype)

    @pl.when(c == pl.num_programs(1) - 1)
    def _():
        c_out[...] = c_scr[...]
        n_out[...] = n_scr[...]
        m_out[...] = m_scr[...]


def _mlstm(zm, gates, norm_g, c0, n0, m0, bsz, t, L):
    Lp = max(L, LANES)
    nc = t // L
    dh = B_HEAD_DIM
    g3 = gates.reshape(bsz, t, LANES)
    gt = jnp.swapaxes(gates[:, :2 * B_HEADS].reshape(bsz, nc, L, 2 * B_HEADS), 2, 3)
    if Lp != L:
        gt = jnp.pad(gt, ((0, 0), (0, 0), (0, 0), (0, Lp - L)))
    ng = norm_g.reshape(B_HEADS, 1, dh).astype(F32)
    n0 = n0.reshape(bsz, B_HEADS, 1, dh)
    m0 = m0.reshape(bsz, B_HEADS, 1, 1)
    kern = functools.partial(_mlstm_kernel, L=L, Lp=Lp)
    zspec = lambda gidx: pl.BlockSpec((None, None, L, B_WIDTH), lambda b, c: (gidx, b, c, 0))
    st = lambda r, cdim: pl.BlockSpec((None, B_HEADS, r, cdim), lambda b, c: (b, 0, 0, 0))
    h, c1, n1, m1 = pl.pallas_call(
        kern,
        grid=(bsz, nc),
        in_specs=[zspec(2), zspec(3), zspec(4), zspec(5), zspec(6),
                  pl.BlockSpec((None, L, LANES), lambda b, c: (b, c, 0)),
                  pl.BlockSpec((None, None, 2 * B_HEADS, Lp), lambda b, c: (b, c, 0, 0)),
                  pl.BlockSpec((B_HEADS, 1, dh), lambda b, c: (0, 0, 0)),
                  st(dh, dh), st(1, dh), st(1, 1)],
        out_specs=[pl.BlockSpec((None, L, B_WIDTH), lambda b, c: (b, c, 0)),
                   st(dh, dh), st(1, dh), st(1, 1)],
        out_shape=[jax.ShapeDtypeStruct((bsz, t, B_WIDTH), BF16),
                   jax.ShapeDtypeStruct((bsz, B_HEADS, dh, dh), F32),
                   jax.ShapeDtypeStruct((bsz, B_HEADS, 1, dh), F32),
                   jax.ShapeDtypeStruct((bsz, B_HEADS, 1, 1), F32)],
        scratch_shapes=[pltpu.VMEM((B_HEADS, dh, dh), F32), pltpu.VMEM((B_HEADS, 1, dh), F32),
                        pltpu.VMEM((B_HEADS, 1, 1), F32)],
        compiler_params=_params("parallel", "arbitrary"),
        name="mlstm",
    )(zm, zm, zm, zm, zm, g3, gt, ng, c0, n0, m0)
    return h, c1, n1.reshape(bsz, B_HEADS, dh), m1.reshape(bsz, B_HEADS)


def _outln_kernel(*refs, n_parts):
    mix = refs[:n_parts]
    wf = refs[n_parts:2 * n_parts]
    x_ref, g_ref, b_ref, o_ref, ob_ref = refs[2 * n_parts:2 * n_parts + 5]
    ws = refs[2 * n_parts + 5:]

    @pl.when(pl.program_id(0) == 0)
    def _():
        for p in range(n_parts):
            ws[p][...] = wf[p][...].astype(BF16)

    tm = x_ref.shape[0]
    sub = min(tm, LANES)
    for r in range(tm // sub):
        rows = pl.ds(r * sub, sub)
        y = jnp.dot(mix[0][rows, :], ws[0][...], preferred_element_type=F32)
        for p in range(1, n_parts):
            y = y + jnp.dot(mix[p][rows, :], ws[p][...], preferred_element_type=F32)
        z = ALPHA * x_ref[rows, :] + y
        mu = jnp.mean(z, axis=1, keepdims=True)
        var = jnp.mean(jnp.square(z - mu), axis=1, keepdims=True)
        xn = (z - mu) * lax.rsqrt(var + LN_EPS) * g_ref[...] + b_ref[...]
        o_ref[rows, :] = xn
        ob_ref[rows, :] = xn.astype(BF16)


def _out_ln(mix_parts, w_all, layer, x, ln_g, ln_b, tm):
    m, d = x.shape
    n_parts = len(mix_parts)
    kp = mix_parts[0].shape[1]
    assert all(mp.shape[1] == kp for mp in mix_parts) and n_parts * kp == w_all.shape[1]
    wspec = lambda p: pl.BlockSpec((None, kp, d), lambda i: (layer, p, 0), pipeline_mode=pl.Buffered(1))
    in_specs = ([pl.BlockSpec((tm, kp), lambda i: (i, 0)) for _ in mix_parts]
                + [wspec(p) for p in range(n_parts)]
                + [pl.BlockSpec((tm, d), lambda i: (i, 0)),
                   pl.BlockSpec((1, d), lambda i: (0, 0)),
                   pl.BlockSpec((1, d), lambda i: (0, 0))])
    return pl.pallas_call(
        functools.partial(_outln_kernel, n_parts=n_parts),
        grid=(m // tm,),
        in_specs=in_specs,
        out_specs=[pl.BlockSpec((tm, d), lambda i: (i, 0)), pl.BlockSpec((tm, d), lambda i: (i, 0))],
        out_shape=[jax.ShapeDtypeStruct((m, d), F32), jax.ShapeDtypeStruct((m, d), BF16)],
        scratch_shapes=[pltpu.VMEM((kp, d), BF16) for _ in mix_parts],
        compiler_params=_params("arbitrary"),
        name="outproj_ln",
    )(*mix_parts, *([w_all] * n_parts), x, ln_g.reshape(1, d).astype(F32), ln_b.reshape(1, d).astype(F32))


def _odd_kernel(*refs, tm, carry_mode, seq_tiles, seq_rows):
    if carry_mode:
        x_ref, wb, wc, wx, wg, cw_ref, mix_ref, tail_ref, wbf, carry = refs
    else:
        x_ref, wb, wc, wx, wg, cw_ref, plast_ref, pprev_ref, mix_ref, tail_ref, wbf = refs

    @pl.when(pl.program_id(1) == 0)
    def _():
        for gidx, wref in enumerate((wb, wc, wx, wg)):
            wbf[gidx] = wref[...].astype(BF16)

    if carry_mode:
        @pl.when(pl.program_id(1) % seq_tiles == 0)
        def _():
            carry[...] = jnp.zeros_like(carry)
        last = carry[SUBLANES - 1:SUBLANES, :]
        prev = carry[SUBLANES - 2:SUBLANES - 1, :]

    cw = cw_ref[...]
    sub = min(tm, 4 * LANES)
    for r in range(tm // sub):
        rs = pl.ds(r * sub, sub)
        x = x_ref[rs, :]
        zb = jnp.dot(x, wbf[0], preferred_element_type=F32)
        zc = jnp.dot(x, wbf[1], preferred_element_type=F32)
        zx = jnp.dot(x, wbf[2], preferred_element_type=F32)
        zg = jnp.dot(x, wbf[3], preferred_element_type=F32)
        u = zc * zx
        rows = lax.broadcasted_iota(jnp.int32, u.shape, 0)
        r1 = pltpu.roll(u, 1, 0)
        r2 = pltpu.roll(u, 2, 0)
        if carry_mode:
            rm = rows
        else:
            last = plast_ref[rs, :]
            prev = pprev_ref[rs, :]
            rm = rows & (seq_rows - 1)
        u1 = jnp.where(rm == 0, last, r1)
        u2 = jnp.where(rm == 0, prev, jnp.where(rm == 1, last, r2))
        y = cw[0:1, :] * u2 + cw[1:2, :] * u1 + cw[2:3, :] * u
        mix_ref[rs, :] = (zb * y * _silu(zg)).astype(mix_ref.dtype)
        if carry_mode:
            last = u[sub - 1:sub, :]
            prev = u[sub - 2:sub - 1, :]
        else:
            tail_ref[rs, :] = u
    if carry_mode:
        tail = u[sub - SUBLANES:, :]
        carry[...] = tail
        tail_ref[...] = tail


def _odd_mix(x_bf, w_all, o, conv_w, tm, tn, seq_len, state=None):
    m, k = x_bf.shape
    nn = C_WIDTH // tn
    carry_mode = state is None
    wspec = lambda gidx: pl.BlockSpec((None, k, tn), lambda j, i: (o, 0, gidx * nn + j))
    in_specs = [pl.BlockSpec((tm, k), lambda j, i: (i, 0)),
                wspec(0), wspec(1), wspec(2), wspec(3),
                pl.BlockSpec((CONV_K, tn), lambda j, i: (0, j))]
    args = [x_bf, w_all, w_all, w_all, w_all, conv_w.astype(F32)]
    scratch = [pltpu.VMEM((4, k, tn), BF16)]
    if carry_mode:
        assert seq_len % tm == 0
        tail_shape = jax.ShapeDtypeStruct((m // tm, SUBLANES, C_WIDTH), F32)
        tail_spec = pl.BlockSpec((None, SUBLANES, tn), lambda j, i: (i, 0, j))
        scratch.append(pltpu.VMEM((SUBLANES, tn), F32))
    else:
        assert tm % seq_len == 0 and seq_len & (seq_len - 1) == 0
        in_specs += [pl.BlockSpec((tm, tn), lambda j, i: (i, j))] * 2
        args += list(state)
        tail_shape = jax.ShapeDtypeStruct((m, C_WIDTH), F32)
        tail_spec = pl.BlockSpec((tm, tn), lambda j, i: (i, j))
    kern = functools.partial(_odd_kernel, tm=tm, carry_mode=carry_mode,
                             seq_tiles=max(seq_len // tm, 1), seq_rows=seq_len)
    return pl.pallas_call(
        kern,
        grid=(nn, m // tm),
        in_specs=in_specs,
        out_specs=[pl.BlockSpec((tm, tn), lambda j, i: (i, j)), tail_spec],
        out_shape=[jax.ShapeDtypeStruct((m, C_WIDTH), BF16), tail_shape],
        scratch_shapes=scratch,
        compiler_params=_params("parallel", "arbitrary"),
        name="odd_mix",
    )(*args)


def _row_tiles(m):
    return min(m, 1024), min(m, 512)


def _project_even(x_bf, wt_all, e, bsz, t, kv_stacks, sample=None):
    n_even = wt_all.shape[0]
    hk, hv = ((n_even, st) for st in kv_stacks)
    m = bsz * t
    tm, _ = _row_tiles(m)
    ng = N_GROUPS_EVEN - 2
    if sample is None:
        (zm,) = _inproj_even(x_bf, wt_all, e, (0, 2, ng), tm)
        zk, ka = _inproj_even(x_bf, wt_all, e, (1, 0, 1), tm, heads=hk)
        attn_s = None
    else:
        n_seq = sample[0].shape[1]
        assert n_seq == ng + 1
        tm_a = tm // 2
        zm, attn_a = _inproj_even_with_attn(x_bf, wt_all, e, (0, 2, ng), tm_a, None, sample, 0)
        zk, ka, attn_b = _inproj_even_with_attn(x_bf, wt_all, e, (1, 0, 1), tm_a, hk, sample, ng)
        attn_s = jnp.concatenate([attn_a, attn_b], axis=0)
    zv, va = _inproj_even(x_bf, wt_all, e, (2, 0, 1), tm, heads=hv)
    return (zm.reshape(ng, bsz, t, 1024), zk.reshape(1, bsz, t, 1024), zv.reshape(1, bsz, t, 1024),
            (ka, va), attn_s)


def _finish_even(x, bsz, t, zm, attn, gates, w_out_all, e, norm_g, ln_g, ln_b, state, L):
    _, tm_out = _row_tiles(bsz * t)
    hm, c1, n1, m1 = _mlstm(zm, gates, norm_g, *state, bsz, t, L)
    xn, xn_bf = _out_ln([attn.reshape(bsz * t, A_WIDTH), hm.reshape(bsz * t, B_WIDTH)],
                        w_out_all, e, x, ln_g, ln_b, tm_out)
    return xn, xn_bf, (c1, n1, m1)


def _even_layers(xp, xp_bf, bp, tp, xs, xs_bf, bs, ts, wt_all, e, gbias, w_out_all, norm_g,
                 ln_g, ln_b, bias_p, bias_s, cache_k, cache_v, page_table, state_s, kv_p, kv_s):
    gates_p, xp_bf = _gates(xp if xp_bf is None else xp_bf, wt_all, e, gbias, _row_tiles(bp * tp)[0])
    gates_s, xs_bf = _gates(xs if xs_bf is None else xs_bf, wt_all, e, gbias, _row_tiles(bs * ts)[0])
    zm_s, zk_s, zv_s, kv_s, _ = _project_even(xs_bf, wt_all, e, bs, ts, kv_s)
    sample = (zm_s, zk_s, zv_s, bias_s, cache_k, cache_v, page_table, ts)
    zm_p, zk_p, zv_p, kv_p, attn_s = _project_even(xp_bf, wt_all, e, bp, tp, kv_p, sample)
    attn_p = _moba_prompt(zm_p, zk_p, zv_p, bias_p, bp, tp)
    zeros = (jnp.zeros((bp, B_HEADS, B_HEAD_DIM, B_HEAD_DIM), F32),
             jnp.zeros((bp, B_HEADS, B_HEAD_DIM), F32), jnp.zeros((bp, B_HEADS), F32))
    xp, xp_bf, st_p = _finish_even(xp, bp, tp, zm_p, attn_p, gates_p, w_out_all, e,
                                   norm_g, ln_g, ln_b, zeros, math.gcd(tp, MLSTM_PROMPT_CHUNK))
    xs, xs_bf, st_s = _finish_even(xs, bs, ts, zm_s, attn_s, gates_s, w_out_all, e,
                                   norm_g, ln_g, ln_b, state_s, math.gcd(ts, MLSTM_CHUNK))
    return (xp, xp_bf, kv_p, st_p), (xs, xs_bf, kv_s, st_s)


def _odd_layer(x, x_bf, bsz, t, w_in_all, o, w_out_all, conv_w, ln_g, ln_b, buf):
    tm, tm_out = _row_tiles(bsz * t)
    if buf is None:
        tm = math.gcd(2 * tm, t)
        mix, tail = _odd_mix(x_bf, w_in_all, o, conv_w, tm, 256, t)
        seq_tiles = t // tm
        new_buf = tail.reshape(bsz, seq_tiles, SUBLANES, C_WIDTH)[:, -1, SUBLANES - (CONV_K - 1):, :]
    else:
        plast = jnp.repeat(buf[:, 1, :], t, axis=0)
        pprev = jnp.repeat(buf[:, 0, :], t, axis=0)
        mix, u = _odd_mix(x_bf, w_in_all, o, conv_w, tm, 256, t, state=(plast, pprev))
        new_buf = u.reshape(bsz, t, C_WIDTH)[:, t - (CONV_K - 1):, :]
    xn, xn_bf = _out_ln([mix], w_out_all, o, x, ln_g, ln_b, tm_out)
    return xn, xn_bf, new_buf


def kernel(x_prompt, x_sample, cache_k, cache_v, page_table, state_C, state_n, state_m, state_conv,
           w_in_even, w_out_even, mlstm_gate_bias, mlstm_norm_g, rel_bias,
           w_in_odd, w_out_odd, conv_w, ln_g, ln_b):
    bp, tp, d = x_prompt.shape
    bs, ts, _ = x_sample.shape
    bias_p = _bias_tiles(rel_bias, _prompt_bias_idx())
    bias_s = _bias_tiles(rel_bias, _sample_bias_idx(ts)).reshape(A_HEADS * ts, -1)

    xp = x_prompt.reshape(bp * tp, d)
    xs = x_sample.reshape(bs * ts, d)
    xp_bf = xs_bf = None
    wt_even = jnp.swapaxes(w_in_even, 1, 2)

    kv_p = kv_s = (None, None)
    pc, pn, pm, pb = [], [], [], []
    sc, sn, sm, sb = [], [], [], []
    for layer in range(DEPTH):
        if layer % 2 == 0:
            e = layer // 2
            gbias = jnp.pad(mlstm_gate_bias[e], (0, LANES - 2 * B_HEADS)).reshape(1, LANES).astype(F32)
            (xp, xp_bf, kv_p, (c1, n1, m1)), (xs, xs_bf, kv_s, (c2, n2, m2)) = _even_layers(
                xp, xp_bf, bp, tp, xs, xs_bf, bs, ts, wt_even, e, gbias, w_out_even,
                mlstm_norm_g[e], ln_g[layer], ln_b[layer], bias_p, bias_s, cache_k, cache_v, page_table,
                (state_C[e], state_n[e], state_m[e]), kv_p, kv_s)
            pc.append(c1); pn.append(n1); pm.append(m1)
            sc.append(c2); sn.append(n2); sm.append(m2)
        else:
            o = layer // 2
            xp, xp_bf, b1 = _odd_layer(xp, xp_bf, bp, tp, w_in_odd, o, w_out_odd, conv_w[o],
                                       ln_g[layer], ln_b[layer], None)
            xs, xs_bf, b2 = _odd_layer(xs, xs_bf, bs, ts, w_in_odd, o, w_out_odd, conv_w[o],
                                       ln_g[layer], ln_b[layer], state_conv[o])
            pb.append(b1); sb.append(b2)
    n_even = w_in_even.shape[0]
    heads = lambda st, b, t: st.reshape(n_even, b, t, A_HEADS, A_HEAD_DIM)
    return (xp.reshape(bp, tp, d), xs.reshape(bs, ts, d),
            heads(kv_p[0], bp, tp), heads(kv_p[1], bp, tp),
            jnp.stack(pc), jnp.stack(pn), jnp.stack(pm), jnp.stack(pb),
            heads(kv_s[0], bs, ts), heads(kv_s[1], bs, ts),
            jnp.stack(sc), jnp.stack(sn), jnp.stack(sm), jnp.stack(sb))
```

```python
import functools
import math

import numpy as np
import jax
import jax.numpy as jnp
from jax import lax
from jax.experimental import pallas as pl
from jax.experimental.pallas import tpu as pltpu

F32 = jnp.float32
BF16 = jnp.bfloat16

D_MODEL = 2048
DEPTH = 4
PAGE_SIZE = 128
A_HEADS = 8
A_HEAD_DIM = 128
A_WIDTH = A_HEADS * A_HEAD_DIM
MOBA_BLOCK = 256
MOBA_TOPK = 3
REL_BUCKETS = 32
REL_MAX_DIST = 128
B_HEADS = 4
B_HEAD_DIM = 256
B_WIDTH = B_HEADS * B_HEAD_DIM
MLSTM_CHUNK = 64
MLSTM_PROMPT_CHUNK = 512
C_WIDTH = D_MODEL
CONV_K = 3
N_GROUPS_EVEN = 9
EVEN_MAIN = N_GROUPS_EVEN * 1024
ALPHA = (2.0 * DEPTH) ** 0.25
LN_EPS = 1e-5

LANES = 128
SUBLANES = 8
VMEM_LIMIT = 56 * 1024 * 1024

NEG = -1e30


def _params(*sem):
    return pltpu.CompilerParams(dimension_semantics=sem, vmem_limit_bytes=VMEM_LIMIT)


def _nt(a, b):
    return lax.dot_general(a, b, (((1,), (1,)), ((), ())), preferred_element_type=F32)


def _silu(x):
    return x * jax.nn.sigmoid(x)


def _bucket_np(d):
    d = np.asarray(d, np.int64)
    n = np.maximum(d, 0)
    max_exact = REL_BUCKETS // 2
    nf = np.maximum(n, 1).astype(np.float64)
    large = max_exact + (np.log(nf / max_exact) / math.log(REL_MAX_DIST / max_exact)
                         * (REL_BUCKETS - max_exact)).astype(np.int64)
    large = np.minimum(large, REL_BUCKETS - 1)
    out = np.where(n < max_exact, n, large)
    return np.where(d < 0, -1, out).astype(np.int32)


def _bias_kernel(tab_ref, idx_ref, o_ref):
    h = pl.program_id(0)
    idx = idx_ref[...]
    out = jnp.full(idx.shape, NEG, F32)
    for b in range(REL_BUCKETS):
        out = jnp.where(idx == b, tab_ref[h, b], out)
    o_ref[...] = out


def _bias_tiles(rel_bias, idx):
    r, c = idx.shape
    return pl.pallas_call(
        _bias_kernel,
        grid=(A_HEADS,),
        in_specs=[pl.BlockSpec(memory_space=pltpu.SMEM),
                  pl.BlockSpec((r, c), lambda h: (0, 0))],
        out_specs=pl.BlockSpec((None, r, c), lambda h: (h, 0, 0)),
        out_shape=jax.ShapeDtypeStruct((A_HEADS, r, c), F32),
        compiler_params=_params("parallel"),
        name="bias_tiles",
    )(rel_bias.astype(F32), jnp.asarray(idx))


def _prompt_bias_idx():
    qi = np.arange(MOBA_BLOCK)[:, None]
    kj = np.arange(2 * MOBA_BLOCK)[None, :]
    return _bucket_np(qi + MOBA_BLOCK - kj)


def _sample_bias_idx(s_len):
    s = np.arange(s_len)[:, None]
    lane = np.arange(2 * PAGE_SIZE * A_HEADS)[None, :]
    off = lane // A_HEADS
    far = _bucket_np(MOBA_BLOCK + s - off)
    lo = np.arange(LANES)[None, :]
    own = _bucket_np(np.where(lo < A_HEADS * s_len, s - (lo % s_len), -1))
    return np.concatenate([far, own], axis=1)


def _inproj_prep(w_ref, wb_ref):
    @pl.when(pl.program_id(1) == 0)
    def _():
        wb_ref[...] = w_ref[...].astype(BF16)


def _inproj_compute(x_ref, o_ref, oh_ref, wb_ref, slot, part=0, n_parts=1):
    tm = x_ref.shape[0]
    nf = wb_ref.shape[0] // n_parts
    cols = pl.ds(part * nf, nf)
    y = _nt(x_ref[...], wb_ref[cols, :])
    o_ref[:, cols] = y.astype(o_ref.dtype)
    if oh_ref is not None:
        nh = nf // A_HEAD_DIM
        yh = y.reshape(tm, nh, A_HEAD_DIM)
        if slot is None:
            oh_ref[:, pl.ds(part * nh, nh), :] = yh
        else:
            oh_ref[slot, :, pl.ds(part * nh, nh), :] = yh
            if part == 0:
                for other in range(oh_ref.shape[0]):
                    if other != slot:
                        oh_ref[other] = jnp.zeros(oh_ref.shape[1:], oh_ref.dtype)


def _inproj_kernel(x_ref, w_ref, *rest, heads_out, aliased, slot):
    if aliased:
        rest = rest[1:]
    if heads_out:
        o_ref, oh_ref, wb_ref = rest
    else:
        (o_ref, wb_ref), oh_ref = rest, None
    _inproj_prep(w_ref, wb_ref)
    _inproj_compute(x_ref, o_ref, oh_ref, wb_ref, slot)


def _heads_out_spec(heads, e, m, tm, index_map):
    n_layers, stack = heads
    shape = jax.ShapeDtypeStruct((n_layers, m, A_HEADS, A_HEAD_DIM), F32)
    if stack is None:
        return pl.BlockSpec((n_layers, tm, A_HEADS, A_HEAD_DIM), index_map(0)), shape, None, e
    return pl.BlockSpec((None, tm, A_HEADS, A_HEAD_DIM), index_map(e)), shape, stack, None


def _inproj_even(x_bf, wt_all, e, groups, tm, heads=None):
    m, k = x_bf.shape
    gw = 1024
    first, skip, n_groups = groups
    gmap = lambda j: first + j + skip * jnp.minimum(j, 1)
    out_specs = [pl.BlockSpec((None, tm, gw), lambda j, i: (j, i, 0))]
    out_shape = [jax.ShapeDtypeStruct((n_groups, m, gw), BF16)]
    in_specs = [pl.BlockSpec((tm, k), lambda j, i: (i, 0)),
                pl.BlockSpec((None, gw, k), lambda j, i: (e, gmap(j), 0))]
    args = [x_bf, wt_all]
    aliases = {}
    slot = None
    if heads is not None:
        assert n_groups == 1
        spec, shape, stack, slot = _heads_out_spec(heads, e, m, tm, lambda l: (lambda j, i: (l, i, 0, 0)))
        out_specs.append(spec)
        out_shape.append(shape)
        if stack is not None:
            in_specs.append(pl.BlockSpec(memory_space=pl.ANY))
            args.append(stack)
            aliases = {len(args) - 1: 1}
    return pl.pallas_call(
        functools.partial(_inproj_kernel, heads_out=heads is not None, aliased=bool(aliases), slot=slot),
        grid=(n_groups, m // tm),
        in_specs=in_specs,
        out_specs=out_specs,
        out_shape=out_shape,
        scratch_shapes=[pltpu.VMEM((gw, k), BF16)],
        input_output_aliases=aliases,
        compiler_params=_params("parallel", "arbitrary"),
        name="inproj_even",
    )(*args)


def _gates_kernel(x_ref, w_ref, b_ref, o_ref, *xb_ref):
    w = w_ref[...]
    wp = jnp.concatenate([w, jnp.zeros((LANES - w.shape[0], w.shape[1]), F32)], axis=0)
    xb = x_ref[...].astype(BF16)
    o_ref[...] = _nt(xb, wp.astype(BF16)) + b_ref[...]
    if xb_ref:
        xb_ref[0][...] = xb


def _gates(x, wt_all, e, gbias, tm):
    m, k = x.shape
    ng = 2 * B_HEADS
    assert ng == SUBLANES and EVEN_MAIN % ng == 0
    out_specs = [pl.BlockSpec((tm, LANES), lambda i: (i, 0))]
    out_shape = [jax.ShapeDtypeStruct((m, LANES), F32)]
    if x.dtype != BF16:
        out_specs.append(pl.BlockSpec((tm, k), lambda i: (i, 0)))
        out_shape.append(jax.ShapeDtypeStruct((m, k), BF16))
    outs = pl.pallas_call(
        _gates_kernel,
        grid=(m // tm,),
        in_specs=[pl.BlockSpec((tm, k), lambda i: (i, 0)),
                  pl.BlockSpec((None, ng, k), lambda i: (e, EVEN_MAIN // ng, 0)),
                  pl.BlockSpec((1, LANES), lambda i: (0, 0))],
        out_specs=out_specs,
        out_shape=out_shape,
        compiler_params=_params("parallel"),
        name="mlstm_gates",
    )(x, wt_all, gbias)
    return outs[0], (outs[1] if len(outs) > 1 else x)


def _moba_prompt_kernel(q_ref, k_ref, v_ref, g_ref, bias_ref, o_ref, kmean_ref, negm_ref,
                        *, nblk, scale):
    blk = MOBA_BLOCK
    for j in range(nblk):
        kmean_ref[j:j + 1, :] = jnp.mean(k_ref[j * blk:(j + 1) * blk, :].astype(F32), axis=0, keepdims=True)
    far_bias = bias_ref[0:1, 0:1]
    eye = (lax.broadcasted_iota(jnp.int32, (blk, blk), 0)
           == lax.broadcasted_iota(jnp.int32, (blk, blk), 1)).astype(BF16)
    rowi = lax.broadcasted_iota(jnp.int32, (nblk, blk), 0)

    i0 = MOBA_TOPK + 1
    if i0 < nblk:
        nq = (nblk - i0) * blk
        gt = lax.dot_general(kmean_ref[...], q_ref[i0 * blk:, :].astype(F32), (((1,), (1,)), ((), ())),
                             precision=lax.Precision.HIGHEST, preferred_element_type=F32)
        rowq = lax.broadcasted_iota(jnp.int32, (nblk, nq), 0)
        own = i0 + (lax.broadcasted_iota(jnp.int32, (nblk, nq), 1) >> (blk.bit_length() - 1))
        valid = rowq < own
        sel_t = jnp.zeros((nblk, nq), F32)
        for j in range(nblk - 1):
            gj = gt[j:j + 1, :]
            beats = ((gt > gj) | ((gt == gj) & (rowq < j))) & valid
            rank = jnp.sum(beats.astype(F32), axis=0, keepdims=True)
            sel_t = jnp.where(rowq == j, (rank < MOBA_TOPK).astype(F32), sel_t)
        sel_t = jnp.concatenate([sel_t, jnp.zeros((LANES - nblk, nq), F32)], axis=0).astype(BF16)
        for i in range(i0, nblk):
            cs = (i - i0) * blk
            sel = _nt(eye, sel_t[:, cs:cs + blk])
            negm_ref[i - i0] = (sel - 1.0) * (-NEG)

    for i in range(nblk):
        q = q_ref[i * blk:(i + 1) * blk, :].astype(F32)
        negm = negm_ref[i - MOBA_TOPK - 1] if i > MOBA_TOPK else None
        w = (i + 1) * blk
        s = _nt((q * scale).astype(BF16), k_ref[0:w, :])
        pieces = []
        for j in range(i + 1):
            sj = s[:, j * blk:(j + 1) * blk]
            if j == i:
                sj = sj + bias_ref[:, blk:2 * blk]
            elif j == i - 1:
                sj = sj + bias_ref[:, 0:blk]
                if negm is not None:
                    sj = sj + negm[:, j:j + 1]
            else:
                sj = sj + (far_bias if negm is None else far_bias + negm[:, j:j + 1])
            pieces.append(sj)
        m = jnp.max(functools.reduce(jnp.maximum, pieces), axis=1, keepdims=True)
        ps = [jnp.exp(sj - m) for sj in pieces]
        l = jnp.sum(functools.reduce(jnp.add, ps), axis=1, keepdims=True)
        pcat = ps[0].astype(BF16) if i == 0 else jnp.concatenate([p.astype(BF16) for p in ps], axis=1)
        acc = jnp.dot(pcat, v_ref[0:w, :], preferred_element_type=F32)
        g = g_ref[i * blk:(i + 1) * blk, :].astype(F32)
        o_ref[i * blk:(i + 1) * blk, :] = (acc / l * _silu(g)).astype(o_ref.dtype)


def _moba_prompt(zm, zk, zv, bias_p, bsz, t):
    blk = MOBA_BLOCK
    nblk = t // blk
    assert nblk <= SUBLANES
    dh = A_HEAD_DIM
    kern = functools.partial(_moba_prompt_kernel, nblk=nblk, scale=dh ** -0.5)
    zspec = lambda g: pl.BlockSpec((None, None, t, dh), lambda b, h: (g, b, 0, h))
    return pl.pallas_call(
        kern,
        grid=(bsz, A_HEADS),
        in_specs=[zspec(0), zspec(0), zspec(0), zspec(1),
                  pl.BlockSpec((None, blk, 2 * blk), lambda b, h: (h, 0, 0))],
        out_specs=pl.BlockSpec((None, t, dh), lambda b, h: (b, 0, h)),
        out_shape=jax.ShapeDtypeStruct((bsz, t, A_WIDTH), BF16),
        scratch_shapes=[pltpu.VMEM((nblk, dh), F32),
                        pltpu.VMEM((max(nblk - MOBA_TOPK - 1, 1), blk, LANES), F32)],
        compiler_params=_params("parallel", "parallel"),
        name="moba_prompt",
    )(zm, zk, zv, zm, bias_p)


def _stack_heads(x):
    return jnp.concatenate([x[:, h * A_HEAD_DIM:(h + 1) * A_HEAD_DIM] for h in range(A_HEADS)], axis=0)


def _moba_sample_body(q_ref, kn_ref, vn_ref, ga_ref, bias_ref, k_refs, v_refs, o_ref,
                      qm_ref, gate_ref, m_ref, l_ref, oblk_ref, *, nb, bps, s_len, scale,
                      alongside=None):
    n = pl.program_id(1)
    rows = A_HEADS * s_len
    s_shift = s_len.bit_length() - 1
    far_w = PAGE_SIZE * A_HEADS

    @pl.when(n == 0)
    def _():
        qm_ref[...] = _stack_heads(q_ref[...].astype(F32))
        gate_ref[...] = jnp.full_like(gate_ref, NEG)
        m_ref[...] = jnp.full_like(m_ref, NEG)
        l_ref[...] = jnp.zeros_like(l_ref)

    qb = qm_ref[...].astype(BF16)
    row = lax.broadcasted_iota(jnp.int32, (rows, far_w), 0)
    lane = lax.broadcasted_iota(jnp.int32, (rows, far_w), 1)
    useful = (lane & (A_HEADS - 1)) == (row >> s_shift)
    lane_b = lax.broadcasted_iota(jnp.int32, (rows, LANES), 1)
    last = n == pl.num_programs(1) - 1
    far_bias = bias_ref[:, 0:1]

    def pages(ref):
        return ref[...].reshape(far_w, A_HEAD_DIM).astype(BF16)

    gate_new = gate_ref[...]
    m_new = m_ref[...]
    l_new = l_ref[...]
    for c in range(bps):
        blk = n * bps + c
        raw0 = _nt(qb, pages(k_refs[2 * c]))
        raw1 = _nt(qb, pages(k_refs[2 * c + 1]))
        if alongside is not None:
            alongside(c, bps)
        gsum = jnp.sum(jnp.where(useful, raw0 + raw1, 0.0), axis=1, keepdims=True)
        if c == bps - 1:
            last_v = (jnp.zeros((rows, far_w), jnp.int32) + blk) == nb - 1
            b0 = jnp.where(last_v, bias_ref[:, 0:far_w], far_bias)
            b1 = jnp.where(last_v, bias_ref[:, far_w:2 * far_w], far_bias)
        else:
            b0 = b1 = far_bias
        s0 = jnp.where(useful, raw0 * scale + b0, NEG)
        s1 = jnp.where(useful, raw1 * scale + b1, NEG)
        mloc = jnp.max(jnp.maximum(s0, s1), axis=1, keepdims=True)
        p0 = jnp.exp(s0 - mloc)
        p1 = jnp.exp(s1 - mloc)
        lloc = jnp.sum(p0 + p1, axis=1, keepdims=True)
        oblk_ref[blk] = (jnp.dot(p0.astype(BF16), pages(v_refs[2 * c]), preferred_element_type=F32)
                         + jnp.dot(p1.astype(BF16), pages(v_refs[2 * c + 1]), preferred_element_type=F32))
        here = lane_b == blk
        gate_new = jnp.where(here, gsum * (1.0 / MOBA_BLOCK), gate_new)
        m_new = jnp.where(here, mloc, m_new)
        l_new = jnp.where(here, lloc, l_new)
    gate_ref[...] = gate_new
    m_ref[...] = m_new
    l_ref[...] = l_new

    @pl.when(last)
    def _():
        row_b = lax.broadcasted_iota(jnp.int32, (rows, LANES), 0)
        lane_f = lane_b.astype(F32)
        g = jnp.where(lane_b < nb, gate_ref[...], -jnp.inf)
        sel = jnp.zeros((rows, LANES), jnp.bool_)
        for _r in range(min(MOBA_TOPK, nb)):
            mx = jnp.max(g, axis=1, keepdims=True)
            idx = jnp.min(jnp.where(g == mx, lane_f, float(LANES)), axis=1, keepdims=True)
            hit = lane_f == idx
            sel = sel | hit
            g = jnp.where(hit, -jnp.inf, g)
        pad = jnp.zeros((LANES - rows, A_HEAD_DIM), F32)
        kn = jnp.concatenate([_stack_heads(kn_ref[...].astype(F32)), pad], axis=0).astype(BF16)
        vn = jnp.concatenate([_stack_heads(vn_ref[...].astype(F32)), pad], axis=0).astype(BF16)
        so = _nt(qb, kn) * scale + bias_ref[:, 2 * far_w:2 * far_w + LANES]
        ok = ((lane_b < rows) & ((lane_b >> s_shift) == (row_b >> s_shift))
              & ((lane_b & (s_len - 1)) <= (row_b & (s_len - 1))))
        so = jnp.where(ok, so, NEG)
        mm = m_ref[...]
        mtot = jnp.maximum(jnp.max(jnp.where(sel, mm, NEG), axis=1, keepdims=True),
                           jnp.max(so, axis=1, keepdims=True))
        w = jnp.where(sel, jnp.exp(mm - mtot), 0.0)
        po = jnp.exp(so - mtot)
        lsum = jnp.sum(w * l_ref[...], axis=1, keepdims=True) + jnp.sum(po, axis=1, keepdims=True)
        acc = jnp.dot(po.astype(BF16), vn, preferred_element_type=F32)
        for nn in range(nb):
            acc = acc + w[:, nn:nn + 1] * oblk_ref[nn]
        out = acc / lsum
        res = jnp.concatenate([out[h * s_len:(h + 1) * s_len, :] for h in range(A_HEADS)], axis=1)
        o_ref[...] = (res * _silu(ga_ref[...].astype(F32))).astype(o_ref.dtype)


def _inproj_attn_kernel(pt_ref, x_ref, w_ref, q_ref, kn_ref, vn_ref, ga_ref, bias_ref, *rest,
                        heads_out, aliased, slot, nb, bps, s_len, scale):
    del pt_ref
    npg = 2 * bps
    k_refs = rest[:npg]
    v_refs = rest[npg:2 * npg]
    rest = rest[2 * npg + (1 if aliased else 0):]
    if heads_out:
        o_ref, oh_ref, ao_ref, wb_ref = rest[:4]
    else:
        (o_ref, ao_ref, wb_ref), oh_ref = rest[:3], None
    _inproj_prep(w_ref, wb_ref)
    _moba_sample_body(q_ref, kn_ref, vn_ref, ga_ref, bias_ref, k_refs, v_refs, ao_ref, *rest[-5:],
                      nb=nb, bps=bps, s_len=s_len, scale=scale,
                      alongside=functools.partial(_inproj_compute, x_ref, o_ref, oh_ref, wb_ref, slot))


def _inproj_even_with_attn(x_bf, wt_all, e, groups, tm, heads, sample, b0):
    zm, zk, zv, bias_s, cache_k, cache_v, page_table, s_len = sample
    m, k = x_bf.shape
    gw = 1024
    first, skip, n_groups = groups
    nsteps = m // tm
    n_pages = page_table.shape[1]
    ppb = MOBA_BLOCK // PAGE_SIZE
    assert ppb == 2 and n_pages % ppb == 0 and s_len == SUBLANES and A_HEADS == SUBLANES
    nb = n_pages // ppb
    assert nb <= LANES and nb % nsteps == 0
    bps = nb // nsteps
    rows = A_HEADS * s_len
    dh = A_HEAD_DIM
    gmap = lambda j: first + j + skip * jnp.minimum(j, 1)
    tok = lambda g: pl.BlockSpec((None, None, s_len, A_WIDTH), lambda j, i, pt: (g, b0 + j, 0, 0))
    page = lambda o: pl.BlockSpec((None, None, PAGE_SIZE, A_HEADS, dh),
                                  lambda j, i, pt: (e, pt[b0 + j, ppb * bps * i + o], 0, 0, 0))
    pages = [page(o) for o in range(ppb * bps)]
    out_specs = [pl.BlockSpec((None, tm, gw), lambda j, i, pt: (j, i, 0))]
    out_shape = [jax.ShapeDtypeStruct((n_groups, m, gw), BF16)]
    args = [page_table, x_bf, wt_all, zm, zk, zv, zm, bias_s,
            *([cache_k] * (ppb * bps)), *([cache_v] * (ppb * bps))]
    extra_specs = []
    aliases = {}
    slot = None
    if heads is not None:
        assert n_groups == 1
        spec, shape, stack, slot = _heads_out_spec(heads, e, m, tm,
                                                   lambda l: (lambda j, i, pt: (l, i, 0, 0)))
        out_specs.append(spec)
        out_shape.append(shape)
        if stack is not None:
            extra_specs.append(pl.BlockSpec(memory_space=pl.ANY))
            args.append(stack)
            aliases = {len(args) - 1: 1}
    out_specs.append(pl.BlockSpec((None, s_len, A_WIDTH), lambda j, i, pt: (j, 0, 0)))
    out_shape.append(jax.ShapeDtypeStruct((n_groups, s_len, A_WIDTH), BF16))
    grid_spec = pltpu.PrefetchScalarGridSpec(
        num_scalar_prefetch=1,
        grid=(n_groups, nsteps),
        in_specs=[pl.BlockSpec((tm, k), lambda j, i, pt: (i, 0)),
                  pl.BlockSpec((None, gw, k), lambda j, i, pt: (e, gmap(j), 0)),
                  tok(0), tok(0), tok(0), tok(1),
                  pl.BlockSpec(bias_s.shape, lambda j, i, pt: (0, 0))] + pages + pages + extra_specs,
        out_specs=out_specs,
        scratch_shapes=[pltpu.VMEM((gw, k), BF16),
                        pltpu.VMEM((rows, dh), F32),
                        pltpu.VMEM((rows, LANES), F32),
                        pltpu.VMEM((rows, LANES), F32),
                        pltpu.VMEM((rows, LANES), F32),
                        pltpu.VMEM((nb, rows, dh), F32)])
    kern = functools.partial(_inproj_attn_kernel, heads_out=heads is not None, aliased=bool(aliases),
                             slot=slot, nb=nb, bps=bps, s_len=s_len, scale=dh ** -0.5)
    return pl.pallas_call(
        kern,
        grid_spec=grid_spec,
        out_shape=out_shape,
        input_output_aliases=aliases,
        compiler_params=_params("arbitrary", "arbitrary"),
        name="inproj_even_sample_attn",
    )(*args)


def _mlstm_kernel(q_ref, k_ref, v_ref, og_ref, gm_ref, g_ref, gt_ref, ng_ref, c0_ref, n0_ref, m0_ref,
                  h_ref, c_out, n_out, m_out, c_scr, n_scr, m_scr, *, L, Lp):
    c = pl.program_id(1)
    dk = B_HEAD_DIM

    @pl.when(c == 0)
    def _():
        c_scr[...] = c0_ref[...]
        n_scr[...] = n0_ref[...]
        m_scr[...] = m0_ref[...]

    def padded(x):
        if L == Lp:
            return x
        return jnp.concatenate([x, jnp.zeros((Lp - L, x.shape[1]), x.dtype)], axis=0)

    g = padded(g_ref[...])
    gt = gt_ref[...]
    r = lax.broadcasted_iota(jnp.int32, (Lp, Lp), 0)
    cc = lax.broadcasted_iota(jnp.int32, (Lp, Lp), 1)
    tri = r >= cc
    tri_t = r <= cc
    rv = lax.broadcasted_iota(jnp.int32, (Lp, 1), 0) < L
    cv = lax.broadcasted_iota(jnp.int32, (1, Lp), 1) < L

    heads = range(B_HEADS)
    cols = [slice(hh * dk, (hh + 1) * dk) for hh in heads]
    ph = []
    for hh in heads:
        q = padded(q_ref[:, cols[hh]].astype(F32))
        k = padded(k_ref[:, cols[hh]].astype(F32)) * (dk ** -0.5)
        v = padded(v_ref[:, cols[hh]].astype(F32))
        li_col = g[:, hh:hh + 1]
        li_row = gt[hh:hh + 1, :]
        lf_col = jax.nn.log_sigmoid(g[:, hh + B_HEADS:hh + B_HEADS + 1])
        lf_row = jax.nn.log_sigmoid(gt[hh + B_HEADS:hh + B_HEADS + 1, :])
        if L != Lp:
            li_col = jnp.where(rv, li_col, NEG)
            lf_col = jnp.where(rv, lf_col, 0.0)
            li_row = jnp.where(cv, li_row, NEG)
            lf_row = jnp.where(cv, lf_row, 0.0)
        b_col = jnp.sum(jnp.where(tri, lf_row, 0.0), axis=1, keepdims=True)
        b_row = jnp.sum(jnp.where(tri_t, lf_col, 0.0), axis=0, keepdims=True)
        m_prev = m_scr[hh]
        log_d = jnp.where(tri, b_col - b_row + li_row, NEG)
        m_inter = b_col + m_prev
        m_t = jnp.maximum(m_inter, jnp.max(log_d, axis=1, keepdims=True))
        ph.append(dict(q=q, k=k, vb=v.astype(BF16), li_col=li_col, li_row=li_row, b_col=b_col,
                       b_row=b_row, m_prev=m_prev, log_d=log_d, m_inter=m_inter, m_t=m_t))
    for hh in heads:
        p = ph[hh]
        qb = p['q'].astype(BF16)
        s = _nt(qb, p['k'].astype(BF16)) * jnp.exp(p['log_d'] - p['m_t'])
        w_inter = jnp.exp(p['m_inter'] - p['m_t'])
        c_old = c_scr[hh]
        n_old = n_scr[hh]
        num = (jnp.dot(s.astype(BF16), p['vb'], preferred_element_type=F32)
               + w_inter * jnp.dot(qb, c_old.astype(BF16), preferred_element_type=F32))
        den = jnp.sum(s, axis=1, keepdims=True) + w_inter * jnp.sum(p['q'] * n_old, axis=1, keepdims=True)
        p.update(h=num / jnp.maximum(jnp.abs(den), jnp.exp(-p['m_t'])), c_old=c_old, n_old=n_old)
    for hh in heads:
        p = ph[hh]
        b_last = p['b_row'][:, L - 1:L]
        m_new = jnp.maximum(b_last + p['m_prev'],
                            jnp.max(b_last - p['b_row'] + p['li_row'], axis=1, keepdims=True))
        decay = jnp.exp(b_last + p['m_prev'] - m_new)
        wk = jnp.exp(b_last - p['b_col'] + p['li_col'] - m_new)
        kw = p['k'] * wk
        c_scr[hh] = decay * p['c_old'] + lax.dot_general(kw.astype(BF16), p['vb'], (((0,), (0,)), ((), ())),
                                                         preferred_element_type=F32)
        n_scr[hh] = decay * p['n_old'] + jnp.sum(kw, axis=0, keepdims=True)
        m_scr[hh] = m_new
    for hh in heads:
        hv = ph[hh]['h'][:L]
        mu = jnp.mean(hv, axis=1, keepdims=True)
        var = jnp.mean(jnp.square(hv - mu), axis=1, keepdims=True)
        hn = (hv - mu) * lax.rsqrt(var + LN_EPS) * ng_ref[hh]
        h_ref[:, cols[hh]] = (hn * jax.nn.sigmoid(og_ref[:, cols[hh]].astype(F32))
                              * _silu(gm_ref[:, cols[hh]].astype(F32))).astype(h_ref.dtype)

    @pl.when(c == pl.num_programs(1) - 1)
    def _():
        c_out[...] = c_scr[...]
        n_out[...] = n_scr[...]
        m_out[...] = m_scr[...]


def _mlstm(zm, gates, norm_g, c0, n0, m0, bsz, t, L):
    Lp = max(L, LANES)
    nc = t // L
    dh = B_HEAD_DIM
    g3 = gates.reshape(bsz, t, LANES)
    gt = jnp.swapaxes(gates[:, :2 * B_HEADS].reshape(bsz, nc, L, 2 * B_HEADS), 2, 3)
    if Lp != L:
        gt = jnp.pad(gt, ((0, 0), (0, 0), (0, 0), (0, Lp - L)))
    ng = norm_g.reshape(B_HEADS, 1, dh).astype(F32)
    n0 = n0.reshape(bsz, B_HEADS, 1, dh)
    m0 = m0.reshape(bsz, B_HEADS, 1, 1)
    kern = functools.partial(_mlstm_kernel, L=L, Lp=Lp)
    zspec = lambda gidx: pl.BlockSpec((None, None, L, B_WIDTH), lambda b, c: (gidx, b, c, 0))
    st = lambda r, cdim: pl.BlockSpec((None, B_HEADS, r, cdim), lambda b, c: (b, 0, 0, 0))
    h, c1, n1, m1 = pl.pallas_call(
        kern,
        grid=(bsz, nc),
        in_specs=[zspec(2), zspec(3), zspec(4), zspec(5), zspec(6),
                  pl.BlockSpec((None, L, LANES), lambda b, c: (b, c, 0)),
                  pl.BlockSpec((None, None, 2 * B_HEADS, Lp), lambda b, c: (b, c, 0, 0)),
                  pl.BlockSpec((B_HEADS, 1, dh), lambda b, c: (0, 0, 0)),
                  st(dh, dh), st(1, dh), st(1, 1)],
        out_specs=[pl.BlockSpec((None, L, B_WIDTH), lambda b, c: (b, c, 0)),
                   st(dh, dh), st(1, dh), st(1, 1)],
        out_shape=[jax.ShapeDtypeStruct((bsz, t, B_WIDTH), BF16),
                   jax.ShapeDtypeStruct((bsz, B_HEADS, dh, dh), F32),
                   jax.ShapeDtypeStruct((bsz, B_HEADS, 1, dh), F32),
                   jax.ShapeDtypeStruct((bsz, B_HEADS, 1, 1), F32)],
        scratch_shapes=[pltpu.VMEM((B_HEADS, dh, dh), F32), pltpu.VMEM((B_HEADS, 1, dh), F32),
                        pltpu.VMEM((B_HEADS, 1, 1), F32)],
        compiler_params=_params("parallel", "arbitrary"),
        name="mlstm",
    )(zm, zm, zm, zm, zm, g3, gt, ng, c0, n0, m0)
    return h, c1, n1.reshape(bsz, B_HEADS, dh), m1.reshape(bsz, B_HEADS)


def _outln_kernel(*refs, n_parts):
    mix = refs[:n_parts]
    wf = refs[n_parts:2 * n_parts]
    x_ref, g_ref, b_ref, o_ref, ob_ref = refs[2 * n_parts:2 * n_parts + 5]
    ws = refs[2 * n_parts + 5:]

    @pl.when(pl.program_id(0) == 0)
    def _():
        for p in range(n_parts):
            ws[p][...] = wf[p][...].astype(BF16)

    tm = x_ref.shape[0]
    sub = min(tm, LANES)
    for r in range(tm // sub):
        rows = pl.ds(r * sub, sub)
        y = jnp.dot(mix[0][rows, :], ws[0][...], preferred_element_type=F32)
        for p in range(1, n_parts):
            y = y + jnp.dot(mix[p][rows, :], ws[p][...], preferred_element_type=F32)
        z = ALPHA * x_ref[rows, :] + y
        mu = jnp.mean(z, axis=1, keepdims=True)
        var = jnp.mean(jnp.square(z - mu), axis=1, keepdims=True)
        xn = (z - mu) * lax.rsqrt(var + LN_EPS) * g_ref[...] + b_ref[...]
        o_ref[rows, :] = xn
        ob_ref[rows, :] = xn.astype(BF16)


def _out_ln(mix_parts, w_all, layer, x, ln_g, ln_b, tm):
    m, d = x.shape
    n_parts = len(mix_parts)
    kp = mix_parts[0].shape[1]
    assert all(mp.shape[1] == kp for mp in mix_parts) and n_parts * kp == w_all.shape[1]
    wspec = lambda p: pl.BlockSpec((None, kp, d), lambda i: (layer, p, 0), pipeline_mode=pl.Buffered(1))
    in_specs = ([pl.BlockSpec((tm, kp), lambda i: (i, 0)) for _ in mix_parts]
                + [wspec(p) for p in range(n_parts)]
                + [pl.BlockSpec((tm, d), lambda i: (i, 0)),
                   pl.BlockSpec((1, d), lambda i: (0, 0)),
                   pl.BlockSpec((1, d), lambda i: (0, 0))])
    return pl.pallas_call(
        functools.partial(_outln_kernel, n_parts=n_parts),
        grid=(m // tm,),
        in_specs=in_specs,
        out_specs=[pl.BlockSpec((tm, d), lambda i: (i, 0)), pl.BlockSpec((tm, d), lambda i: (i, 0))],
        out_shape=[jax.ShapeDtypeStruct((m, d), F32), jax.ShapeDtypeStruct((m, d), BF16)],
        scratch_shapes=[pltpu.VMEM((kp, d), BF16) for _ in mix_parts],
        compiler_params=_params("arbitrary"),
        name="outproj_ln",
    )(*mix_parts, *([w_all] * n_parts), x, ln_g.reshape(1, d).astype(F32), ln_b.reshape(1, d).astype(F32))


def _odd_kernel(*refs, tm, carry_mode, seq_tiles, seq_rows):
    if carry_mode:
        x_ref, wb, wc, wx, wg, cw_ref, mix_ref, tail_ref, wbf, carry = refs
    else:
        x_ref, wb, wc, wx, wg, cw_ref, plast_ref, pprev_ref, mix_ref, tail_ref, wbf = refs

    @pl.when(pl.program_id(1) == 0)
    def _():
        for gidx, wref in enumerate((wb, wc, wx, wg)):
            wbf[gidx] = wref[...].astype(BF16)

    if carry_mode:
        @pl.when(pl.program_id(1) % seq_tiles == 0)
        def _():
            carry[...] = jnp.zeros_like(carry)
        last = carry[SUBLANES - 1:SUBLANES, :]
        prev = carry[SUBLANES - 2:SUBLANES - 1, :]

    cw = cw_ref[...]
    sub = min(tm, 4 * LANES)
    for r in range(tm // sub):
        rs = pl.ds(r * sub, sub)
        x = x_ref[rs, :]
        zb = jnp.dot(x, wbf[0], preferred_element_type=F32)
        zc = jnp.dot(x, wbf[1], preferred_element_type=F32)
        zx = jnp.dot(x, wbf[2], preferred_element_type=F32)
        zg = jnp.dot(x, wbf[3], preferred_element_type=F32)
        u = zc * zx
        rows = lax.broadcasted_iota(jnp.int32, u.shape, 0)
        r1 = pltpu.roll(u, 1, 0)
        r2 = pltpu.roll(u, 2, 0)
        if carry_mode:
            rm = rows
        else:
            last = plast_ref[rs, :]
            prev = pprev_ref[rs, :]
            rm = rows & (seq_rows - 1)
        u1 = jnp.where(rm == 0, last, r1)
        u2 = jnp.where(rm == 0, prev, jnp.where(rm == 1, last, r2))
        y = cw[0:1, :] * u2 + cw[1:2, :] * u1 + cw[2:3, :] * u
        mix_ref[rs, :] = (zb * y * _silu(zg)).astype(mix_ref.dtype)
        if carry_mode:
            last = u[sub - 1:sub, :]
            prev = u[sub - 2:sub - 1, :]
        else:
            tail_ref[rs, :] = u
    if carry_mode:
        tail = u[sub - SUBLANES:, :]
        carry[...] = tail
        tail_ref[...] = tail


def _odd_mix(x_bf, w_all, o, conv_w, tm, tn, seq_len, state=None):
    m, k = x_bf.shape
    nn = C_WIDTH // tn
    carry_mode = state is None
    wspec = lambda gidx: pl.BlockSpec((None, k, tn), lambda j, i: (o, 0, gidx * nn + j))
    in_specs = [pl.BlockSpec((tm, k), lambda j, i: (i, 0)),
                wspec(0), wspec(1), wspec(2), wspec(3),
                pl.BlockSpec((CONV_K, tn), lambda j, i: (0, j))]
    args = [x_bf, w_all, w_all, w_all, w_all, conv_w.astype(F32)]
    scratch = [pltpu.VMEM((4, k, tn), BF16)]
    if carry_mode:
        assert seq_len % tm == 0
        tail_shape = jax.ShapeDtypeStruct((m // tm, SUBLANES, C_WIDTH), F32)
        tail_spec = pl.BlockSpec((None, SUBLANES, tn), lambda j, i: (i, 0, j))
        scratch.append(pltpu.VMEM((SUBLANES, tn), F32))
    else:
        assert tm % seq_len == 0 and seq_len & (seq_len - 1) == 0
        in_specs += [pl.BlockSpec((tm, tn), lambda j, i: (i, j))] * 2
        args += list(state)
        tail_shape = jax.ShapeDtypeStruct((m, C_WIDTH), F32)
        tail_spec = pl.BlockSpec((tm, tn), lambda j, i: (i, j))
    kern = functools.partial(_odd_kernel, tm=tm, carry_mode=carry_mode,
                             seq_tiles=max(seq_len // tm, 1), seq_rows=seq_len)
    return pl.pallas_call(
        kern,
        grid=(nn, m // tm),
        in_specs=in_specs,
        out_specs=[pl.BlockSpec((tm, tn), lambda j, i: (i, j)), tail_spec],
        out_shape=[jax.ShapeDtypeStruct((m, C_WIDTH), BF16), tail_shape],
        scratch_shapes=scratch,
        compiler_params=_params("parallel", "arbitrary"),
        name="odd_mix",
    )(*args)


def _row_tiles(m):
    return min(m, 1024), min(m, 512)


def _project_even(x_bf, wt_all, e, bsz, t, kv_stacks, sample=None):
    n_even = wt_all.shape[0]
    hk, hv = ((n_even, st) for st in kv_stacks)
    m = bsz * t
    tm, _ = _row_tiles(m)
    ng = N_GROUPS_EVEN - 2
    if sample is None:
        (zm,) = _inproj_even(x_bf, wt_all, e, (0, 2, ng), tm)
        zk, ka = _inproj_even(x_bf, wt_all, e, (1, 0, 1), tm, heads=hk)
        attn_s = None
    else:
        n_seq = sample[0].shape[1]
        assert n_seq == ng + 1
        tm_a = tm // 2
        zm, attn_a = _inproj_even_with_attn(x_bf, wt_all, e, (0, 2, ng), tm_a, None, sample, 0)
        zk, ka, attn_b = _inproj_even_with_attn(x_bf, wt_all, e, (1, 0, 1), tm_a, hk, sample, ng)
        attn_s = jnp.concatenate([attn_a, attn_b], axis=0)
    zv, va = _inproj_even(x_bf, wt_all, e, (2, 0, 1), tm, heads=hv)
    return (zm.reshape(ng, bsz, t, 1024), zk.reshape(1, bsz, t, 1024), zv.reshape(1, bsz, t, 1024),
            (ka, va), attn_s)


def _finish_even(x, bsz, t, zm, attn, gates, w_out_all, e, norm_g, ln_g, ln_b, state, L):
    _, tm_out = _row_tiles(bsz * t)
    hm, c1, n1, m1 = _mlstm(zm, gates, norm_g, *state, bsz, t, L)
    xn, xn_bf = _out_ln([attn.reshape(bsz * t, A_WIDTH), hm.reshape(bsz * t, B_WIDTH)],
                        w_out_all, e, x, ln_g, ln_b, tm_out)
    return xn, xn_bf, (c1, n1, m1)


def _even_layers(xp, xp_bf, bp, tp, xs, xs_bf, bs, ts, wt_all, e, gbias, w_out_all, norm_g,
                 ln_g, ln_b, bias_p, bias_s, cache_k, cache_v, page_table, state_s, kv_p, kv_s):
    gates_p, xp_bf = _gates(xp if xp_bf is None else xp_bf, wt_all, e, gbias, _row_tiles(bp * tp)[0])
    gates_s, xs_bf = _gates(xs if xs_bf is None else xs_bf, wt_all, e, gbias, _row_tiles(bs * ts)[0])
    zm_s, zk_s, zv_s, kv_s, _ = _project_even(xs_bf, wt_all, e, bs, ts, kv_s)
    sample = (zm_s, zk_s, zv_s, bias_s, cache_k, cache_v, page_table, ts)
    zm_p, zk_p, zv_p, kv_p, attn_s = _project_even(xp_bf, wt_all, e, bp, tp, kv_p, sample)
    attn_p = _moba_prompt(zm_p, zk_p, zv_p, bias_p, bp, tp)
    zeros = (jnp.zeros((bp, B_HEADS, B_HEAD_DIM, B_HEAD_DIM), F32),
             jnp.zeros((bp, B_HEADS, B_HEAD_DIM), F32), jnp.zeros((bp, B_HEADS), F32))
    xp, xp_bf, st_p = _finish_even(xp, bp, tp, zm_p, attn_p, gates_p, w_out_all, e,
                                   norm_g, ln_g, ln_b, zeros, math.gcd(tp, MLSTM_PROMPT_CHUNK))
    xs, xs_bf, st_s = _finish_even(xs, bs, ts, zm_s, attn_s, gates_s, w_out_all, e,
                                   norm_g, ln_g, ln_b, state_s, math.gcd(ts, MLSTM_CHUNK))
    return (xp, xp_bf, kv_p, st_p), (xs, xs_bf, kv_s, st_s)


def _odd_layer(x, x_bf, bsz, t, w_in_all, o, w_out_all, conv_w, ln_g, ln_b, buf):
    tm, tm_out = _row_tiles(bsz * t)
    if buf is None:
        tm = math.gcd(2 * tm, t)
        mix, tail = _odd_mix(x_bf, w_in_all, o, conv_w, tm, 256, t)
        seq_tiles = t // tm
        new_buf = tail.reshape(bsz, seq_tiles, SUBLANES, C_WIDTH)[:, -1, SUBLANES - (CONV_K - 1):, :]
    else:
        plast = jnp.repeat(buf[:, 1, :], t, axis=0)
        pprev = jnp.repeat(buf[:, 0, :], t, axis=0)
        mix, u = _odd_mix(x_bf, w_in_all, o, conv_w, tm, 256, t, state=(plast, pprev))
        new_buf = u.reshape(bsz, t, C_WIDTH)[:, t - (CONV_K - 1):, :]
    xn, xn_bf = _out_ln([mix], w_out_all, o, x, ln_g, ln_b, tm_out)
    return xn, xn_bf, new_buf


def kernel(x_prompt, x_sample, cache_k, cache_v, page_table, state_C, state_n, state_m, state_conv,
           w_in_even, w_out_even, mlstm_gate_bias, mlstm_norm_g, rel_bias,
           w_in_odd, w_out_odd, conv_w, ln_g, ln_b):
    bp, tp, d = x_prompt.shape
    bs, ts, _ = x_sample.shape
    bias_p = _bias_tiles(rel_bias, _prompt_bias_idx())
    bias_s = _bias_tiles(rel_bias, _sample_bias_idx(ts)).reshape(A_HEADS * ts, -1)

    xp = x_prompt.reshape(bp * tp, d)
    xs = x_sample.reshape(bs * ts, d)
    xp_bf = xs_bf = None
    wt_even = jnp.swapaxes(w_in_even, 1, 2)

    kv_p = kv_s = (None, None)
    pc, pn, pm, pb = [], [], [], []
    sc, sn, sm, sb = [], [], [], []
    for layer in range(DEPTH):
        if layer % 2 == 0:
            e = layer // 2
            gbias = jnp.pad(mlstm_gate_bias[e], (0, LANES - 2 * B_HEADS)).reshape(1, LANES).astype(F32)
            (xp, xp_bf, kv_p, (c1, n1, m1)), (xs, xs_bf, kv_s, (c2, n2, m2)) = _even_layers(
                xp, xp_bf, bp, tp, xs, xs_bf, bs, ts, wt_even, e, gbias, w_out_even,
                mlstm_norm_g[e], ln_g[layer], ln_b[layer], bias_p, bias_s, cache_k, cache_v, page_table,
                (state_C[e], state_n[e], state_m[e]), kv_p, kv_s)
            pc.append(c1); pn.append(n1); pm.append(m1)
            sc.append(c2); sn.append(n2); sm.append(m2)
        else:
            o = layer // 2
            xp, xp_bf, b1 = _odd_layer(xp, xp_bf, bp, tp, w_in_odd, o, w_out_odd, conv_w[o],
                                       ln_g[layer], ln_b[layer], None)
            xs, xs_bf, b2 = _odd_layer(xs, xs_bf, bs, ts, w_in_odd, o, w_out_odd, conv_w[o],
                                       ln_g[layer], ln_b[layer], state_conv[o])
            pb.append(b1); sb.append(b2)
    n_even = w_in_even.shape[0]
    heads = lambda st, b, t: st.reshape(n_even, b, t, A_HEADS, A_HEAD_DIM)
    return (xp.reshape(bp, tp, d), xs.reshape(bs, ts, d),
            heads(kv_p[0], bp, tp), heads(kv_p[1], bp, tp),
            jnp.stack(pc), jnp.stack(pn), jnp.stack(pm), jnp.stack(pb),
            heads(kv_s[0], bs, ts), heads(kv_s[1], bs, ts),
            jnp.stack(sc), jnp.stack(sn), jnp.stack(sm), jnp.stack(sb))
```
